```python
import math
import numpy as np
import jax
import jax.numpy as jnp
from jax import lax

D_MODEL = 4096
BATCH = 4
SEQ = 2048
DEPTH = 4
DEC_BATCH = 128
DEC_SEQ = 8
PAST_LEN = 16384
PAGE_SIZE = 128

N_MEM = 256
GROUP_W = D_MODEL // 4
HG_HEADS = 8
HG_DK = GROUP_W // HG_HEADS
HG_DV = GROUP_W // HG_HEADS
ML_HEADS = 4
ML_DV = GROUP_W // ML_HEADS
ML_DK = ML_DV // 2
RW_HEAD = 64
RW_HEADS = GROUP_W // RW_HEAD
RW_LORA_W = 64
RW_LORA_A = 64
XA_HEADS = 4
XA_DH = GROUP_W // XA_HEADS
CHUNK = 64
HG_IN = 4 * GROUP_W
ML_QK = ML_HEADS * ML_DK
ML_IN = 2 * ML_QK + 2 * GROUP_W + 2 * ML_HEADS
RW_SHIFT_W = 3 * GROUP_W + RW_LORA_W + RW_LORA_A
RW_IN = RW_SHIFT_W + GROUP_W
XA_IN = 2 * GROUP_W
N_IN = HG_IN + ML_IN + RW_IN + XA_IN
DN_ALPHA = (2.0 * DEPTH) ** 0.25
DN_BETA = (8.0 * DEPTH) ** -0.25
LN_EPS = 1e-5
RW_GN_EPS = 64e-5

kernel_name = 'hybrid_hgrn2_mlstm_rwkv7_memxattn_step'


def _split(x, sizes):
    return jnp.split(x, np.cumsum(sizes)[:-1].tolist(), axis=-1)


def _chunk(x, L):
    B, T = x.shape[:2]
    return jnp.moveaxis(x.reshape((B, T // L, L) + x.shape[2:]), 1, 0)


def _unchunk(x):
    x = jnp.moveaxis(x, 0, 1)
    return x.reshape((x.shape[0], x.shape[1] * x.shape[2]) + x.shape[3:])


def _norm_heads(y, eps, center):
    if center:
        y = y - jnp.mean(y, axis=-1, keepdims=True)
    y = y * lax.rsqrt(jnp.mean(y * y, axis=-1, keepdims=True) + eps)
    return y.reshape(y.shape[:2] + (-1,))


def _layer_norm(x, g, b):
    xf = x.astype(jnp.float32)
    xf = xf - jnp.mean(xf, axis=-1, keepdims=True)
    xf = xf * lax.rsqrt(jnp.mean(xf * xf, axis=-1, keepdims=True) + LN_EPS)
    return (xf * g.astype(jnp.float32) + b.astype(jnp.float32)).astype(x.dtype)


def _hgrn2(q, fpre, inp, lb, S0):
    f32 = jnp.float32
    B, T, _ = q.shape
    L = math.gcd(T, CHUNK)
    shp = (B, T, HG_HEADS, HG_DK)
    q = jax.nn.silu(q.astype(f32)).reshape(shp)
    lbf = lb.astype(f32)
    logf = jnp.logaddexp(jnp.log(lbf), jnp.log1p(-lbf) + jax.nn.log_sigmoid(fpre.astype(f32))).reshape(shp)
    k = -jnp.expm1(logf)
    v = inp.astype(f32).reshape(B, T, HG_HEADS, HG_DV)
    mask = jnp.tril(jnp.ones((L, L), bool))[None, :, :, None, None]

    def step(S, xs):
        qc, kc, vc, gc = xs
        b = jnp.cumsum(gc, axis=1)
        diff = jnp.where(mask, b[:, :, None] - b[:, None, :], -jnp.inf)
        A = jnp.einsum('bthc,bshc,btshc->bths', qc, kc, jnp.exp(diff))
        o = jnp.einsum('bths,bshv->bthv', A, vc) + jnp.einsum('bthc,bhcv->bthv', qc * jnp.exp(b), S)
        bL = b[:, -1]
        S = jnp.exp(bL)[..., None] * S + jnp.einsum('bshc,bshv->bhcv', kc * jnp.exp(bL[:, None] - b), vc)
        return S, o

    S, o = lax.scan(step, S0.astype(f32), (_chunk(q, L), _chunk(k, L), _chunk(v, L), _chunk(logf, L)))
    return _unchunk(o), S


def _mlstm(q, k, v, ig, fg, C0, n0, m0):
    f32 = jnp.float32
    B, T = q.shape[:2]
    L = math.gcd(T, CHUNK)
    q = q.astype(f32).reshape(B, T, ML_HEADS, ML_DK)
    k = k.astype(f32).reshape(B, T, ML_HEADS, ML_DK) * (ML_DK ** -0.5)
    v = v.astype(f32).reshape(B, T, ML_HEADS, ML_DV)
    ig = ig.astype(f32)
    logf = jax.nn.log_sigmoid(fg.astype(f32))
    mask = jnp.tril(jnp.ones((L, L), bool))[None, :, :, None]

    def step(carry, xs):
        C, n, m = carry
        qc, kc, vc, ic, gc = xs
        b = jnp.cumsum(gc, axis=1)
        D = jnp.where(mask, b[:, :, None] - b[:, None, :] + ic[:, None], -jnp.inf)
        m_t = jnp.maximum(b + m[:, None], jnp.max(D, axis=2))
        P = jnp.exp(D - m_t[:, :, None]) * jnp.einsum('bthc,bshc->btsh', qc, kc)
        inter = jnp.exp(b + m[:, None] - m_t)
        num = jnp.einsum('btsh,bshv->bthv', P, vc) + inter[..., None] * jnp.einsum('bthc,bhcv->bthv', qc, C)
        den = jnp.sum(P, axis=2) + inter * jnp.einsum('bthc,bhc->bth', qc, n)
        h = num / jnp.maximum(jnp.abs(den), jnp.exp(-m_t))[..., None]
        m_new = m_t[:, -1]
        bL = b[:, -1]
        wgt = jnp.exp(bL[:, None] - b + ic - m_new[:, None])
        decay = jnp.exp(bL + m - m_new)
        C = decay[..., None, None] * C + jnp.einsum('bsh,bshc,bshv->bhcv', wgt, kc, vc)
        n = decay[..., None] * n + jnp.einsum('bsh,bshc->bhc', wgt, kc)
        return (C, n, m_new), h

    (C, n, m), h = lax.scan(step, (C0.astype(f32), n0.astype(f32), m0.astype(f32)),
                            (_chunk(q, L), _chunk(k, L), _chunk(v, L), _chunk(ig, L), _chunk(logf, L)))
    return _unchunk(h), C, n, m


def _rwkv7(p, buf, mu, w0, w2, a0, a2, k_k, k_a, r_k, ln_w, ln_b, S0):
    f32 = jnp.float32
    B, T, _ = p.shape
    p = p.astype(f32)
    prev = jnp.concatenate([buf.astype(f32)[:, None], p[:, :-1]], axis=1)
    xs = p + (prev - p) * mu.astype(f32)
    r, k, v, xw, xa = _split(xs, [GROUP_W, GROUP_W, GROUP_W, RW_LORA_W, RW_LORA_A])
    w = -jax.nn.softplus(-(w0.astype(f32) + jnp.tanh(xw) @ w2.astype(f32))) - 0.5
    decay = jnp.exp(-jnp.exp(w))
    a = jax.nn.sigmoid(a0.astype(f32) + xa @ a2.astype(f32))
    shp = (B, T, RW_HEADS, RW_HEAD)
    kk = (k * k_k.astype(f32)).reshape(shp)
    kk = kk / jnp.maximum(jnp.sqrt(jnp.sum(kk * kk, axis=-1, keepdims=True)), 1e-12)
    k = k * (1.0 + (a - 1.0) * k_a.astype(f32))
    r, k, v, decay, a = (t.reshape(shp) for t in (r, k, v, decay, a))

    def step(S, xs_t):
        rt, kt, vt, dt, at, kkt = xs_t
        S = (S * dt[:, :, None, :]
             - jnp.einsum('bhij,bhj->bhi', S, kkt)[..., None] * (kkt * at)[:, :, None, :]
             + vt[..., None] * kt[:, :, None, :])
        return S, jnp.einsum('bhij,bhj->bhi', S, rt)

    S, y = lax.scan(step, S0.astype(f32), tuple(jnp.moveaxis(t, 1, 0) for t in (r, k, v, decay, a, kk)))
    y = jnp.moveaxis(y, 0, 1)
    y = _norm_heads(y, RW_GN_EPS, True) * ln_w.astype(f32) + ln_b.astype(f32)
    bonus = jnp.sum(r * k * r_k.astype(f32), axis=-1, keepdims=True) * v
    return y + bonus.reshape(B, T, GROUP_W), S, p[:, -1]


def _mem_attn(q, mk, mv):
    f32 = jnp.float32
    B, T = q.shape[:2]
    q = q.astype(f32).reshape(B, T, XA_HEADS, XA_DH)
    s = jnp.einsum('bthd,bmhd->bhtm', q, mk.astype(f32)) * (XA_DH ** -0.5)
    pr = jax.nn.softmax(s, axis=-1)
    return jnp.einsum('bhtm,bmhd->bthd', pr, mv.astype(f32)).reshape(B, T, GROUP_W)


def _trunk(x, hg_S, ml_C, ml_n, ml_m, rw_S, rw_buf, mem_k, mem_v, params):
    (w_in, hgrn_lb, hgrn_norm_w, mlstm_ig_b, mlstm_fg_b, mlstm_norm_w, rwkv_mu, rwkv_w0, rwkv_w2,
     rwkv_a0, rwkv_a2, rwkv_k_k, rwkv_k_a, rwkv_r_k, rwkv_ln_w, rwkv_ln_b, w_out, ln_g, ln_b) = params
    f32 = jnp.float32
    dt = x.dtype
    lb_all = jnp.cumsum(jax.nn.softmax(hgrn_lb.astype(f32), axis=0), axis=0)
    lb_all = lb_all - lb_all[0]
    o_hgS, o_C, o_n, o_m, o_rwS, o_buf = [], [], [], [], [], []
    for l in range(DEPTH):
        p = x @ w_in[l]
        p_hg, p_ml, p_rw, p_xa = _split(p, [HG_IN, ML_IN, RW_IN, XA_IN])
        hq, hf, hi, hz = _split(p_hg, [GROUP_W, GROUP_W, GROUP_W, GROUP_W])
        h_hg, S_hg = _hgrn2(hq, hf, hi, lb_all[l], hg_S[l])
        o_hg = _norm_heads(h_hg, 1e-5, False) * hgrn_norm_w[l].astype(f32) * jax.nn.silu(hz.astype(f32))
        mq, mk, mv, mi, mf, mz = _split(p_ml, [ML_QK, ML_QK, GROUP_W, ML_HEADS, ML_HEADS, GROUP_W])
        h_ml, C_ml, n_ml, m_ml = _mlstm(mq, mk, mv, mi + mlstm_ig_b[l], mf + mlstm_fg_b[l], ml_C[l], ml_n[l], ml_m[l])
        o_ml = _norm_heads(h_ml, 1e-6, True) * mlstm_norm_w[l].astype(f32) * jax.nn.silu(mz.astype(f32))
        prw, rz = _split(p_rw, [RW_SHIFT_W, GROUP_W])
        y_rw, S_rw, buf_rw = _rwkv7(prw, rw_buf[l], rwkv_mu[l], rwkv_w0[l], rwkv_w2[l], rwkv_a0[l], rwkv_a2[l],
                                    rwkv_k_k[l], rwkv_k_a[l], rwkv_r_k[l], rwkv_ln_w[l], rwkv_ln_b[l], rw_S[l])
        o_rw = y_rw * jax.nn.silu(rz.astype(f32))
        xq, xz = _split(p_xa, [GROUP_W, GROUP_W])
        o_xa = _mem_attn(xq, mem_k[l], mem_v[l]) * jax.nn.silu(xz.astype(f32))
        mix = jnp.concatenate([o_hg, o_ml, o_rw, o_xa], axis=-1).astype(dt)
        x = _layer_norm(DN_ALPHA * x + mix @ w_out[l], ln_g[l], ln_b[l])
        o_hgS.append(S_hg.astype(hg_S.dtype))
        o_C.append(C_ml.astype(ml_C.dtype))
        o_n.append(n_ml.astype(ml_n.dtype))
        o_m.append(m_ml.astype(ml_m.dtype))
        o_rwS.append(S_rw.astype(rw_S.dtype))
        o_buf.append(buf_rw.astype(rw_buf.dtype))
    return (x, jnp.stack(o_hgS), jnp.stack(o_C), jnp.stack(o_n), jnp.stack(o_m), jnp.stack(o_rwS), jnp.stack(o_buf))


def setup_inputs(seed: int = 0) -> dict:
    key = jax.random.key(seed)
    ks = jax.random.split(key, 40)
    f32 = jnp.float32

    def nrm(i, shape, s):
        return s * jax.random.normal(ks[i], shape, f32)

    def uni(i, shape, lo, hi):
        return jax.random.uniform(ks[i], shape, f32, lo, hi)

    return {
        'x_prompt': nrm(0, (BATCH, SEQ, D_MODEL), 1.0),
        'x_sample': nrm(1, (DEC_BATCH, DEC_SEQ, D_MODEL), 1.0),
        'mem_prompt': nrm(2, (BATCH, N_MEM, D_MODEL), 1.0),
        'state_hgrn': nrm(3, (DEPTH, DEC_BATCH, HG_HEADS, HG_DK, HG_DV), 0.5),
        'state_mlstm_C': nrm(4, (DEPTH, DEC_BATCH, ML_HEADS, ML_DK, ML_DV), 0.1),
        'state_mlstm_n': nrm(5, (DEPTH, DEC_BATCH, ML_HEADS, ML_DK), 0.1),
        'state_mlstm_m': uni(6, (DEPTH, DEC_BATCH, ML_HEADS), 0.0, 3.0),
        'state_rwkv': nrm(7, (DEPTH, DEC_BATCH, RW_HEADS, RW_HEAD, RW_HEAD), 0.1),
        'state_rwkv_shift': nrm(8, (DEPTH, DEC_BATCH, RW_SHIFT_W), 1.0),
        'cache_mem_k': nrm(9, (DEPTH, DEC_BATCH, N_MEM, XA_HEADS, XA_DH), 1.0),
        'cache_mem_v': nrm(10, (DEPTH, DEC_BATCH, N_MEM, XA_HEADS, XA_DH), 1.0),
        'w_in': nrm(11, (DEPTH, D_MODEL, N_IN), D_MODEL ** -0.5),
        'hgrn_lb': nrm(12, (DEPTH, GROUP_W), 0.5),
        'hgrn_norm_w': 1.0 + nrm(13, (DEPTH, GROUP_W), 0.02),
        'mlstm_ig_b': nrm(14, (DEPTH, ML_HEADS), 0.1),
        'mlstm_fg_b': jnp.linspace(3.0, 6.0, ML_HEADS, dtype=f32)[None, :] + nrm(15, (DEPTH, ML_HEADS), 0.1),
        'mlstm_norm_w': 1.0 + nrm(16, (DEPTH, GROUP_W), 0.02),
        'rwkv_mu': uni(17, (DEPTH, RW_SHIFT_W), 0.0, 1.0),
        'rwkv_w0': uni(18, (DEPTH, GROUP_W), -6.0, 1.0),
        'rwkv_w2': nrm(19, (DEPTH, RW_LORA_W, GROUP_W), 0.1),
        'rwkv_a0': nrm(20, (DEPTH, GROUP_W), 0.1),
        'rwkv_a2': nrm(21, (DEPTH, RW_LORA_A, GROUP_W), 0.1),
        'rwkv_k_k': 0.85 + nrm(22, (DEPTH, GROUP_W), 0.02),
        'rwkv_k_a': 1.0 + nrm(23, (DEPTH, GROUP_W), 0.02),
        'rwkv_r_k': nrm(24, (DEPTH, RW_HEADS, RW_HEAD), 0.1),
        'rwkv_ln_w': 1.0 + nrm(25, (DEPTH, GROUP_W), 0.02),
        'rwkv_ln_b': nrm(26, (DEPTH, GROUP_W), 0.02),
        'mem_wk': nrm(27, (DEPTH, D_MODEL, GROUP_W), D_MODEL ** -0.5),
        'mem_wv': nrm(28, (DEPTH, D_MODEL, GROUP_W), D_MODEL ** -0.5),
        'w_out': nrm(29, (DEPTH, D_MODEL, D_MODEL), DN_BETA * D_MODEL ** -0.5),
        'ln_g': 1.0 + nrm(30, (DEPTH, D_MODEL), 0.02),
        'ln_b': nrm(31, (DEPTH, D_MODEL), 0.02),
    }


def reference(x_prompt, x_sample, mem_prompt, state_hgrn, state_mlstm_C, state_mlstm_n, state_mlstm_m,
              state_rwkv, state_rwkv_shift, cache_mem_k, cache_mem_v, w_in, hgrn_lb, hgrn_norm_w,
              mlstm_ig_b, mlstm_fg_b, mlstm_norm_w, rwkv_mu, rwkv_w0, rwkv_w2, rwkv_a0, rwkv_a2,
              rwkv_k_k, rwkv_k_a, rwkv_r_k, rwkv_ln_w, rwkv_ln_b, mem_wk, mem_wv, w_out, ln_g, ln_b):
    params = (w_in, hgrn_lb, hgrn_norm_w, mlstm_ig_b, mlstm_fg_b, mlstm_norm_w, rwkv_mu, rwkv_w0, rwkv_w2,
              rwkv_a0, rwkv_a2, rwkv_k_k, rwkv_k_a, rwkv_r_k, rwkv_ln_w, rwkv_ln_b, w_out, ln_g, ln_b)
    B = x_prompt.shape[0]
    mk_p = jnp.einsum('bmd,ldk->lbmk', mem_prompt, mem_wk).reshape(DEPTH, B, N_MEM, XA_HEADS, XA_DH)
    mv_p = jnp.einsum('bmd,ldk->lbmk', mem_prompt, mem_wv).reshape(DEPTH, B, N_MEM, XA_HEADS, XA_DH)

    def zeros_like_state(ref):
        return jnp.zeros((DEPTH, B) + ref.shape[2:], ref.dtype)

    y_prompt, p_hg, p_C, p_n, p_m, p_rw, p_buf = _trunk(
        x_prompt, zeros_like_state(state_hgrn), zeros_like_state(state_mlstm_C), zeros_like_state(state_mlstm_n),
        zeros_like_state(state_mlstm_m), zeros_like_state(state_rwkv), zeros_like_state(state_rwkv_shift),
        mk_p, mv_p, params)
    y_sample, s_hg, s_C, s_n, s_m, s_rw, s_buf = _trunk(
        x_sample, state_hgrn, state_mlstm_C, state_mlstm_n, state_mlstm_m, state_rwkv, state_rwkv_shift,
        cache_mem_k, cache_mem_v, params)
    return (y_prompt, y_sample, p_hg, p_C, p_n, p_m, p_rw, p_buf, mk_p, mv_p, s_hg, s_C, s_n, s_m, s_rw, s_buf)
```

```python
import functools
import math

import jax
import jax.numpy as jnp
from jax import lax
from jax.experimental import pallas as pl
from jax.experimental.pallas import tpu as pltpu

F32 = jnp.float32
BF16 = jnp.bfloat16
HI = lax.Precision.HIGHEST

D_MODEL = 4096
DEPTH = 4
GROUP_W = D_MODEL // 4
N_MEM = 256
HG_HEADS, HG_D = 8, 128
ML_HEADS, ML_DK, ML_DV = 4, 128, 256
RW_HEADS, RW_HEAD, RW_LORA = 16, 64, 64
RW_PAIRS = RW_HEADS // 2
XA_HEADS, XA_DH = 4, 256
RW_SHIFT_W = 3 * GROUP_W + 2 * RW_LORA
N_IN = 13448
DN_ALPHA = (2.0 * DEPTH) ** 0.25
LN_EPS = 1e-5
RW_GN_EPS = 64e-5

C_HQ, C_HF, C_HI, C_HZ = 0, 1024, 2048, 3072
C_MQK, C_MV, C_MZ = 4096, 5120, 6144
C_RR, C_RK, C_RV, C_RZ = 7168, 8192, 9216, 10240
C_XQ, C_XZ = 11264, 12288
C_RX = 13312
C_MG = 13440
NP = 13568
LANE = 128

VMEM_LIMIT = 56 * 1024 * 1024

NT_DIMS = (((1,), (1,)), ((), ()))
TN_DIMS = (((0,), (0,)), ((), ()))


def _cparams(sem):
    return pltpu.CompilerParams(dimension_semantics=sem, vmem_limit_bytes=VMEM_LIMIT)


def _sigmoid(x):
    return jax.nn.sigmoid(x)


def _silu(x):
    return x * _sigmoid(x)


def _log_sigmoid(x):
    return jnp.minimum(x, 0.0) - jnp.log1p(jnp.exp(-jnp.abs(x)))


def _softplus(x):
    return jnp.maximum(x, 0.0) + jnp.log1p(jnp.exp(-jnp.abs(x)))


def _chunk_masks(n, chunk):
    r = lax.broadcasted_iota(jnp.int32, (n, n), 0)
    c = lax.broadcasted_iota(jnp.int32, (n, n), 1)
    sh = int(math.log2(chunk))
    same = lax.shift_right_logical(r, sh) == lax.shift_right_logical(c, sh)
    tri = jnp.where(same & (c <= r), 1.0, 0.0).astype(F32)
    ones = jnp.where(same, 1.0, 0.0).astype(F32)
    return tri, ones


def _mm_kernel(x_ref, w_ref, o_ref):
    o_ref[...] = jnp.dot(x_ref[...], w_ref[...].astype(BF16), preferred_element_type=F32)


def _matmul(x, w, layer, tm, tn):
    m, k = x.shape
    n = w.shape[2]
    return pl.pallas_call(
        _mm_kernel,
        grid=(m // tm, n // tn),
        in_specs=[pl.BlockSpec((tm, k), lambda i, j: (i, 0)),
                  pl.BlockSpec((None, k, tn), lambda i, j: (layer, 0, j))],
        out_specs=pl.BlockSpec((tm, tn), lambda i, j: (i, j)),
        out_shape=jax.ShapeDtypeStruct((m, n), F32),
        compiler_params=_cparams(("parallel", "arbitrary")),
        name="proj_matmul",
    )(x, w)


def _matmul_layers(x, w, tn):
    m, k = x.shape
    depth, _, n = w.shape
    return pl.pallas_call(
        _mm_kernel,
        grid=(depth, n // tn),
        in_specs=[pl.BlockSpec((m, k), lambda l, j: (0, 0)),
                  pl.BlockSpec((None, k, tn), lambda l, j: (l, 0, j))],
        out_specs=pl.BlockSpec((None, m, tn), lambda l, j: (l, 0, j)),
        out_shape=jax.ShapeDtypeStruct((depth, m, n), F32),
        compiler_params=_cparams(("parallel", "arbitrary")),
        name="mem_kv_matmul",
    )(x, w)


def _outproj_kernel(ma_ref, mb_ref, mc_ref, md_ref, w_ref, x_ref, g_ref, b_ref,
                    xo_ref, xb_ref, acc_ref, *, tn, nj):
    j = pl.program_id(1)
    acc = jnp.dot(ma_ref[...], w_ref[0 * GROUP_W:1 * GROUP_W, :], preferred_element_type=F32)
    acc += jnp.dot(mb_ref[...], w_ref[1 * GROUP_W:2 * GROUP_W, :], preferred_element_type=F32)
    acc += jnp.dot(mc_ref[...], w_ref[2 * GROUP_W:3 * GROUP_W, :], preferred_element_type=F32)
    acc += jnp.dot(md_ref[...], w_ref[3 * GROUP_W:4 * GROUP_W, :], preferred_element_type=F32)
    acc_ref[j] = acc

    @pl.when(j == nj - 1)
    def _():
        tm = acc_ref.shape[1]
        tot = jnp.zeros((tm, 1), F32)
        for jj in range(nj):
            y = DN_ALPHA * x_ref[:, jj * tn:(jj + 1) * tn] + acc_ref[jj]
            acc_ref[jj] = y
            tot = tot + jnp.sum(y, axis=1, keepdims=True)
        mean = tot * (1.0 / D_MODEL)
        sq = jnp.zeros((tm, 1), F32)
        for jj in range(nj):
            yc = acc_ref[jj] - mean
            sq = sq + jnp.sum(yc * yc, axis=1, keepdims=True)
        rstd = lax.rsqrt(sq * (1.0 / D_MODEL) + LN_EPS)
        for jj in range(nj):
            sl = slice(jj * tn, (jj + 1) * tn)
            out = (acc_ref[jj] - mean) * rstd * g_ref[:, sl] + b_ref[:, sl]
            xo_ref[:, sl] = out
            xb_ref[:, sl] = out.astype(BF16)


def _outproj_ln(mixes, w_out, layer, x, ln_g, ln_b, tm, tn):
    m = x.shape[0]
    nj = D_MODEL // tn
    mix_spec = pl.BlockSpec((tm, GROUP_W), lambda i, j: (i, 0))
    row_spec = pl.BlockSpec((None, 1, D_MODEL), lambda i, j: (layer, 0, 0))
    blk = pl.BlockSpec((tm, D_MODEL), lambda i, j: (i, 0))
    return pl.pallas_call(
        functools.partial(_outproj_kernel, tn=tn, nj=nj),
        grid=(m // tm, nj),
        in_specs=[mix_spec, mix_spec, mix_spec, mix_spec,
                  pl.BlockSpec((None, D_MODEL, tn), lambda i, j: (layer, 0, j)),
                  blk, row_spec, row_spec],
        out_specs=[blk, blk],
        out_shape=[jax.ShapeDtypeStruct((m, D_MODEL), F32), jax.ShapeDtypeStruct((m, D_MODEL), BF16)],
        scratch_shapes=[pltpu.VMEM((nj, tm, tn), F32)],
        compiler_params=_cparams(("parallel", "arbitrary")),
        name="outproj_ln",
    )(*mixes, w_out, x, ln_g, ln_b)


def _attn_kernel(q_ref, z_ref, k_ref, v_ref, o_ref):
    for h in range(XA_HEADS):
        sl = slice(h * XA_DH, (h + 1) * XA_DH)
        q = q_ref[:, sl].astype(BF16)
        k = k_ref[:, sl].astype(BF16)
        v = v_ref[:, sl].astype(BF16)
        s = lax.dot_general(q, k, NT_DIMS, preferred_element_type=F32) * (XA_DH ** -0.5)
        s = s - jnp.max(s, axis=1, keepdims=True)
        e = jnp.exp(s)
        pr = e / jnp.sum(e, axis=1, keepdims=True)
        o = jnp.dot(pr.astype(BF16), v, preferred_element_type=F32)
        o_ref[:, sl] = (o * _silu(z_ref[:, sl])).astype(BF16)


def _mem_attn(p, mem_k, mem_v, layer, row0, batch, seq, tc):
    nt = seq // tc
    rb0 = row0 // tc
    pspec = lambda cb: pl.BlockSpec((tc, GROUP_W), lambda b, t: (rb0 + b * nt + t, cb))
    kvspec = pl.BlockSpec((None, N_MEM, GROUP_W), lambda b, t: (layer, b, 0))
    return pl.pallas_call(
        _attn_kernel,
        grid=(batch, nt),
        in_specs=[pspec(C_XQ // GROUP_W), pspec(C_XZ // GROUP_W), kvspec, kvspec],
        out_specs=pl.BlockSpec((tc, GROUP_W), lambda b, t: (b * nt + t, 0)),
        out_shape=jax.ShapeDtypeStruct((batch * seq, GROUP_W), BF16),
        compiler_params=_cparams(("parallel", "arbitrary")),
        name="mem_attn",
    )(p, p, mem_k, mem_v)


def _hgrn_kernel(q_ref, f_ref, i_ref, z_ref, par_ref, s0_ref, o_ref, s_ref,
                 st_scr, b_scr, qs_scr, kk_scr, el_scr, qt_scr, kt_scr, vb_scr, h_scr,
                 *, chunk, tc, nt):
    t = pl.program_id(1)
    d = HG_D

    @pl.when(t == 0)
    def _():
        for h in range(HG_HEADS):
            st_scr[h] = s0_ref[0, h].T

    log_lb = par_ref[0:1, :]
    log1m_lb = par_ref[1:2, :]
    one_m_lb = par_ref[2:3, :]
    norm_w = par_ref[3:4, :]

    fpre = f_ref[...]
    bt = log1m_lb + _log_sigmoid(fpre)
    logf = jnp.maximum(log_lb, bt) + jnp.log1p(jnp.exp(-jnp.abs(log_lb - bt)))
    kk = one_m_lb * _sigmoid(-fpre)
    tri, ones = _chunk_masks(tc, chunk)
    b = jnp.dot(tri, logf, precision=HI, preferred_element_type=F32)
    bl = jnp.dot(ones, logf, precision=HI, preferred_element_type=F32)
    qs = _silu(q_ref[...])
    b_scr[...] = b
    qs_scr[...] = qs
    kk_scr[...] = kk
    el_scr[...] = jnp.exp(bl)
    qt_scr[...] = (qs * jnp.exp(b)).astype(BF16)
    kt_scr[...] = (kk * jnp.exp(bl - b)).astype(BF16)
    vb_scr[...] = i_ref[...].astype(BF16)

    rowid = lax.broadcasted_iota(jnp.int32, (chunk, d), 0)

    def chunk_body(c, carry):
        r0 = c * chunk if isinstance(c, int) else pl.multiple_of(c * chunk, chunk)
        rows = pl.ds(r0, chunk)
        for h in range(HG_HEADS):
            sl = slice(h * d, (h + 1) * d)
            bc = b_scr[rows, sl]
            qc = qs_scr[rows, sl]
            kc = kk_scr[rows, sl]
            vc = i_ref[rows, sl]
            o = jnp.zeros((chunk, d), F32)
            for s in range(chunk):
                e = jnp.exp(jnp.where(rowid >= s, bc - bc[s:s + 1, :], -jnp.inf))
                a = jnp.sum(qc * kc[s:s + 1, :] * e, axis=1, keepdims=True)
                o = o + a * vc[s:s + 1, :]
            st = st_scr[h]
            o = o + lax.dot_general(qt_scr[rows, sl], st.astype(BF16), NT_DIMS, preferred_element_type=F32)
            el = el_scr[pl.ds(r0, 1), sl]
            st_scr[h] = st * el + lax.dot_general(vb_scr[rows, sl], kt_scr[rows, sl], TN_DIMS,
                                                  preferred_element_type=F32)
            h_scr[rows, sl] = o
        return carry

    nchunk = tc // chunk
    if nchunk == 1:
        chunk_body(0, 0)
    else:
        lax.fori_loop(0, nchunk, chunk_body, 0)

    for h in range(HG_HEADS):
        sl = slice(h * d, (h + 1) * d)
        hh = h_scr[:, sl]
        ms = jnp.mean(hh * hh, axis=1, keepdims=True)
        y = hh * lax.rsqrt(ms + 1e-5) * norm_w[:, sl] * _silu(z_ref[:, sl])
        o_ref[:, sl] = y.astype(BF16)

    @pl.when(t == nt - 1)
    def _():
        for h in range(HG_HEADS):
            s_ref[0, h] = st_scr[h].T


def _hgrn(p, par, state, layer, row0, batch, seq, tc, chunk):
    nt = seq // tc
    rb0 = row0 // tc
    pspec = lambda cb: pl.BlockSpec((tc, GROUP_W), lambda b, t: (rb0 + b * nt + t, cb))
    if state.ndim == 5:
        sspec = pl.BlockSpec((None, 1, HG_HEADS, HG_D, HG_D), lambda b, t: (layer, b, 0, 0, 0))
    else:
        sspec = pl.BlockSpec((1, HG_HEADS, HG_D, HG_D), lambda b, t: (b, 0, 0, 0))
    big = lambda dt: pltpu.VMEM((tc, GROUP_W), dt)
    return pl.pallas_call(
        functools.partial(_hgrn_kernel, chunk=chunk, tc=tc, nt=nt),
        grid=(batch, nt),
        in_specs=[pspec(C_HQ // GROUP_W), pspec(C_HF // GROUP_W), pspec(C_HI // GROUP_W), pspec(C_HZ // GROUP_W),
                  pl.BlockSpec((None, 8, GROUP_W), lambda b, t: (layer, 0, 0)), sspec],
        out_specs=[pl.BlockSpec((tc, GROUP_W), lambda b, t: (b * nt + t, 0)),
                   pl.BlockSpec((1, HG_HEADS, HG_D, HG_D), lambda b, t: (b, 0, 0, 0))],
        out_shape=[jax.ShapeDtypeStruct((batch * seq, GROUP_W), BF16),
                   jax.ShapeDtypeStruct((batch, HG_HEADS, HG_D, HG_D), F32)],
        scratch_shapes=[pltpu.VMEM((HG_HEADS, HG_D, HG_D), F32),
                        big(F32), big(F32), big(F32), big(F32), big(BF16), big(BF16), big(BF16), big(F32)],
        compiler_params=_cparams(("parallel", "arbitrary")),
        name="hgrn2",
    )(p, p, p, p, par, state)


def _mlstm_kernel(qk_ref, v_ref, z_ref, g_ref, par_ref, bias_ref, c0_ref, n0_ref, m0_ref,
                  o_ref, c_ref, n_ref, m_ref, m_scr, h_scr, *, chunk, tc, nt):
    t = pl.program_id(1)
    L = chunk

    @pl.when(t == 0)
    def _():
        c_ref[...] = c0_ref[...]
        n_ref[...] = n0_ref[...]
        m_scr[...] = jnp.zeros(m_scr.shape, F32)
        for h in range(ML_HEADS):
            m_scr[h:h + 1, :] = jnp.broadcast_to(m0_ref[0, :, h:h + 1], (1, LANE))

    r = lax.broadcasted_iota(jnp.int32, (L, L), 0)
    c = lax.broadcasted_iota(jnp.int32, (L, L), 1)
    tril = r >= c
    triu = r <= c

    for ci in range(tc // L):
        rows = slice(ci * L, (ci + 1) * L)
        g = g_ref[rows, :]
        gt = g.T
        for h in range(ML_HEADS):
            ib = bias_ref[:, h:h + 1]
            fb = bias_ref[:, ML_HEADS + h:ML_HEADS + h + 1]
            i_col = g[:, h:h + 1] + ib
            i_row = gt[h:h + 1, 0:L] + ib
            lf_col = _log_sigmoid(g[:, ML_HEADS + h:ML_HEADS + h + 1] + fb)
            lf_row = _log_sigmoid(gt[ML_HEADS + h:ML_HEADS + h + 1, 0:L] + fb)
            b_col = jnp.sum(jnp.where(tril, lf_row, 0.0), axis=1, keepdims=True)
            b_row = jnp.sum(jnp.where(triu, lf_col, 0.0), axis=0, keepdims=True)
            m = m_scr[h:h + 1, 0:1]
            dmat = jnp.where(tril, b_col - b_row + i_row, -jnp.inf)
            m_t = jnp.maximum(b_col + m, jnp.max(dmat, axis=1, keepdims=True))
            q = qk_ref[rows, h * ML_DK:(h + 1) * ML_DK]
            k = qk_ref[rows, ML_HEADS * ML_DK + h * ML_DK:ML_HEADS * ML_DK + (h + 1) * ML_DK] * (ML_DK ** -0.5)
            vb = v_ref[rows, h * ML_DV:(h + 1) * ML_DV].astype(BF16)
            qb = q.astype(BF16)
            s_qk = lax.dot_general(qb, k.astype(BF16), NT_DIMS, preferred_element_type=F32)
            pm = jnp.exp(dmat - m_t) * s_qk
            inter = jnp.exp(b_col + m - m_t)
            cst = c_ref[0, h]
            nst = n_ref[0, h:h + 1, :]
            num = (jnp.dot(pm.astype(BF16), vb, preferred_element_type=F32)
                   + inter * jnp.dot(qb, cst.astype(BF16), preferred_element_type=F32))
            den = jnp.sum(pm, axis=1, keepdims=True) + inter * jnp.sum(q * nst, axis=1, keepdims=True)
            hh = num / jnp.maximum(jnp.abs(den), jnp.exp(-m_t))
            m_new = m_t[L - 1:L, :]
            b_last = b_col[L - 1:L, :]
            wgt = jnp.exp(b_last - b_col + i_col - m_new)
            decay = jnp.exp(b_last + m - m_new)
            kw = k * wgt
            c_ref[0, h] = decay * cst + lax.dot_general(kw.astype(BF16), vb, TN_DIMS, preferred_element_type=F32)
            n_ref[0, h:h + 1, :] = decay * nst + jnp.sum(kw, axis=0, keepdims=True)
            m_scr[h:h + 1, :] = jnp.broadcast_to(m_new, (1, LANE))
            h_scr[rows, h * ML_DV:(h + 1) * ML_DV] = hh

    for h in range(ML_HEADS):
        sl = slice(h * ML_DV, (h + 1) * ML_DV)
        x = h_scr[:, sl]
        xc = x - jnp.mean(x, axis=1, keepdims=True)
        y = xc * lax.rsqrt(jnp.mean(xc * xc, axis=1, keepdims=True) + 1e-6)
        o_ref[:, sl] = (y * par_ref[:, sl] * _silu(z_ref[:, sl])).astype(BF16)

    @pl.when(t == nt - 1)
    def _():
        m_ref[0] = m_scr[...]


def _mlstm(p, norm_w, bias, c0, n0, m0, layer, row0, batch, seq, tc, chunk):
    nt = seq // tc
    rb0 = row0 // tc
    pspec = lambda cb: pl.BlockSpec((tc, GROUP_W), lambda b, t: (rb0 + b * nt + t, cb))
    if c0.ndim == 5:
        cspec = pl.BlockSpec((None, 1, ML_HEADS, ML_DK, ML_DV), lambda b, t: (layer, b, 0, 0, 0))
        nspec = pl.BlockSpec((None, 1, ML_HEADS, ML_DK), lambda b, t: (layer, b, 0, 0))
        mspec = pl.BlockSpec((None, 1, 1, ML_HEADS), lambda b, t: (layer, b, 0, 0))
    else:
        cspec = pl.BlockSpec((1, ML_HEADS, ML_DK, ML_DV), lambda b, t: (b, 0, 0, 0))
        nspec = pl.BlockSpec((1, ML_HEADS, ML_DK), lambda b, t: (b, 0, 0))
        mspec = pl.BlockSpec((1, 1, ML_HEADS), lambda b, t: (b, 0, 0))
    return pl.pallas_call(
        functools.partial(_mlstm_kernel, chunk=chunk, tc=tc, nt=nt),
        grid=(batch, nt),
        in_specs=[pspec(C_MQK // GROUP_W), pspec(C_MV // GROUP_W), pspec(C_MZ // GROUP_W),
                  pl.BlockSpec((tc, LANE), lambda b, t: (rb0 + b * nt + t, C_MG // LANE)),
                  pl.BlockSpec((None, 1, GROUP_W), lambda b, t: (layer, 0, 0)),
                  pl.BlockSpec((None, 1, 2 * ML_HEADS), lambda b, t: (layer, 0, 0)),
                  cspec, nspec, mspec],
        out_specs=[pl.BlockSpec((tc, GROUP_W), lambda b, t: (b * nt + t, 0)),
                   pl.BlockSpec((1, ML_HEADS, ML_DK, ML_DV), lambda b, t: (b, 0, 0, 0)),
                   pl.BlockSpec((1, ML_HEADS, ML_DK), lambda b, t: (b, 0, 0)),
                   pl.BlockSpec((1, 8, LANE), lambda b, t: (b, 0, 0))],
        out_shape=[jax.ShapeDtypeStruct((batch * seq, GROUP_W), BF16),
                   jax.ShapeDtypeStruct((batch, ML_HEADS, ML_DK, ML_DV), F32),
                   jax.ShapeDtypeStruct((batch, ML_HEADS, ML_DK), F32),
                   jax.ShapeDtypeStruct((batch, 8, LANE), F32)],
        scratch_shapes=[pltpu.VMEM((8, LANE), F32), pltpu.VMEM((tc, GROUP_W), F32)],
        compiler_params=_cparams(("parallel", "arbitrary")),
        name="mlstm",
    )(p, p, p, p, norm_w, bias, c0, n0, m0)


RW_CHUNK = 16


def _rwkv_kernel(r_ref, k_ref, v_ref, z_ref, x_ref, buf_ref, bufx_ref, mu_ref, mux_ref, par_ref,
                 w2_ref, a2_ref, s0_ref, o_ref, s_ref,
                 sp_scr, prev_scr, prevx_scr, kh_scr, rh_scr, ki_scr, ai_scr, kd_scr, ad_scr, vb_scr,
                 gl_scr, y_scr, bonus_scr, *, tc, tp, nt):
    t = pl.program_id(1)
    L = RW_CHUNK
    W = GROUP_W

    lane2 = lax.broadcasted_iota(jnp.int32, (LANE, LANE), 1)
    row2 = lax.broadcasted_iota(jnp.int32, (LANE, LANE), 0)
    diag_blocks = (lane2 < RW_HEAD) == (row2 < RW_HEAD)
    seg_ones = jnp.where(diag_blocks, 1.0, 0.0).astype(F32)

    @pl.when(t == 0)
    def _():
        prev_scr[...] = buf_ref[0]
        prevx_scr[...] = bufx_ref[0]
        zero = jnp.zeros((RW_HEAD, RW_HEAD), F32)
        for pr in range(RW_PAIRS):
            top = jnp.concatenate([s0_ref[0, 2 * pr], zero], axis=1)
            bot = jnp.concatenate([zero, s0_ref[0, 2 * pr + 1]], axis=1)
            sp_scr[pr] = jnp.concatenate([top, bot], axis=0)

    def seg_sum(x):
        parts = [jnp.dot(x[:, i * LANE:(i + 1) * LANE], seg_ones, precision=HI, preferred_element_type=F32)
                 for i in range(x.shape[1] // LANE)]
        return jnp.concatenate(parts, axis=1)

    def shifted(cur, prev_row):
        if tc == 1:
            return prev_row
        rid = lax.broadcasted_iota(jnp.int32, cur.shape, 0)
        return jnp.where(rid == 0, prev_row, pltpu.roll(cur, 1, axis=0))

    def mix(cur, prev_row, mu):
        return cur + (shifted(cur, prev_row) - cur) * mu

    pr_ = r_ref[...]
    pk_ = k_ref[...]
    pv_ = v_ref[...]
    px_ = x_ref[...]
    xr = mix(pr_, prev_scr[:, 0:W], mu_ref[:, 0:W])
    xk = mix(pk_, prev_scr[:, W:2 * W], mu_ref[:, W:2 * W])
    xv = mix(pv_, prev_scr[:, 2 * W:3 * W], mu_ref[:, 2 * W:3 * W])
    xx = mix(px_, prevx_scr[...], mux_ref[...])
    prev_scr[:, 0:W] = pr_[tc - 1:tc, :]
    prev_scr[:, W:2 * W] = pk_[tc - 1:tc, :]
    prev_scr[:, 2 * W:3 * W] = pv_[tc - 1:tc, :]
    prevx_scr[...] = px_[tc - 1:tc, :]

    w0 = par_ref[0:1, :]
    a0 = par_ref[1:2, :]
    k_k = par_ref[2:3, :]
    k_a = par_ref[3:4, :]
    r_k = par_ref[4:5, :]
    ln_w = par_ref[5:6, :]
    ln_b = par_ref[6:7, :]

    wlin = w0 + jnp.dot(jnp.tanh(xx).astype(BF16), w2_ref[...], preferred_element_type=F32)
    wdec = -_softplus(-wlin) - 0.5
    logd = -jnp.exp(wdec)
    a = _sigmoid(a0 + jnp.dot(xx.astype(BF16), a2_ref[...], preferred_element_type=F32))
    kk = xk * k_k
    kk = kk / jnp.maximum(jnp.sqrt(seg_sum(kk * kk)), 1e-12)
    kp = xk * (1.0 + (a - 1.0) * k_a)
    alpha = a * kk
    bonus = seg_sum(xr * kp * r_k) * xv

    if tp > tc:
        pad = lambda u: jnp.concatenate([u, jnp.zeros((tp - tc, u.shape[1]), F32)], axis=0)
        logd, kk, kp, alpha, xr, xv = pad(logd), pad(kk), pad(kp), pad(alpha), pad(xr), pad(xv)
        bonus_scr[...] = pad(bonus)
    else:
        bonus_scr[...] = bonus

    tri, ones = _chunk_masks(tp, L)
    g = jnp.dot(tri, logd, precision=HI, preferred_element_type=F32)
    gl = jnp.dot(ones, logd, precision=HI, preferred_element_type=F32)
    einv = jnp.exp(-g)
    egl = jnp.exp(gl - g)
    kh_scr[...] = (kk * jnp.exp(g - logd)).astype(BF16)
    rh_scr[...] = (xr * jnp.exp(g)).astype(BF16)
    ki_scr[...] = (kp * einv).astype(BF16)
    ai_scr[...] = (alpha * einv).astype(BF16)
    kd_scr[...] = (kp * egl).astype(BF16)
    ad_scr[...] = (alpha * egl).astype(BF16)
    vb_scr[...] = xv.astype(BF16)
    gl_scr[...] = jnp.exp(gl)

    lane_l = lax.broadcasted_iota(jnp.int32, (L, LANE), 1)
    head_a = lane_l < RW_HEAD
    rl = lax.broadcasted_iota(jnp.int32, (L, L), 0)
    cl = lax.broadcasted_iota(jnp.int32, (L, L), 1)
    strict = rl > cl
    incl = rl >= cl
    zb = jnp.zeros((L, LANE), BF16)

    def chunk_body(c, carry):
        r0 = c * L if isinstance(c, int) else pl.multiple_of(c * L, L)
        rows = pl.ds(r0, L)
        for pr in range(RW_PAIRS):
            sl = slice(pr * LANE, (pr + 1) * LANE)
            kh = kh_scr[rows, sl]
            rh = rh_scr[rows, sl]
            vb = vb_scr[rows, sl]
            x4 = jnp.concatenate([jnp.where(head_a, kh, zb), jnp.where(head_a, zb, kh),
                                  jnp.where(head_a, rh, zb), jnp.where(head_a, zb, rh)], axis=0)
            y2 = jnp.concatenate([ai_scr[rows, sl], ki_scr[rows, sl]], axis=0)
            gm = lax.dot_general(x4, y2, NT_DIMS, preferred_element_type=F32)
            sp = sp_scr[pr]
            ks = lax.dot_general(jnp.concatenate([kh, rh], axis=0), sp.astype(BF16), NT_DIMS,
                                 preferred_element_type=F32)
            n_a = jnp.where(strict, gm[0:L, 0:L], 0.0)
            n_b = jnp.where(strict, gm[L:2 * L, 0:L], 0.0)
            m_ab = jnp.concatenate([jnp.where(strict, gm[0:L, L:2 * L], 0.0),
                                    jnp.where(strict, gm[L:2 * L, L:2 * L], 0.0)], axis=0)
            mv = jnp.dot(m_ab.astype(BF16), vb, preferred_element_type=F32)
            w = ks[0:L] + jnp.where(head_a, mv[0:L], mv[L:2 * L])
            for s in range(L - 1):
                coef = jnp.where(head_a, n_a[:, s:s + 1], n_b[:, s:s + 1])
                w = w - coef * w[s:s + 1, :]
            vw = jnp.concatenate([vb, w.astype(BF16)], axis=0)
            cm = jnp.concatenate([
                jnp.concatenate([jnp.where(incl, gm[2 * L:3 * L, L:2 * L], 0.0),
                                 -jnp.where(incl, gm[2 * L:3 * L, 0:L], 0.0)], axis=1),
                jnp.concatenate([jnp.where(incl, gm[3 * L:4 * L, L:2 * L], 0.0),
                                 -jnp.where(incl, gm[3 * L:4 * L, 0:L], 0.0)], axis=1)], axis=0)
            yy = jnp.dot(cm.astype(BF16), vw, preferred_element_type=F32)
            y_scr[rows, sl] = ks[L:2 * L] + jnp.where(head_a, yy[0:L], yy[L:2 * L])
            kd_ad = jnp.concatenate([kd_scr[rows, sl], -ad_scr[rows, sl]], axis=0)
            upd = lax.dot_general(vw, kd_ad, TN_DIMS, preferred_element_type=F32)
            sp_scr[pr] = sp * gl_scr[pl.ds(r0, 1), sl] + jnp.where(diag_blocks, upd, 0.0)
        return carry

    nchunk = tp // L
    if nchunk == 1:
        chunk_body(0, 0)
    else:
        lax.fori_loop(0, nchunk, chunk_body, 0)

    y = y_scr[0:tc, :]
    yc = y - seg_sum(y) * (1.0 / RW_HEAD)
    yn = yc * lax.rsqrt(seg_sum(yc * yc) * (1.0 / RW_HEAD) + RW_GN_EPS)
    out = (yn * ln_w + ln_b + bonus_scr[0:tc, :]) * _silu(z_ref[...])
    o_ref[...] = out.astype(BF16)

    @pl.when(t == nt - 1)
    def _():
        for pr in range(RW_PAIRS):
            sp = sp_scr[pr]
            s_ref[0, 2 * pr] = sp[0:RW_HEAD, 0:RW_HEAD]
            s_ref[0, 2 * pr + 1] = sp[RW_HEAD:LANE, RW_HEAD:LANE]


def _rwkv(p, buf, bufx, mu, mux, par, w2p, a2p, state, layer, row0, batch, seq, tc):
    nt = seq // tc
    rb0 = row0 // tc
    tp = max(tc, RW_CHUNK)
    pspec = lambda cb: pl.BlockSpec((tc, GROUP_W), lambda b, t: (rb0 + b * nt + t, cb))
    if state.ndim == 5:
        sspec = pl.BlockSpec((None, 1, RW_HEADS, RW_HEAD, RW_HEAD), lambda b, t: (layer, b, 0, 0, 0))
    else:
        sspec = pl.BlockSpec((1, RW_HEADS, RW_HEAD, RW_HEAD), lambda b, t: (b, 0, 0, 0))
    lay = lambda shape: pl.BlockSpec((None,) + shape, lambda b, t: (layer,) + (0,) * len(shape))
    big = lambda dt: pltpu.VMEM((tp, GROUP_W), dt)
    return pl.pallas_call(
        functools.partial(_rwkv_kernel, tc=tc, tp=tp, nt=nt),
        grid=(batch, nt),
        in_specs=[pspec(C_RR // GROUP_W), pspec(C_RK // GROUP_W), pspec(C_RV // GROUP_W), pspec(C_RZ // GROUP_W),
                  pl.BlockSpec((tc, LANE), lambda b, t: (rb0 + b * nt + t, C_RX // LANE)),
                  pl.BlockSpec((1, 1, 3 * GROUP_W), lambda b, t: (b, 0, 0)),
                  pl.BlockSpec((1, 1, LANE), lambda b, t: (b, 0, 0)),
                  lay((1, 3 * GROUP_W)), lay((1, LANE)), lay((8, GROUP_W)),
                  lay((LANE, GROUP_W)), lay((LANE, GROUP_W)), sspec],
        out_specs=[pl.BlockSpec((tc, GROUP_W), lambda b, t: (b * nt + t, 0)),
                   pl.BlockSpec((1, RW_HEADS, RW_HEAD, RW_HEAD), lambda b, t: (b, 0, 0, 0))],
        out_shape=[jax.ShapeDtypeStruct((batch * seq, GROUP_W), BF16),
                   jax.ShapeDtypeStruct((batch, RW_HEADS, RW_HEAD, RW_HEAD), F32)],
        scratch_shapes=[pltpu.VMEM((RW_PAIRS, LANE, LANE), F32),
                        pltpu.VMEM((1, 3 * GROUP_W), F32), pltpu.VMEM((1, LANE), F32),
                        big(BF16), big(BF16), big(BF16), big(BF16), big(BF16), big(BF16), big(BF16),
                        big(F32), big(F32), big(F32)],
        compiler_params=_cparams(("parallel", "arbitrary")),
        name="rwkv7",
    )(p, p, p, p, p, buf, bufx, mu, mux, par, w2p, a2p, state)


def _relayout_w_in(w_in):
    seg = lambda a, b: w_in[:, :, a:b]
    pad = jnp.zeros(w_in.shape[:2] + (NP - N_IN,), w_in.dtype)
    parts = [seg(0, 6144),
             seg(6152, 7176),
             seg(7176, 10248),
             seg(10376, 11400),
             seg(11400, 13448),
             seg(10248, 10376),
             seg(6144, 6152),
             pad]
    return jnp.concatenate(parts, axis=-1).astype(BF16)


def _pad_rows(x, rows):
    return jnp.concatenate([x, jnp.zeros((x.shape[0], rows - x.shape[1], x.shape[2]), x.dtype)], axis=1)


def _trunk_layer(l, x_f32, x_bf, w_in_r, w_out_b, hg_par, ml_nw, ml_bias, rw_mu, rw_mux, rw_par, rw_w2, rw_a2,
                 ln_g, ln_b, groups, cfg):
    p = _matmul(x_bf, w_in_r, l, cfg["mm_tm"], cfg["mm_tn"])
    mixes = [[], [], [], []]
    new_states = []
    for grp in groups:
        row0, batch, seq = grp["row0"], grp["batch"], grp["seq"]
        mix_a, s_hg = _hgrn(p, hg_par, grp["hg"], l, row0, batch, seq, grp["hg_tc"], grp["hg_chunk"])
        mix_b, c_ml, n_ml, m_ml = _mlstm(p, ml_nw, ml_bias, grp["ml_c"], grp["ml_n"], grp["ml_m"], l, row0, batch,
                                         seq, grp["ml_tc"], grp["ml_chunk"])
        mix_c, s_rw = _rwkv(p, grp["rw_buf"][l], grp["rw_bufx"][l], rw_mu, rw_mux, rw_par, rw_w2, rw_a2,
                            grp["rw"], l, row0, batch, seq, grp["rw_tc"])
        mix_d = _mem_attn(p, grp["mem_k"], grp["mem_v"], l, row0, batch, seq, grp["xa_tc"])
        last = p[row0:row0 + batch * seq].reshape(batch, seq, NP)[:, seq - 1, :]
        buf_new = jnp.concatenate([last[:, C_RR:C_RR + 3 * GROUP_W], last[:, C_RX:C_RX + 2 * RW_LORA]], axis=-1)
        for lst, mx in zip(mixes, (mix_a, mix_b, mix_c, mix_d)):
            lst.append(mx)
        new_states.append((s_hg, c_ml, n_ml, m_ml[:, :ML_HEADS, 0], s_rw, buf_new))
    mixes = [jnp.concatenate(lst, axis=0) if len(lst) > 1 else lst[0] for lst in mixes]
    x_f32, x_bf = _outproj_ln(mixes, w_out_b, l, x_f32, ln_g, ln_b, cfg["op_tm"], cfg["op_tn"])
    return x_f32, x_bf, new_states


def kernel(x_prompt, x_sample, mem_prompt, state_hgrn, state_mlstm_C, state_mlstm_n, state_mlstm_m, state_rwkv, state_rwkv_shift, cache_mem_k, cache_mem_v, w_in, hgrn_lb, hgrn_norm_w, mlstm_ig_b, mlstm_fg_b, mlstm_norm_w, rwkv_mu, rwkv_w0, rwkv_w2, rwkv_a0, rwkv_a2, rwkv_k_k, rwkv_k_a, rwkv_r_k, rwkv_ln_w, rwkv_ln_b, mem_wk, mem_wv, w_out, ln_g, ln_b):
    bp, tp_, _ = x_prompt.shape
    bs, ts, _ = x_sample.shape
    depth = w_in.shape[0]
    mp, ms = bp * tp_, bs * ts

    w_in_r = _relayout_w_in(w_in)
    w_out_b = w_out.astype(BF16)
    lb_all = jnp.cumsum(jax.nn.softmax(hgrn_lb.astype(F32), axis=0), axis=0)
    lb_all = lb_all - lb_all[0]
    zrow = jnp.zeros_like(lb_all)
    hg_par = jnp.stack([jnp.log(lb_all), jnp.log1p(-lb_all), 1.0 - lb_all, hgrn_norm_w.astype(F32),
                        zrow, zrow, zrow, zrow], axis=1)
    ml_nw = mlstm_norm_w.astype(F32)[:, None, :]
    ml_bias = jnp.concatenate([mlstm_ig_b, mlstm_fg_b], axis=-1).astype(F32)[:, None, :]
    rw_mu = rwkv_mu[:, None, :3 * GROUP_W].astype(F32)
    rw_mux = rwkv_mu[:, None, 3 * GROUP_W:].astype(F32)
    zr = jnp.zeros((depth, GROUP_W), F32)
    rw_par = jnp.stack([rwkv_w0, rwkv_a0, rwkv_k_k, rwkv_k_a, rwkv_r_k.reshape(depth, GROUP_W), rwkv_ln_w,
                        rwkv_ln_b, zr], axis=1).astype(F32)
    zl = jnp.zeros((depth, RW_LORA, GROUP_W), F32)
    rw_w2 = jnp.concatenate([rwkv_w2.astype(F32), zl], axis=1).astype(BF16)
    rw_a2 = jnp.concatenate([zl, rwkv_a2.astype(F32)], axis=1).astype(BF16)
    ln_g3 = ln_g.astype(F32)[:, None, :]
    ln_b3 = ln_b.astype(F32)[:, None, :]

    mem_x = mem_prompt.reshape(bp * N_MEM, D_MODEL).astype(BF16)
    mk_p = _matmul_layers(mem_x, mem_wk, 256)
    mv_p = _matmul_layers(mem_x, mem_wv, 256)

    def split_buf(buf):
        return buf[:, :, None, :3 * GROUP_W].astype(F32), buf[:, :, None, 3 * GROUP_W:].astype(F32)

    zbuf, zbufx = split_buf(jnp.zeros((depth, bp, RW_SHIFT_W), F32))
    sbuf, sbufx = split_buf(state_rwkv_shift)
    prompt = dict(row0=0, batch=bp, seq=tp_,
                  hg=jnp.zeros((bp, HG_HEADS, HG_D, HG_D), F32), hg_tc=256, hg_chunk=16,
                  ml_c=jnp.zeros((bp, ML_HEADS, ML_DK, ML_DV), F32), ml_n=jnp.zeros((bp, ML_HEADS, ML_DK), F32),
                  ml_m=jnp.zeros((bp, 1, ML_HEADS), F32), ml_tc=256, ml_chunk=64,
                  rw=jnp.zeros((bp, RW_HEADS, RW_HEAD, RW_HEAD), F32), rw_buf=zbuf, rw_bufx=zbufx, rw_tc=128,
                  mem_k=mk_p, mem_v=mv_p, xa_tc=512)
    sample = dict(row0=mp, batch=bs, seq=ts,
                  hg=state_hgrn, hg_tc=ts, hg_chunk=ts,
                  ml_c=state_mlstm_C, ml_n=state_mlstm_n, ml_m=state_mlstm_m[:, :, None, :], ml_tc=ts, ml_chunk=ts,
                  rw=state_rwkv, rw_buf=sbuf, rw_bufx=sbufx, rw_tc=ts,
                  mem_k=cache_mem_k.reshape(depth, bs * N_MEM, GROUP_W),
                  mem_v=cache_mem_v.reshape(depth, bs * N_MEM, GROUP_W), xa_tc=ts)
    cfg = dict(mm_tm=1024, mm_tn=256, op_tm=256, op_tn=512)

    x_f32 = jnp.concatenate([x_prompt.reshape(mp, D_MODEL), x_sample.reshape(ms, D_MODEL)], axis=0).astype(F32)
    x_bf = x_f32.astype(BF16)
    per_layer = []
    for l in range(depth):
        x_f32, x_bf, st = _trunk_layer(l, x_f32, x_bf, w_in_r, w_out_b, hg_par, ml_nw, ml_bias, rw_mu, rw_mux,
                                       rw_par, rw_w2, rw_a2, ln_g3, ln_b3, [prompt, sample], cfg)
        per_layer.append(st)

    def stack(g, i):
        return jnp.stack([per_layer[l][g][i] for l in range(depth)], axis=0)

    y_prompt = x_f32[:mp].reshape(bp, tp_, D_MODEL)
    y_sample = x_f32[mp:].reshape(bs, ts, D_MODEL)
    p_states = tuple(stack(0, i) for i in range(6))
    s_states = tuple(stack(1, i) for i in range(6))
    mk_out = mk_p.reshape(depth, bp, N_MEM, XA_HEADS, XA_DH)
    mv_out = mv_p.reshape(depth, bp, N_MEM, XA_HEADS, XA_DH)
    return (y_prompt, y_sample) + p_states + (mk_out, mv_out) + s_states
```

```python
import functools
import math

import jax
import jax.numpy as jnp
from jax import lax
from jax.experimental import pallas as pl
from jax.experimental.pallas import tpu as pltpu

F32 = jnp.float32
BF16 = jnp.bfloat16
HI = lax.Precision.HIGHEST

D_MODEL = 4096
DEPTH = 4
GROUP_W = D_MODEL // 4
N_MEM = 256
HG_HEADS, HG_D = 8, 128
ML_HEADS, ML_DK, ML_DV = 4, 128, 256
RW_HEADS, RW_HEAD, RW_LORA = 16, 64, 64
RW_PAIRS = RW_HEADS // 2
XA_HEADS, XA_DH = 4, 256
RW_SHIFT_W = 3 * GROUP_W + 2 * RW_LORA
N_IN = 13448
DN_ALPHA = (2.0 * DEPTH) ** 0.25
LN_EPS = 1e-5
RW_GN_EPS = 64e-5

C_HQ, C_HF, C_HI, C_HZ = 0, 1024, 2048, 3072
C_MQK, C_MV, C_MZ = 4096, 5120, 6144
C_RR, C_RK, C_RV, C_RZ = 7168, 8192, 9216, 10240
C_XQ, C_XZ = 11264, 12288
C_RX = 13312
C_MG = 13440
NP = 13568
LANE = 128

VMEM_LIMIT = 60 * 1024 * 1024

NT_DIMS = (((1,), (1,)), ((), ()))
TN_DIMS = (((0,), (0,)), ((), ()))


def _cparams(sem):
    return pltpu.CompilerParams(dimension_semantics=sem, vmem_limit_bytes=VMEM_LIMIT)


def _sigmoid(x):
    return jax.nn.sigmoid(x)


def _silu(x):
    return x * _sigmoid(x)


def _log_sigmoid(x):
    return jnp.minimum(x, 0.0) - jnp.log1p(jnp.exp(-jnp.abs(x)))


def _softplus(x):
    return jnp.maximum(x, 0.0) + jnp.log1p(jnp.exp(-jnp.abs(x)))


def _chunk_masks(n, chunk):
    r = lax.broadcasted_iota(jnp.int32, (n, n), 0)
    c = lax.broadcasted_iota(jnp.int32, (n, n), 1)
    sh = int(math.log2(chunk))
    same = lax.shift_right_logical(r, sh) == lax.shift_right_logical(c, sh)
    tri = jnp.where(same & (c <= r), 1.0, 0.0).astype(BF16)
    ones = jnp.where(same, 1.0, 0.0).astype(BF16)
    return tri, ones


def _split_bf16(x, terms):
    parts = []
    for i in range(terms):
        part = x.astype(BF16)
        parts.append(part)
        if i + 1 < terms:
            x = x - part.astype(F32)
    return parts


def _dot_sel_left(sel, x, terms):
    return sum(jnp.dot(sel, part, preferred_element_type=F32) for part in _split_bf16(x, terms))


def _dot_sel_right(x, sel, terms):
    return sum(jnp.dot(part, sel, preferred_element_type=F32) for part in _split_bf16(x, terms))


def _chunk_cumsum(x, chunk):
    rid = lax.broadcasted_iota(jnp.int32, x.shape, 0) & (chunk - 1)
    step = 1
    while step < chunk:
        x = x + jnp.where(rid >= step, pltpu.roll(x, step, axis=0), 0.0)
        step *= 2
    return x


def _mm_kernel(x_ref, w_ref, o_ref):
    o_ref[...] = jnp.dot(x_ref[...], w_ref[...].astype(BF16), preferred_element_type=F32)


def _matmul(x, w, layer, tm, tn):
    m, k = x.shape
    n = w.shape[2]
    return pl.pallas_call(
        _mm_kernel,
        grid=(m // tm, n // tn),
        in_specs=[pl.BlockSpec((tm, k), lambda i, j: (i, 0)),
                  pl.BlockSpec((None, k, tn), lambda i, j: (layer, 0, j))],
        out_specs=pl.BlockSpec((tm, tn), lambda i, j: (i, j)),
        out_shape=jax.ShapeDtypeStruct((m, n), F32),
        compiler_params=_cparams(("parallel", "arbitrary")),
        name="proj_matmul",
    )(x, w)


def _matmul_layers(x, w, tn):
    m, k = x.shape
    depth, _, n = w.shape
    return pl.pallas_call(
        _mm_kernel,
        grid=(depth, n // tn),
        in_specs=[pl.BlockSpec((m, k), lambda l, j: (0, 0)),
                  pl.BlockSpec((None, k, tn), lambda l, j: (l, 0, j))],
        out_specs=pl.BlockSpec((None, m, tn), lambda l, j: (l, 0, j)),
        out_shape=jax.ShapeDtypeStruct((depth, m, n), F32),
        compiler_params=_cparams(("parallel", "arbitrary")),
        name="mem_kv_matmul",
    )(x, w)


def _outproj_kernel(mix_ref, w_ref, x_ref, g_ref, b_ref, xo_ref, xb_ref, acc_ref, *, tn, nj):
    j = pl.program_id(1)
    acc_ref[j] = jnp.dot(mix_ref[...], w_ref[...], preferred_element_type=F32)

    @pl.when(j == nj - 1)
    def _():
        tm = acc_ref.shape[1]
        tot = jnp.zeros((tm, 1), F32)
        for jj in range(nj):
            y = DN_ALPHA * x_ref[:, jj * tn:(jj + 1) * tn] + acc_ref[jj]
            acc_ref[jj] = y
            tot = tot + jnp.sum(y, axis=1, keepdims=True)
        mean = tot * (1.0 / D_MODEL)
        sq = jnp.zeros((tm, 1), F32)
        for jj in range(nj):
            yc = acc_ref[jj] - mean
            sq = sq + jnp.sum(yc * yc, axis=1, keepdims=True)
        rstd = lax.rsqrt(sq * (1.0 / D_MODEL) + LN_EPS)
        for jj in range(nj):
            sl = slice(jj * tn, (jj + 1) * tn)
            out = (acc_ref[jj] - mean) * rstd * g_ref[:, sl] + b_ref[:, sl]
            xo_ref[:, sl] = out
            xb_ref[:, sl] = out.astype(BF16)


def _outproj_ln(mix, w_out, layer, x, ln_g, ln_b, tm, tn):
    m = x.shape[0]
    nj = D_MODEL // tn
    row_spec = pl.BlockSpec((None, 1, D_MODEL), lambda i, j: (layer, 0, 0))
    blk = pl.BlockSpec((tm, D_MODEL), lambda i, j: (i, 0))
    return pl.pallas_call(
        functools.partial(_outproj_kernel, tn=tn, nj=nj),
        grid=(m // tm, nj),
        in_specs=[blk, pl.BlockSpec((None, D_MODEL, tn), lambda i, j: (layer, 0, j)),
                  blk, row_spec, row_spec],
        out_specs=[blk, blk],
        out_shape=[jax.ShapeDtypeStruct((m, D_MODEL), F32), jax.ShapeDtypeStruct((m, D_MODEL), BF16)],
        scratch_shapes=[pltpu.VMEM((nj, tm, tn), F32)],
        compiler_params=_cparams(("parallel", "arbitrary")),
        name="outproj_ln",
    )(mix, w_out, x, ln_g, ln_b)


def _attn_kernel(q_ref, z_ref, k_ref, v_ref, o_ref):
    lanes = [slice(h * XA_DH, (h + 1) * XA_DH) for h in range(XA_HEADS)]
    scores = [lax.dot_general(q_ref[:, sl].astype(BF16), k_ref[:, sl].astype(BF16), NT_DIMS,
                              preferred_element_type=F32) * (XA_DH ** -0.5)
              for sl in lanes]
    outs = []
    for sl, s in zip(lanes, scores):
        e = jnp.exp(s - jnp.max(s, axis=1, keepdims=True))
        pr = e / jnp.sum(e, axis=1, keepdims=True)
        outs.append(jnp.dot(pr.astype(BF16), v_ref[:, sl].astype(BF16), preferred_element_type=F32))
    for sl, o in zip(lanes, outs):
        o_ref[:, sl] = (o * _silu(z_ref[:, sl])).astype(BF16)


def _mem_attn(p, mem_k, mem_v, layer, row0, batch, seq, tc):
    nt = seq // tc
    rb0 = row0 // tc
    pspec = lambda cb: pl.BlockSpec((tc, GROUP_W), lambda b, t: (rb0 + b * nt + t, cb))
    kvspec = pl.BlockSpec((None, N_MEM, GROUP_W), lambda b, t: (layer, b, 0))
    return pl.pallas_call(
        _attn_kernel,
        grid=(batch, nt),
        in_specs=[pspec(C_XQ // GROUP_W), pspec(C_XZ // GROUP_W), kvspec, kvspec],
        out_specs=pl.BlockSpec((tc, GROUP_W), lambda b, t: (b * nt + t, 0)),
        out_shape=jax.ShapeDtypeStruct((batch * seq, GROUP_W), BF16),
        compiler_params=_cparams(("parallel", "arbitrary")),
        name="mem_attn",
    )(p, p, mem_k, mem_v)


def _hgrn_kernel(q_ref, f_ref, i_ref, z_ref, par_ref, s0_ref, o_ref, s_ref,
                 st_scr, b_scr, qs_scr, kk_scr, qt_scr, vb_scr, h_scr, *, chunk, tc, nt):
    t = pl.program_id(1)
    d = HG_D
    sub = 8

    @pl.when(t == 0)
    def _():
        for h in range(HG_HEADS):
            st_scr[h] = s0_ref[0, h].T

    log_lb = par_ref[0:1, :]
    log1m_lb = par_ref[1:2, :]
    one_m_lb = par_ref[2:3, :]
    norm_w = par_ref[3:4, :]

    fpre = f_ref[...]
    bt = log1m_lb + _log_sigmoid(fpre)
    logf = jnp.maximum(log_lb, bt) + jnp.log1p(jnp.exp(-jnp.abs(log_lb - bt)))
    b = _chunk_cumsum(logf, chunk)
    qs = _silu(q_ref[...])
    b_scr[...] = b
    qs_scr[...] = qs
    kk_scr[...] = one_m_lb * _sigmoid(-fpre)
    qt_scr[...] = (qs * jnp.exp(b)).astype(BF16)
    vb_scr[...] = i_ref[...].astype(BF16)

    rid = lax.broadcasted_iota(jnp.int32, (sub, d), 0)

    def intra(bc, qc, kc, vc):
        blocks = []
        for rb in range(chunk // sub):
            rs = slice(rb * sub, (rb + 1) * sub)
            bb, qb = bc[rs], qc[rs]
            o = jnp.zeros((sub, d), F32)
            for s in range((rb + 1) * sub):
                diff = bb - bc[s:s + 1, :]
                if s >= rb * sub:
                    diff = jnp.where(rid >= s - rb * sub, diff, -jnp.inf)
                a = jnp.sum(qb * kc[s:s + 1, :] * jnp.exp(diff), axis=1, keepdims=True)
                o = o + a * vc[s:s + 1, :]
            blocks.append(o)
        return blocks[0] if len(blocks) == 1 else jnp.concatenate(blocks, axis=0)

    def chunk_body(c, carry):
        r0 = c * chunk if isinstance(c, int) else pl.multiple_of(c * chunk, chunk)
        rows = pl.ds(r0, chunk)
        lanes = [slice(h * d, (h + 1) * d) for h in range(HG_HEADS)]
        inter, upd, decay = [], [], []
        for h, sl in enumerate(lanes):
            bc = b_scr[rows, sl]
            b_last = bc[chunk - 1:chunk, :]
            kt = (kk_scr[rows, sl] * jnp.exp(b_last - bc)).astype(BF16)
            inter.append(lax.dot_general(qt_scr[rows, sl], st_scr[h].astype(BF16), NT_DIMS,
                                         preferred_element_type=F32))
            upd.append(lax.dot_general(vb_scr[rows, sl], kt, TN_DIMS, preferred_element_type=F32))
            decay.append(jnp.exp(b_last))
        for h, sl in enumerate(lanes):
            o = intra(b_scr[rows, sl], qs_scr[rows, sl], kk_scr[rows, sl], i_ref[rows, sl])
            h_scr[rows, sl] = o + inter[h]
            st_scr[h] = st_scr[h] * decay[h] + upd[h]
        return carry

    nchunk = tc // chunk
    if nchunk == 1:
        chunk_body(0, 0)
    else:
        lax.fori_loop(0, nchunk, chunk_body, 0)

    for h in range(HG_HEADS):
        sl = slice(h * d, (h + 1) * d)
        hh = h_scr[:, sl]
        ms = jnp.mean(hh * hh, axis=1, keepdims=True)
        y = hh * lax.rsqrt(ms + 1e-5) * norm_w[:, sl] * _silu(z_ref[:, sl])
        o_ref[:, sl] = y.astype(BF16)

    @pl.when(t == nt - 1)
    def _():
        for h in range(HG_HEADS):
            s_ref[0, h] = st_scr[h].T


def _hgrn(p, par, state, layer, row0, batch, seq, tc, chunk):
    nt = seq // tc
    rb0 = row0 // tc
    pspec = lambda cb: pl.BlockSpec((tc, GROUP_W), lambda b, t: (rb0 + b * nt + t, cb))
    if state.ndim == 5:
        sspec = pl.BlockSpec((None, 1, HG_HEADS, HG_D, HG_D), lambda b, t: (layer, b, 0, 0, 0))
    else:
        sspec = pl.BlockSpec((1, HG_HEADS, HG_D, HG_D), lambda b, t: (b, 0, 0, 0))
    big = lambda dt: pltpu.VMEM((tc, GROUP_W), dt)
    return pl.pallas_call(
        functools.partial(_hgrn_kernel, chunk=chunk, tc=tc, nt=nt),
        grid=(batch, nt),
        in_specs=[pspec(C_HQ // GROUP_W), pspec(C_HF // GROUP_W), pspec(C_HI // GROUP_W), pspec(C_HZ // GROUP_W),
                  pl.BlockSpec((None, 8, GROUP_W), lambda b, t: (layer, 0, 0)), sspec],
        out_specs=[pl.BlockSpec((tc, GROUP_W), lambda b, t: (b * nt + t, 0)),
                   pl.BlockSpec((1, HG_HEADS, HG_D, HG_D), lambda b, t: (b, 0, 0, 0))],
        out_shape=[jax.ShapeDtypeStruct((batch * seq, GROUP_W), BF16),
                   jax.ShapeDtypeStruct((batch, HG_HEADS, HG_D, HG_D), F32)],
        scratch_shapes=[pltpu.VMEM((HG_HEADS, HG_D, HG_D), F32),
                        big(F32), big(F32), big(F32), big(BF16), big(BF16), big(F32)],
        compiler_params=_cparams(("parallel", "arbitrary")),
        name="hgrn2",
    )(p, p, p, p, par, state)


def _mlstm_kernel(qk_ref, v_ref, z_ref, g_ref, par_ref, bias_ref, c0_ref, n0_ref, m0_ref,
                  o_ref, c_ref, n_ref, m_ref, m_scr, h_scr, *, chunk, tc, nt):
    t = pl.program_id(1)
    L = chunk

    @pl.when(t == 0)
    def _():
        c_ref[...] = c0_ref[...]
        n_ref[...] = n0_ref[...]
        m_scr[...] = jnp.zeros(m_scr.shape, F32)
        for h in range(ML_HEADS):
            m_scr[h:h + 1, :] = jnp.broadcast_to(m0_ref[0, :, h:h + 1], (1, LANE))

    r = lax.broadcasted_iota(jnp.int32, (L, L), 0)
    c = lax.broadcasted_iota(jnp.int32, (L, L), 1)
    tril = r >= c
    triu = r <= c

    heads = range(ML_HEADS)
    nchunk = tc // L
    pre = {}
    for ci in range(nchunk):
        rows = slice(ci * L, (ci + 1) * L)
        g = g_ref[rows, :]
        gt = g.T
        for h in heads:
            ib = bias_ref[:, h:h + 1]
            fb = bias_ref[:, ML_HEADS + h:ML_HEADS + h + 1]
            i_col = g[:, h:h + 1] + ib
            i_row = gt[h:h + 1, 0:L] + ib
            lf_col = _log_sigmoid(g[:, ML_HEADS + h:ML_HEADS + h + 1] + fb)
            lf_row = _log_sigmoid(gt[ML_HEADS + h:ML_HEADS + h + 1, 0:L] + fb)
            b_col = jnp.sum(jnp.where(tril, lf_row, 0.0), axis=1, keepdims=True)
            b_row = jnp.sum(jnp.where(triu, lf_col, 0.0), axis=0, keepdims=True)
            dmat = jnp.where(tril, b_col - b_row + i_row, -jnp.inf)
            q = qk_ref[rows, h * ML_DK:(h + 1) * ML_DK]
            k = qk_ref[rows, ML_HEADS * ML_DK + h * ML_DK:ML_HEADS * ML_DK + (h + 1) * ML_DK] * (ML_DK ** -0.5)
            qb = q.astype(BF16)
            pre[ci, h] = dict(
                i_col=i_col, b_col=b_col, dmat=dmat, dmax=jnp.max(dmat, axis=1, keepdims=True),
                q=q, k=k, qb=qb, vb=v_ref[rows, h * ML_DV:(h + 1) * ML_DV].astype(BF16),
                s_qk=lax.dot_general(qb, k.astype(BF16), NT_DIMS, preferred_element_type=F32))

    for ci in range(nchunk):
        rows = slice(ci * L, (ci + 1) * L)
        mid = []
        for h in heads:
            e = pre[ci, h]
            m = m_scr[h:h + 1, 0:1]
            m_t = jnp.maximum(e["b_col"] + m, e["dmax"])
            pm = jnp.exp(e["dmat"] - m_t) * e["s_qk"]
            cst = c_ref[0, h]
            mid.append(dict(m=m, m_t=m_t, pm=pm, cst=cst,
                            pv=jnp.dot(pm.astype(BF16), e["vb"], preferred_element_type=F32),
                            qc=jnp.dot(e["qb"], cst.astype(BF16), preferred_element_type=F32)))
        for h in heads:
            e, u = pre[ci, h], mid[h]
            m, m_t, b_col = u["m"], u["m_t"], e["b_col"]
            nst = n_ref[0, h:h + 1, :]
            inter = jnp.exp(b_col + m - m_t)
            num = u["pv"] + inter * u["qc"]
            den = jnp.sum(u["pm"], axis=1, keepdims=True) + inter * jnp.sum(e["q"] * nst, axis=1, keepdims=True)
            hh = num / jnp.maximum(jnp.abs(den), jnp.exp(-m_t))
            m_new = m_t[L - 1:L, :]
            b_last = b_col[L - 1:L, :]
            wgt = jnp.exp(b_last - b_col + e["i_col"] - m_new)
            decay = jnp.exp(b_last + m - m_new)
            kw = e["k"] * wgt
            c_ref[0, h] = decay * u["cst"] + lax.dot_general(kw.astype(BF16), e["vb"], TN_DIMS,
                                                             preferred_element_type=F32)
            n_ref[0, h:h + 1, :] = decay * nst + jnp.sum(kw, axis=0, keepdims=True)
            m_scr[h:h + 1, :] = jnp.broadcast_to(m_new, (1, LANE))
            h_scr[rows, h * ML_DV:(h + 1) * ML_DV] = hh

    for h in range(ML_HEADS):
        sl = slice(h * ML_DV, (h + 1) * ML_DV)
        x = h_scr[:, sl]
        xc = x - jnp.mean(x, axis=1, keepdims=True)
        y = xc * lax.rsqrt(jnp.mean(xc * xc, axis=1, keepdims=True) + 1e-6)
        o_ref[:, sl] = (y * par_ref[:, sl] * _silu(z_ref[:, sl])).astype(BF16)

    @pl.when(t == nt - 1)
    def _():
        m_ref[0] = m_scr[...]


def _mlstm(p, norm_w, bias, c0, n0, m0, layer, row0, batch, seq, tc, chunk):
    nt = seq // tc
    rb0 = row0 // tc
    pspec = lambda cb: pl.BlockSpec((tc, GROUP_W), lambda b, t: (rb0 + b * nt + t, cb))
    if c0.ndim == 5:
        cspec = pl.BlockSpec((None, 1, ML_HEADS, ML_DK, ML_DV), lambda b, t: (layer, b, 0, 0, 0))
        nspec = pl.BlockSpec((None, 1, ML_HEADS, ML_DK), lambda b, t: (layer, b, 0, 0))
        mspec = pl.BlockSpec((None, 1, 1, ML_HEADS), lambda b, t: (layer, b, 0, 0))
    else:
        cspec = pl.BlockSpec((1, ML_HEADS, ML_DK, ML_DV), lambda b, t: (b, 0, 0, 0))
        nspec = pl.BlockSpec((1, ML_HEADS, ML_DK), lambda b, t: (b, 0, 0))
        mspec = pl.BlockSpec((1, 1, ML_HEADS), lambda b, t: (b, 0, 0))
    return pl.pallas_call(
        functools.partial(_mlstm_kernel, chunk=chunk, tc=tc, nt=nt),
        grid=(batch, nt),
        in_specs=[pspec(C_MQK // GROUP_W), pspec(C_MV // GROUP_W), pspec(C_MZ // GROUP_W),
                  pl.BlockSpec((tc, LANE), lambda b, t: (rb0 + b * nt + t, C_MG // LANE)),
                  pl.BlockSpec((None, 1, GROUP_W), lambda b, t: (layer, 0, 0)),
                  pl.BlockSpec((None, 1, 2 * ML_HEADS), lambda b, t: (layer, 0, 0)),
                  cspec, nspec, mspec],
        out_specs=[pl.BlockSpec((tc, GROUP_W), lambda b, t: (b * nt + t, 0)),
                   pl.BlockSpec((1, ML_HEADS, ML_DK, ML_DV), lambda b, t: (b, 0, 0, 0)),
                   pl.BlockSpec((1, ML_HEADS, ML_DK), lambda b, t: (b, 0, 0)),
                   pl.BlockSpec((1, 8, LANE), lambda b, t: (b, 0, 0))],
        out_shape=[jax.ShapeDtypeStruct((batch * seq, GROUP_W), BF16),
                   jax.ShapeDtypeStruct((batch, ML_HEADS, ML_DK, ML_DV), F32),
                   jax.ShapeDtypeStruct((batch, ML_HEADS, ML_DK), F32),
                   jax.ShapeDtypeStruct((batch, 8, LANE), F32)],
        scratch_shapes=[pltpu.VMEM((8, LANE), F32), pltpu.VMEM((tc, GROUP_W), F32)],
        compiler_params=_cparams(("parallel", "arbitrary")),
        name="mlstm",
    )(p, p, p, p, norm_w, bias, c0, n0, m0)


RW_CHUNK = 16


def _rwkv_kernel(r_ref, k_ref, v_ref, z_ref, x_ref, buf_ref, bufx_ref, mu_ref, mux_ref, par_ref,
                 w2_ref, a2_ref, s0_ref, o_ref, s_ref,
                 sp_scr, prev_scr, prevx_scr, kh_scr, rh_scr, ki_scr, ai_scr, kd_scr, ad_scr, vb_scr,
                 gl_scr, y_scr, bonus_scr, *, tc, tp, nt):
    t = pl.program_id(1)
    L = RW_CHUNK
    W = GROUP_W

    lane2 = lax.broadcasted_iota(jnp.int32, (LANE, LANE), 1)
    row2 = lax.broadcasted_iota(jnp.int32, (LANE, LANE), 0)
    diag_blocks = (lane2 < RW_HEAD) == (row2 < RW_HEAD)
    seg_ones = jnp.where(diag_blocks, 1.0, 0.0).astype(BF16)

    @pl.when(t == 0)
    def _():
        prev_scr[...] = buf_ref[0]
        prevx_scr[...] = bufx_ref[0]
        zero = jnp.zeros((RW_HEAD, RW_HEAD), F32)
        for pr in range(RW_PAIRS):
            top = jnp.concatenate([s0_ref[0, 2 * pr], zero], axis=1)
            bot = jnp.concatenate([zero, s0_ref[0, 2 * pr + 1]], axis=1)
            sp_scr[pr] = jnp.concatenate([top, bot], axis=0)

    def seg_sum(x):
        parts = [_dot_sel_right(x[:, i * LANE:(i + 1) * LANE], seg_ones, 2) for i in range(x.shape[1] // LANE)]
        return jnp.concatenate(parts, axis=1)

    def shifted(cur, prev_row):
        if tc == 1:
            return prev_row
        rid = lax.broadcasted_iota(jnp.int32, cur.shape, 0)
        return jnp.where(rid == 0, prev_row, pltpu.roll(cur, 1, axis=0))

    def mix(cur, prev_row, mu):
        return cur + (shifted(cur, prev_row) - cur) * mu

    pr_ = r_ref[...]
    pk_ = k_ref[...]
    pv_ = v_ref[...]
    px_ = x_ref[...]
    xr = mix(pr_, prev_scr[:, 0:W], mu_ref[:, 0:W])
    xk = mix(pk_, prev_scr[:, W:2 * W], mu_ref[:, W:2 * W])
    xv = mix(pv_, prev_scr[:, 2 * W:3 * W], mu_ref[:, 2 * W:3 * W])
    xx = mix(px_, prevx_scr[...], mux_ref[...])
    prev_scr[:, 0:W] = pr_[tc - 1:tc, :]
    prev_scr[:, W:2 * W] = pk_[tc - 1:tc, :]
    prev_scr[:, 2 * W:3 * W] = pv_[tc - 1:tc, :]
    prevx_scr[...] = px_[tc - 1:tc, :]

    w0 = par_ref[0:1, :]
    a0 = par_ref[1:2, :]
    k_k = par_ref[2:3, :]
    k_a = par_ref[3:4, :]
    r_k = par_ref[4:5, :]
    ln_w = par_ref[5:6, :]
    ln_b = par_ref[6:7, :]

    wlin = w0 + jnp.dot(jnp.tanh(xx).astype(BF16), w2_ref[...], preferred_element_type=F32)
    wdec = -_softplus(-wlin) - 0.5
    logd = -jnp.exp(wdec)
    a = _sigmoid(a0 + jnp.dot(xx.astype(BF16), a2_ref[...], preferred_element_type=F32))
    kk = xk * k_k
    kk = kk / jnp.maximum(jnp.sqrt(seg_sum(kk * kk)), 1e-12)
    kp = xk * (1.0 + (a - 1.0) * k_a)
    alpha = a * kk
    bonus = seg_sum(xr * kp * r_k) * xv

    if tp > tc:
        pad = lambda u: jnp.concatenate([u, jnp.zeros((tp - tc, u.shape[1]), F32)], axis=0)
        logd, kk, kp, alpha, xr, xv = pad(logd), pad(kk), pad(kp), pad(alpha), pad(xr), pad(xv)
        bonus_scr[...] = pad(bonus)
    else:
        bonus_scr[...] = bonus

    tri, ones = _chunk_masks(tp, L)
    g = _dot_sel_left(tri, logd, 3)
    gl = _dot_sel_left(ones, logd, 3)
    einv = jnp.exp(-g)
    egl = jnp.exp(gl - g)
    kh_scr[...] = (kk * jnp.exp(g - logd)).astype(BF16)
    rh_scr[...] = (xr * jnp.exp(g)).astype(BF16)
    ki_scr[...] = (kp * einv).astype(BF16)
    ai_scr[...] = (alpha * einv).astype(BF16)
    kd_scr[...] = (kp * egl).astype(BF16)
    ad_scr[...] = (alpha * egl).astype(BF16)
    vb_scr[...] = xv.astype(BF16)
    gl_scr[...] = jnp.exp(gl)

    lane_l = lax.broadcasted_iota(jnp.int32, (L, LANE), 1)
    head_a = lane_l < RW_HEAD
    rl = lax.broadcasted_iota(jnp.int32, (L, L), 0)
    cl = lax.broadcasted_iota(jnp.int32, (L, L), 1)
    strict = rl > cl
    incl = rl >= cl
    zb = jnp.zeros((L, LANE), BF16)

    def chunk_body(c, carry):
        r0 = c * L if isinstance(c, int) else pl.multiple_of(c * L, L)
        rows = pl.ds(r0, L)
        pairs = range(RW_PAIRS)
        lanes = [slice(pr * LANE, (pr + 1) * LANE) for pr in pairs]
        vb = [vb_scr[rows, sl] for sl in lanes]
        gm, ks = [], []
        for pr in pairs:
            sl = lanes[pr]
            kh = kh_scr[rows, sl]
            rh = rh_scr[rows, sl]
            x4 = jnp.concatenate([jnp.where(head_a, kh, zb), jnp.where(head_a, zb, kh),
                                  jnp.where(head_a, rh, zb), jnp.where(head_a, zb, rh)], axis=0)
            y2 = jnp.concatenate([ai_scr[rows, sl], ki_scr[rows, sl]], axis=0)
            gm.append(lax.dot_general(x4, y2, NT_DIMS, preferred_element_type=F32))
            ks.append(lax.dot_general(jnp.concatenate([kh, rh], axis=0), sp_scr[pr].astype(BF16), NT_DIMS,
                                      preferred_element_type=F32))
        mv = []
        for pr in pairs:
            g = gm[pr]
            m_ab = jnp.concatenate([jnp.where(strict, g[0:L, L:2 * L], 0.0),
                                    jnp.where(strict, g[L:2 * L, L:2 * L], 0.0)], axis=0)
            mv.append(jnp.dot(m_ab.astype(BF16), vb[pr], preferred_element_type=F32))
        vw = []
        for pr in pairs:
            g = gm[pr]
            n_a = jnp.where(strict, g[0:L, 0:L], 0.0)
            n_b = jnp.where(strict, g[L:2 * L, 0:L], 0.0)
            w = ks[pr][0:L] + jnp.where(head_a, mv[pr][0:L], mv[pr][L:2 * L])
            for s in range(L - 1):
                coef = jnp.where(head_a, n_a[:, s:s + 1], n_b[:, s:s + 1])
                w = w - coef * w[s:s + 1, :]
            vw.append(jnp.concatenate([vb[pr], w.astype(BF16)], axis=0))
        yy, upd = [], []
        for pr in pairs:
            g = gm[pr]
            sl = lanes[pr]
            cm = jnp.concatenate([
                jnp.concatenate([jnp.where(incl, g[2 * L:3 * L, L:2 * L], 0.0),
                                 -jnp.where(incl, g[2 * L:3 * L, 0:L], 0.0)], axis=1),
                jnp.concatenate([jnp.where(incl, g[3 * L:4 * L, L:2 * L], 0.0),
                                 -jnp.where(incl, g[3 * L:4 * L, 0:L], 0.0)], axis=1)], axis=0)
            yy.append(jnp.dot(cm.astype(BF16), vw[pr], preferred_element_type=F32))
            kd_ad = jnp.concatenate([kd_scr[rows, sl], -ad_scr[rows, sl]], axis=0)
            upd.append(lax.dot_general(vw[pr], kd_ad, TN_DIMS, preferred_element_type=F32))
        for pr in pairs:
            sl = lanes[pr]
            y_scr[rows, sl] = ks[pr][L:2 * L] + jnp.where(head_a, yy[pr][0:L], yy[pr][L:2 * L])
            sp_scr[pr] = sp_scr[pr] * gl_scr[pl.ds(r0, 1), sl] + jnp.where(diag_blocks, upd[pr], 0.0)
        return carry

    nchunk = tp // L
    if nchunk == 1:
        chunk_body(0, 0)
    else:
        lax.fori_loop(0, nchunk, chunk_body, 0)

    y = y_scr[0:tc, :]
    yc = y - seg_sum(y) * (1.0 / RW_HEAD)
    yn = yc * lax.rsqrt(seg_sum(yc * yc) * (1.0 / RW_HEAD) + RW_GN_EPS)
    out = (yn * ln_w + ln_b + bonus_scr[0:tc, :]) * _silu(z_ref[...])
    o_ref[...] = out.astype(BF16)

    @pl.when(t == nt - 1)
    def _():
        for pr in range(RW_PAIRS):
            sp = sp_scr[pr]
            s_ref[0, 2 * pr] = sp[0:RW_HEAD, 0:RW_HEAD]
            s_ref[0, 2 * pr + 1] = sp[RW_HEAD:LANE, RW_HEAD:LANE]


def _rwkv(p, buf, bufx, mu, mux, par, w2p, a2p, state, layer, row0, batch, seq, tc):
    nt = seq // tc
    rb0 = row0 // tc
    tp = max(tc, RW_CHUNK)
    pspec = lambda cb: pl.BlockSpec((tc, GROUP_W), lambda b, t: (rb0 + b * nt + t, cb))
    if state.ndim == 5:
        sspec = pl.BlockSpec((None, 1, RW_HEADS, RW_HEAD, RW_HEAD), lambda b, t: (layer, b, 0, 0, 0))
    else:
        sspec = pl.BlockSpec((1, RW_HEADS, RW_HEAD, RW_HEAD), lambda b, t: (b, 0, 0, 0))
    lay = lambda shape: pl.BlockSpec((None,) + shape, lambda b, t: (layer,) + (0,) * len(shape))
    big = lambda dt: pltpu.VMEM((tp, GROUP_W), dt)
    return pl.pallas_call(
        functools.partial(_rwkv_kernel, tc=tc, tp=tp, nt=nt),
        grid=(batch, nt),
        in_specs=[pspec(C_RR // GROUP_W), pspec(C_RK // GROUP_W), pspec(C_RV // GROUP_W), pspec(C_RZ // GROUP_W),
                  pl.BlockSpec((tc, LANE), lambda b, t: (rb0 + b * nt + t, C_RX // LANE)),
                  pl.BlockSpec((1, 1, 3 * GROUP_W), lambda b, t: (b, 0, 0)),
                  pl.BlockSpec((1, 1, LANE), lambda b, t: (b, 0, 0)),
                  lay((1, 3 * GROUP_W)), lay((1, LANE)), lay((8, GROUP_W)),
                  lay((LANE, GROUP_W)), lay((LANE, GROUP_W)), sspec],
        out_specs=[pl.BlockSpec((tc, GROUP_W), lambda b, t: (b * nt + t, 0)),
                   pl.BlockSpec((1, RW_HEADS, RW_HEAD, RW_HEAD), lambda b, t: (b, 0, 0, 0))],
        out_shape=[jax.ShapeDtypeStruct((batch * seq, GROUP_W), BF16),
                   jax.ShapeDtypeStruct((batch, RW_HEADS, RW_HEAD, RW_HEAD), F32)],
        scratch_shapes=[pltpu.VMEM((RW_PAIRS, LANE, LANE), F32),
                        pltpu.VMEM((1, 3 * GROUP_W), F32), pltpu.VMEM((1, LANE), F32),
                        big(BF16), big(BF16), big(BF16), big(BF16), big(BF16), big(BF16), big(BF16),
                        big(F32), big(F32), big(F32)],
        compiler_params=_cparams(("parallel", "arbitrary")),
        name="rwkv7",
    )(p, p, p, p, p, buf, bufx, mu, mux, par, w2p, a2p, state)


def _relayout_w_in(w_in):
    seg = lambda a, b: w_in[:, :, a:b]
    pad = jnp.zeros(w_in.shape[:2] + (NP - N_IN,), w_in.dtype)
    parts = [seg(0, 6144),
             seg(6152, 7176),
             seg(7176, 10248),
             seg(10376, 11400),
             seg(11400, 13448),
             seg(10248, 10376),
             seg(6144, 6152),
             pad]
    return jnp.concatenate(parts, axis=-1).astype(BF16)


def _pad_rows(x, rows):
    return jnp.concatenate([x, jnp.zeros((x.shape[0], rows - x.shape[1], x.shape[2]), x.dtype)], axis=1)


def _trunk_layer(l, x_f32, x_bf, w_in_r, w_out_b, hg_par, ml_nw, ml_bias, rw_mu, rw_mux, rw_par, rw_w2, rw_a2,
                 ln_g, ln_b, groups, cfg):
    p = _matmul(x_bf, w_in_r, l, cfg["mm_tm"], cfg["mm_tn"])
    mixes = [[], [], [], []]
    new_states = []
    for grp in groups:
        row0, batch, seq = grp["row0"], grp["batch"], grp["seq"]
        mix_a, s_hg = _hgrn(p, hg_par, grp["hg"], l, row0, batch, seq, grp["hg_tc"], grp["hg_chunk"])
        mix_b, c_ml, n_ml, m_ml = _mlstm(p, ml_nw, ml_bias, grp["ml_c"], grp["ml_n"], grp["ml_m"], l, row0, batch,
                                         seq, grp["ml_tc"], grp["ml_chunk"])
        mix_c, s_rw = _rwkv(p, grp["rw_buf"][l], grp["rw_bufx"][l], rw_mu, rw_mux, rw_par, rw_w2, rw_a2,
                            grp["rw"], l, row0, batch, seq, grp["rw_tc"])
        mix_d = _mem_attn(p, grp["mem_k"], grp["mem_v"], l, row0, batch, seq, grp["xa_tc"])
        last_rkv = lax.slice(p, (row0 + seq - 1, C_RR), (row0 + batch * seq, C_RR + 3 * GROUP_W), (seq, 1))
        last_x = lax.slice(p, (row0 + seq - 1, C_RX), (row0 + batch * seq, C_RX + 2 * RW_LORA), (seq, 1))
        buf_new = jnp.concatenate([last_rkv, last_x], axis=-1)
        for lst, mx in zip(mixes, (mix_a, mix_b, mix_c, mix_d)):
            lst.append(mx)
        new_states.append((s_hg, c_ml, n_ml, m_ml[:, :ML_HEADS, 0], s_rw, buf_new))
    mix = jnp.concatenate([jnp.concatenate(lst, axis=0) for lst in mixes], axis=1)
    x_f32, x_bf = _outproj_ln(mix, w_out_b, l, x_f32, ln_g, ln_b, cfg["op_tm"], cfg["op_tn"])
    return x_f32, x_bf, new_states


def kernel(x_prompt, x_sample, mem_prompt, state_hgrn, state_mlstm_C, state_mlstm_n, state_mlstm_m, state_rwkv, state_rwkv_shift, cache_mem_k, cache_mem_v, w_in, hgrn_lb, hgrn_norm_w, mlstm_ig_b, mlstm_fg_b, mlstm_norm_w, rwkv_mu, rwkv_w0, rwkv_w2, rwkv_a0, rwkv_a2, rwkv_k_k, rwkv_k_a, rwkv_r_k, rwkv_ln_w, rwkv_ln_b, mem_wk, mem_wv, w_out, ln_g, ln_b):
    bp, tp_, _ = x_prompt.shape
    bs, ts, _ = x_sample.shape
    depth = w_in.shape[0]
    mp, ms = bp * tp_, bs * ts

    w_in_r = _relayout_w_in(w_in)
    w_out_b = w_out.astype(BF16)
    lb_all = jnp.cumsum(jax.nn.softmax(hgrn_lb.astype(F32), axis=0), axis=0)
    lb_all = lb_all - lb_all[0]
    zrow = jnp.zeros_like(lb_all)
    hg_par = jnp.stack([jnp.log(lb_all), jnp.log1p(-lb_all), 1.0 - lb_all, hgrn_norm_w.astype(F32),
                        zrow, zrow, zrow, zrow], axis=1)
    ml_nw = mlstm_norm_w.astype(F32)[:, None, :]
    ml_bias = jnp.concatenate([mlstm_ig_b, mlstm_fg_b], axis=-1).astype(F32)[:, None, :]
    rw_mu = rwkv_mu[:, None, :3 * GROUP_W].astype(F32)
    rw_mux = rwkv_mu[:, None, 3 * GROUP_W:].astype(F32)
    zr = jnp.zeros((depth, GROUP_W), F32)
    rw_par = jnp.stack([rwkv_w0, rwkv_a0, rwkv_k_k, rwkv_k_a, rwkv_r_k.reshape(depth, GROUP_W), rwkv_ln_w,
                        rwkv_ln_b, zr], axis=1).astype(F32)
    zl = jnp.zeros((depth, RW_LORA, GROUP_W), F32)
    rw_w2 = jnp.concatenate([rwkv_w2.astype(F32), zl], axis=1).astype(BF16)
    rw_a2 = jnp.concatenate([zl, rwkv_a2.astype(F32)], axis=1).astype(BF16)
    ln_g3 = ln_g.astype(F32)[:, None, :]
    ln_b3 = ln_b.astype(F32)[:, None, :]

    mem_x = mem_prompt.reshape(bp * N_MEM, D_MODEL).astype(BF16)
    mk_p = _matmul_layers(mem_x, mem_wk, 256)
    mv_p = _matmul_layers(mem_x, mem_wv, 256)
    mk_out = mk_p.reshape(depth, bp, N_MEM, XA_HEADS, XA_DH)
    mv_out = mv_p.reshape(depth, bp, N_MEM, XA_HEADS, XA_DH)

    def split_buf(buf):
        return buf[:, :, None, :3 * GROUP_W].astype(F32), buf[:, :, None, 3 * GROUP_W:].astype(F32)

    zbuf, zbufx = split_buf(jnp.zeros((depth, bp, RW_SHIFT_W), F32))
    sbuf, sbufx = split_buf(state_rwkv_shift)
    prompt = dict(row0=0, batch=bp, seq=tp_,
                  hg=jnp.zeros((bp, HG_HEADS, HG_D, HG_D), F32), hg_tc=256, hg_chunk=16,
                  ml_c=jnp.zeros((bp, ML_HEADS, ML_DK, ML_DV), F32), ml_n=jnp.zeros((bp, ML_HEADS, ML_DK), F32),
                  ml_m=jnp.zeros((bp, 1, ML_HEADS), F32), ml_tc=256, ml_chunk=64,
                  rw=jnp.zeros((bp, RW_HEADS, RW_HEAD, RW_HEAD), F32), rw_buf=zbuf, rw_bufx=zbufx, rw_tc=128,
                  mem_k=mk_p, mem_v=mv_p, xa_tc=512)
    sample = dict(row0=mp, batch=bs, seq=ts,
                  hg=state_hgrn, hg_tc=ts, hg_chunk=ts,
                  ml_c=state_mlstm_C, ml_n=state_mlstm_n, ml_m=state_mlstm_m[:, :, None, :], ml_tc=ts, ml_chunk=ts,
                  rw=state_rwkv, rw_buf=sbuf, rw_bufx=sbufx, rw_tc=ts,
                  mem_k=cache_mem_k.astype(BF16).reshape(depth, bs * N_MEM, GROUP_W),
                  mem_v=cache_mem_v.astype(BF16).reshape(depth, bs * N_MEM, GROUP_W), xa_tc=ts)
    cfg = dict(mm_tm=1024, mm_tn=256, op_tm=384, op_tn=512)

    x_f32 = jnp.concatenate([x_prompt.reshape(mp, D_MODEL), x_sample.reshape(ms, D_MODEL)], axis=0).astype(F32)
    x_bf = x_f32.astype(BF16)
    per_layer = []
    for l in range(depth):
        x_f32, x_bf, st = _trunk_layer(l, x_f32, x_bf, w_in_r, w_out_b, hg_par, ml_nw, ml_bias, rw_mu, rw_mux,
                                       rw_par, rw_w2, rw_a2, ln_g3, ln_b3, [prompt, sample], cfg)
        per_layer.append(st)

    def stack(g, i):
        return jnp.stack([per_layer[l][g][i] for l in range(depth)], axis=0)

    y_prompt = x_f32[:mp].reshape(bp, tp_, D_MODEL)
    y_sample = x_f32[mp:].reshape(bs, ts, D_MODEL)
    p_states = tuple(stack(0, i) for i in range(6))
    s_states = tuple(stack(1, i) for i in range(6))
    return (y_prompt, y_sample) + p_states + (mk_out, mv_out) + s_states
```

```python
import functools
import math

import jax
import jax.numpy as jnp
from jax import lax
from jax.experimental import pallas as pl
from jax.experimental.pallas import tpu as pltpu

F32 = jnp.float32
BF16 = jnp.bfloat16
HI = lax.Precision.HIGHEST

D_MODEL = 4096
DEPTH = 4
GROUP_W = D_MODEL // 4
N_MEM = 256
HG_HEADS, HG_D = 8, 128
ML_HEADS, ML_DK, ML_DV = 4, 128, 256
RW_HEADS, RW_HEAD, RW_LORA = 16, 64, 64
RW_PAIRS = RW_HEADS // 2
XA_HEADS, XA_DH = 4, 256
RW_SHIFT_W = 3 * GROUP_W + 2 * RW_LORA
N_IN = 13448
DN_ALPHA = (2.0 * DEPTH) ** 0.25
LN_EPS = 1e-5
RW_GN_EPS = 64e-5

C_HQ, C_HF, C_HI, C_HZ = 0, 1024, 2048, 3072
C_MQK, C_MV, C_MZ = 4096, 5120, 6144
C_RR, C_RK, C_RV, C_RZ = 7168, 8192, 9216, 10240
C_XQ, C_XZ = 11264, 12288
NP_MAIN = 13312
ORIG_MG, ORIG_MZ, ORIG_RX, ORIG_RZ = 6144, 6152, 10248, 10376
CB_RX, CB_MG = 0, 1
LANE = 128

VMEM_LIMIT = 60 * 1024 * 1024

NT_DIMS = (((1,), (1,)), ((), ()))
TN_DIMS = (((0,), (0,)), ((), ()))


def _cparams(sem):
    return pltpu.CompilerParams(dimension_semantics=sem, vmem_limit_bytes=VMEM_LIMIT)


def _sigmoid(x):
    return jax.nn.sigmoid(x)


def _silu(x):
    return x * _sigmoid(x)


def _log_sigmoid(x):
    return jnp.minimum(x, 0.0) - jnp.log1p(jnp.exp(-jnp.abs(x)))


def _softplus(x):
    return jnp.maximum(x, 0.0) + jnp.log1p(jnp.exp(-jnp.abs(x)))


def _chunk_masks(n, chunk):
    r = lax.broadcasted_iota(jnp.int32, (n, n), 0)
    c = lax.broadcasted_iota(jnp.int32, (n, n), 1)
    sh = int(math.log2(chunk))
    same = lax.shift_right_logical(r, sh) == lax.shift_right_logical(c, sh)
    tri = jnp.where(same & (c <= r), 1.0, 0.0).astype(BF16)
    ones = jnp.where(same, 1.0, 0.0).astype(BF16)
    return tri, ones


def _split_bf16(x, terms):
    parts = []
    for i in range(terms):
        part = x.astype(BF16)
        parts.append(part)
        if i + 1 < terms:
            x = x - part.astype(F32)
    return parts


def _dot_sel_left(sel, x, terms):
    return sum(jnp.dot(sel, part, preferred_element_type=F32) for part in _split_bf16(x, terms))


def _dot_sel_right(x, sel, terms):
    return sum(jnp.dot(part, sel, preferred_element_type=F32) for part in _split_bf16(x, terms))


def _chunk_cumsum(x, chunk):
    rid = lax.broadcasted_iota(jnp.int32, x.shape, 0) & (chunk - 1)
    step = 1
    while step < chunk:
        x = x + jnp.where(rid >= step, pltpu.roll(x, step, axis=0), 0.0)
        step *= 2
    return x


def _mm_kernel(x_ref, w_ref, o_ref):
    o_ref[...] = jnp.dot(x_ref[...], w_ref[...].astype(BF16), preferred_element_type=F32)


def _mm_nt_kernel(x_ref, w_ref, o_ref):
    o_ref[...] = lax.dot_general(x_ref[...], w_ref[0].astype(BF16), NT_DIMS, preferred_element_type=F32)


def _proj_matmul(x, w_t, layer, tm, tn):
    m, k = x.shape

    def w_index(i, j):
        col = j * tn
        off = jnp.where(col >= C_RZ, ORIG_RZ - C_RZ, jnp.where(col >= C_MZ, ORIG_MZ - C_MZ, 0))
        return (layer, pl.multiple_of(col + off, 8), 0)

    return pl.pallas_call(
        _mm_nt_kernel,
        grid=(m // tm, NP_MAIN // tn),
        in_specs=[pl.BlockSpec((tm, k), lambda i, j: (i, 0)),
                  pl.BlockSpec((pl.Element(1), pl.Element(tn), pl.Element(k)), w_index)],
        out_specs=pl.BlockSpec((tm, tn), lambda i, j: (i, j)),
        out_shape=jax.ShapeDtypeStruct((m, NP_MAIN), F32),
        compiler_params=_cparams(("parallel", "arbitrary")),
        name="proj_matmul",
    )(x, w_t)


def _proj_small(x, w_t, layer, tm):
    m, k = x.shape

    def w_index(i, j):
        return (layer, pl.multiple_of(jnp.where(j == 0, ORIG_RX, ORIG_MG), 8), 0)

    return pl.pallas_call(
        _mm_nt_kernel,
        grid=(m // tm, 2),
        in_specs=[pl.BlockSpec((tm, k), lambda i, j: (i, 0)),
                  pl.BlockSpec((pl.Element(1), pl.Element(LANE), pl.Element(k)), w_index)],
        out_specs=pl.BlockSpec((tm, LANE), lambda i, j: (i, j)),
        out_shape=jax.ShapeDtypeStruct((m, 2 * LANE), F32),
        compiler_params=_cparams(("parallel", "arbitrary")),
        name="proj_small",
    )(x, w_t)


def _matmul_layers(x, w, tn):
    m, k = x.shape
    depth, _, n = w.shape
    return pl.pallas_call(
        _mm_kernel,
        grid=(depth, n // tn),
        in_specs=[pl.BlockSpec((m, k), lambda l, j: (0, 0)),
                  pl.BlockSpec((None, k, tn), lambda l, j: (l, 0, j))],
        out_specs=pl.BlockSpec((None, m, tn), lambda l, j: (l, 0, j)),
        out_shape=jax.ShapeDtypeStruct((depth, m, n), F32),
        compiler_params=_cparams(("parallel", "arbitrary")),
        name="mem_kv_matmul",
    )(x, w)


def _outproj_kernel(mix_ref, w_ref, x_ref, g_ref, b_ref, xo_ref, xb_ref, acc_ref, *, tn, nj):
    j = pl.program_id(1)
    acc_ref[j] = jnp.dot(mix_ref[...], w_ref[...], preferred_element_type=F32)

    @pl.when(j == nj - 1)
    def _():
        tm = acc_ref.shape[1]
        tot = jnp.zeros((tm, 1), F32)
        for jj in range(nj):
            y = DN_ALPHA * x_ref[:, jj * tn:(jj + 1) * tn] + acc_ref[jj]
            acc_ref[jj] = y
            tot = tot + jnp.sum(y, axis=1, keepdims=True)
        mean = tot * (1.0 / D_MODEL)
        sq = jnp.zeros((tm, 1), F32)
        for jj in range(nj):
            yc = acc_ref[jj] - mean
            sq = sq + jnp.sum(yc * yc, axis=1, keepdims=True)
        rstd = lax.rsqrt(sq * (1.0 / D_MODEL) + LN_EPS)
        for jj in range(nj):
            sl = slice(jj * tn, (jj + 1) * tn)
            out = (acc_ref[jj] - mean) * rstd * g_ref[:, sl] + b_ref[:, sl]
            xo_ref[:, sl] = out
            xb_ref[:, sl] = out.astype(BF16)


def _outproj_ln(mix, w_out, layer, x, ln_g, ln_b, tm, tn):
    m = x.shape[0]
    nj = D_MODEL // tn
    row_spec = pl.BlockSpec((None, 1, D_MODEL), lambda i, j: (layer, 0, 0))
    blk = pl.BlockSpec((tm, D_MODEL), lambda i, j: (i, 0))
    return pl.pallas_call(
        functools.partial(_outproj_kernel, tn=tn, nj=nj),
        grid=(m // tm, nj),
        in_specs=[blk, pl.BlockSpec((None, D_MODEL, tn), lambda i, j: (layer, 0, j)),
                  blk, row_spec, row_spec],
        out_specs=[blk, blk],
        out_shape=[jax.ShapeDtypeStruct((m, D_MODEL), F32), jax.ShapeDtypeStruct((m, D_MODEL), BF16)],
        scratch_shapes=[pltpu.VMEM((nj, tm, tn), F32)],
        compiler_params=_cparams(("parallel", "arbitrary")),
        name="outproj_ln",
    )(mix, w_out, x, ln_g, ln_b)


def _attn_kernel(q_ref, z_ref, k_ref, v_ref, _mix_ref, o_ref):
    lanes = [slice(h * XA_DH, (h + 1) * XA_DH) for h in range(XA_HEADS)]
    scores = [lax.dot_general(q_ref[:, sl].astype(BF16), k_ref[:, sl].astype(BF16), NT_DIMS,
                              preferred_element_type=F32) * (XA_DH ** -0.5)
              for sl in lanes]
    outs = []
    for sl, s in zip(lanes, scores):
        e = jnp.exp(s - jnp.max(s, axis=1, keepdims=True))
        pr = e / jnp.sum(e, axis=1, keepdims=True)
        outs.append(jnp.dot(pr.astype(BF16), v_ref[:, sl].astype(BF16), preferred_element_type=F32))
    for sl, o in zip(lanes, outs):
        o_ref[:, sl] = (o * _silu(z_ref[:, sl])).astype(BF16)


def _mem_attn(p, mem_k, mem_v, mix, layer, row0, batch, seq, tc):
    nt = seq // tc
    rb0 = row0 // tc
    pspec = lambda cb: pl.BlockSpec((tc, GROUP_W), lambda b, t: (rb0 + b * nt + t, cb))
    kvspec = pl.BlockSpec((None, N_MEM, GROUP_W), lambda b, t: (layer, b, 0))
    return pl.pallas_call(
        _attn_kernel,
        grid=(batch, nt),
        in_specs=[pspec(C_XQ // GROUP_W), pspec(C_XZ // GROUP_W), kvspec, kvspec, ANY_SPEC],
        out_specs=_mix_out_spec(tc, nt, rb0, 3),
        out_shape=_sds(mix),
        input_output_aliases={4: 0},
        compiler_params=_cparams(("parallel", "arbitrary")),
        name="mem_attn",
    )(p, p, mem_k, mem_v, mix)


XA_SUB = XA_DH // LANE
XA_ROWS = N_MEM * XA_SUB * XA_HEADS


def _attn_cache_kernel(q_ref, z_ref, k_ref, v_ref, _mix_ref, o_ref):
    t = q_ref.shape[0]
    tiles = XA_HEADS * XA_SUB
    qx = jnp.concatenate([q_ref[:, c * LANE:(c + 1) * LANE] for c in range(tiles)], axis=0).astype(BF16)
    s_all = lax.dot_general(qx, k_ref[...].astype(BF16), NT_DIMS, preferred_element_type=F32)
    col = lax.broadcasted_iota(jnp.int32, (t, XA_ROWS), 1)
    col_head = col & (XA_HEADS - 1)
    col_sub = lax.shift_right_logical(col, 2) & (XA_SUB - 1)
    probs = []
    for h in range(XA_HEADS):
        mine = col_head == h
        part = jnp.zeros((t, XA_ROWS), F32)
        for s in range(XA_SUB):
            rows = slice((h * XA_SUB + s) * t, (h * XA_SUB + s + 1) * t)
            part = part + jnp.where(mine & (col_sub == s), s_all[rows], 0.0)
        other = jnp.where(col_sub == 0, pltpu.roll(part, XA_ROWS - XA_HEADS, axis=1),
                          pltpu.roll(part, XA_HEADS, axis=1))
        sc = jnp.where(mine, (part + other) * (XA_DH ** -0.5), -jnp.inf)
        e = jnp.exp(sc - jnp.max(sc, axis=1, keepdims=True))
        probs.append(e / (jnp.sum(e, axis=1, keepdims=True) * (1.0 / XA_SUB)))
    pr = jnp.concatenate(probs, axis=0)
    sub_all = lax.shift_right_logical(lax.broadcasted_iota(jnp.int32, pr.shape, 1), 2) & (XA_SUB - 1)
    vb = v_ref[...].astype(BF16)
    outs = [jnp.dot(jnp.where(sub_all == s, pr, 0.0).astype(BF16), vb, preferred_element_type=F32)
            for s in range(XA_SUB)]
    for h in range(XA_HEADS):
        for s in range(XA_SUB):
            sl = slice((h * XA_SUB + s) * LANE, (h * XA_SUB + s + 1) * LANE)
            o_ref[:, sl] = (outs[s][h * t:(h + 1) * t] * _silu(z_ref[:, sl])).astype(BF16)


def _mem_attn_cache(p, cache_k, cache_v, mix, layer, row0, batch, seq):
    assert XA_SUB == 2 and XA_HEADS == 4
    rb0 = row0 // seq
    pspec = lambda cb: pl.BlockSpec((seq, GROUP_W), lambda b, t: (rb0 + b, cb))
    kvspec = pl.BlockSpec((None, XA_ROWS, LANE), lambda b, t: (layer, b, 0))
    return pl.pallas_call(
        _attn_cache_kernel,
        grid=(batch, 1),
        in_specs=[pspec(C_XQ // GROUP_W), pspec(C_XZ // GROUP_W), kvspec, kvspec, ANY_SPEC],
        out_specs=_mix_out_spec(seq, 1, rb0, 3),
        out_shape=_sds(mix),
        input_output_aliases={4: 0},
        compiler_params=_cparams(("parallel", "arbitrary")),
        name="mem_attn_cache",
    )(p, p, cache_k, cache_v, mix)


def _hgrn_kernel(q_ref, f_ref, i_ref, z_ref, par_ref, s0_ref, _mix_ref, _acc_ref, o_ref, s_ref,
                 st_scr, b_scr, qs_scr, kk_scr, qt_scr, vb_scr, h_scr, *, chunk, tc, nt):
    t = pl.program_id(1)
    d = HG_D
    sub = 8

    @pl.when(t == 0)
    def _():
        for h in range(HG_HEADS):
            st_scr[h] = s0_ref[0, h].T

    log_lb = par_ref[0:1, :]
    log1m_lb = par_ref[1:2, :]
    one_m_lb = par_ref[2:3, :]
    norm_w = par_ref[3:4, :]

    fpre = f_ref[...]
    bt = log1m_lb + _log_sigmoid(fpre)
    logf = jnp.maximum(log_lb, bt) + jnp.log1p(jnp.exp(-jnp.abs(log_lb - bt)))
    b = _chunk_cumsum(logf, chunk)
    qs = _silu(q_ref[...])
    b_scr[...] = b
    qs_scr[...] = qs
    kk_scr[...] = one_m_lb * _sigmoid(-fpre)
    qt_scr[...] = (qs * jnp.exp(b)).astype(BF16)
    vb_scr[...] = i_ref[...].astype(BF16)

    rid = lax.broadcasted_iota(jnp.int32, (sub, d), 0)

    def intra(bc, qc, kc, vc):
        blocks = []
        for rb in range(chunk // sub):
            rs = slice(rb * sub, (rb + 1) * sub)
            bb, qb = bc[rs], qc[rs]
            o = jnp.zeros((sub, d), F32)
            for s in range((rb + 1) * sub):
                diff = bb - bc[s:s + 1, :]
                if s >= rb * sub:
                    diff = jnp.where(rid >= s - rb * sub, diff, -jnp.inf)
                a = jnp.sum(qb * kc[s:s + 1, :] * jnp.exp(diff), axis=1, keepdims=True)
                o = o + a * vc[s:s + 1, :]
            blocks.append(o)
        return blocks[0] if len(blocks) == 1 else jnp.concatenate(blocks, axis=0)

    def chunk_body(c, carry):
        r0 = c * chunk if isinstance(c, int) else pl.multiple_of(c * chunk, chunk)
        rows = pl.ds(r0, chunk)
        lanes = [slice(h * d, (h + 1) * d) for h in range(HG_HEADS)]
        inter, upd, decay = [], [], []
        for h, sl in enumerate(lanes):
            bc = b_scr[rows, sl]
            b_last = bc[chunk - 1:chunk, :]
            kt = (kk_scr[rows, sl] * jnp.exp(b_last - bc)).astype(BF16)
            inter.append(lax.dot_general(qt_scr[rows, sl], st_scr[h].astype(BF16), NT_DIMS,
                                         preferred_element_type=F32))
            upd.append(lax.dot_general(vb_scr[rows, sl], kt, TN_DIMS, preferred_element_type=F32))
            decay.append(jnp.exp(b_last))
        for h, sl in enumerate(lanes):
            o = intra(b_scr[rows, sl], qs_scr[rows, sl], kk_scr[rows, sl], i_ref[rows, sl])
            h_scr[rows, sl] = o + inter[h]
            st_scr[h] = st_scr[h] * decay[h] + upd[h]
        return carry

    nchunk = tc // chunk
    if nchunk == 1:
        chunk_body(0, 0)
    else:
        lax.fori_loop(0, nchunk, chunk_body, 0)

    for h in range(HG_HEADS):
        sl = slice(h * d, (h + 1) * d)
        hh = h_scr[:, sl]
        ms = jnp.mean(hh * hh, axis=1, keepdims=True)
        y = hh * lax.rsqrt(ms + 1e-5) * norm_w[:, sl] * _silu(z_ref[:, sl])
        o_ref[:, sl] = y.astype(BF16)

    @pl.when(t == nt - 1)
    def _():
        for h in range(HG_HEADS):
            s_ref[0, h] = st_scr[h].T


ANY_SPEC = pl.BlockSpec(memory_space=pl.ANY)


def _state_in_spec(state, layer, tail):
    zeros = (0,) * len(tail)
    if state.ndim == len(tail) + 2:
        return pl.BlockSpec((None, 1) + tail, lambda b, t: (layer, b) + zeros)
    return pl.BlockSpec((1,) + tail, lambda b, t: (b,) + zeros)


def _state_out_spec(layer, tail):
    zeros = (0,) * len(tail)
    return pl.BlockSpec((None, 1) + tail, lambda b, t: (layer, b) + zeros)


def _mix_out_spec(tc, nt, rb0, group):
    return pl.BlockSpec((tc, GROUP_W), lambda b, t: (rb0 + b * nt + t, group))


def _sds(x):
    return jax.ShapeDtypeStruct(x.shape, x.dtype)


def _hgrn(p, par, state, mix, acc, layer, row0, batch, seq, tc, chunk):
    nt = seq // tc
    rb0 = row0 // tc
    pspec = lambda cb: pl.BlockSpec((tc, GROUP_W), lambda b, t: (rb0 + b * nt + t, cb))
    tail = (HG_HEADS, HG_D, HG_D)
    big = lambda dt: pltpu.VMEM((tc, GROUP_W), dt)
    return pl.pallas_call(
        functools.partial(_hgrn_kernel, chunk=chunk, tc=tc, nt=nt),
        grid=(batch, nt),
        in_specs=[pspec(C_HQ // GROUP_W), pspec(C_HF // GROUP_W), pspec(C_HI // GROUP_W), pspec(C_HZ // GROUP_W),
                  pl.BlockSpec((None, 8, GROUP_W), lambda b, t: (layer, 0, 0)),
                  _state_in_spec(state, layer, tail), ANY_SPEC, ANY_SPEC],
        out_specs=[_mix_out_spec(tc, nt, rb0, 0), _state_out_spec(layer, tail)],
        out_shape=[_sds(mix), _sds(acc)],
        input_output_aliases={6: 0, 7: 1},
        scratch_shapes=[pltpu.VMEM((HG_HEADS, HG_D, HG_D), F32),
                        big(F32), big(F32), big(F32), big(BF16), big(BF16), big(F32)],
        compiler_params=_cparams(("parallel", "arbitrary")),
        name="hgrn2",
    )(p, p, p, p, par, state, mix, acc)


def _mlstm_kernel(qk_ref, v_ref, z_ref, g_ref, par_ref, bias_ref, c0_ref, n0_ref, m0_ref,
                  _mix_ref, _acc_c_ref, _acc_n_ref, _acc_m_ref,
                  o_ref, c_ref, n_ref, m_ref, m_scr, h_scr, *, chunk, tc, nt):
    t = pl.program_id(1)
    L = chunk

    @pl.when(t == 0)
    def _():
        c_ref[...] = c0_ref[...]
        n_ref[...] = n0_ref[...]
        m_scr[...] = jnp.zeros(m_scr.shape, F32)
        for h in range(ML_HEADS):
            m_scr[h:h + 1, :] = jnp.broadcast_to(m0_ref[0, :, h:h + 1], (1, LANE))

    r = lax.broadcasted_iota(jnp.int32, (L, L), 0)
    c = lax.broadcasted_iota(jnp.int32, (L, L), 1)
    tril = r >= c
    triu = r <= c

    heads = range(ML_HEADS)
    nchunk = tc // L
    pre = {}
    for ci in range(nchunk):
        rows = slice(ci * L, (ci + 1) * L)
        g = g_ref[rows, :]
        gt = g.T
        for h in heads:
            ib = bias_ref[:, h:h + 1]
            fb = bias_ref[:, ML_HEADS + h:ML_HEADS + h + 1]
            i_col = g[:, h:h + 1] + ib
            i_row = gt[h:h + 1, 0:L] + ib
            lf_col = _log_sigmoid(g[:, ML_HEADS + h:ML_HEADS + h + 1] + fb)
            lf_row = _log_sigmoid(gt[ML_HEADS + h:ML_HEADS + h + 1, 0:L] + fb)
            b_col = jnp.sum(jnp.where(tril, lf_row, 0.0), axis=1, keepdims=True)
            b_row = jnp.sum(jnp.where(triu, lf_col, 0.0), axis=0, keepdims=True)
            dmat = jnp.where(tril, b_col - b_row + i_row, -jnp.inf)
            q = qk_ref[rows, h * ML_DK:(h + 1) * ML_DK]
            k = qk_ref[rows, ML_HEADS * ML_DK + h * ML_DK:ML_HEADS * ML_DK + (h + 1) * ML_DK] * (ML_DK ** -0.5)
            qb = q.astype(BF16)
            pre[ci, h] = dict(
                i_col=i_col, b_col=b_col, dmat=dmat, dmax=jnp.max(dmat, axis=1, keepdims=True),
                q=q, k=k, qb=qb, vb=v_ref[rows, h * ML_DV:(h + 1) * ML_DV].astype(BF16),
                s_qk=lax.dot_general(qb, k.astype(BF16), NT_DIMS, preferred_element_type=F32))

    for ci in range(nchunk):
        rows = slice(ci * L, (ci + 1) * L)
        mid = []
        for h in heads:
            e = pre[ci, h]
            m = m_scr[h:h + 1, 0:1]
            m_t = jnp.maximum(e["b_col"] + m, e["dmax"])
            pm = jnp.exp(e["dmat"] - m_t) * e["s_qk"]
            cst = c_ref[0, h]
            mid.append(dict(m=m, m_t=m_t, pm=pm, cst=cst,
                            pv=jnp.dot(pm.astype(BF16), e["vb"], preferred_element_type=F32),
                            qc=jnp.dot(e["qb"], cst.astype(BF16), preferred_element_type=F32)))
        for h in heads:
            e, u = pre[ci, h], mid[h]
            m, m_t, b_col = u["m"], u["m_t"], e["b_col"]
            nst = n_ref[0, h:h + 1, :]
            inter = jnp.exp(b_col + m - m_t)
            num = u["pv"] + inter * u["qc"]
            den = jnp.sum(u["pm"], axis=1, keepdims=True) + inter * jnp.sum(e["q"] * nst, axis=1, keepdims=True)
            hh = num / jnp.maximum(jnp.abs(den), jnp.exp(-m_t))
            m_new = m_t[L - 1:L, :]
            b_last = b_col[L - 1:L, :]
            wgt = jnp.exp(b_last - b_col + e["i_col"] - m_new)
            decay = jnp.exp(b_last + m - m_new)
            kw = e["k"] * wgt
            c_ref[0, h] = decay * u["cst"] + lax.dot_general(kw.astype(BF16), e["vb"], TN_DIMS,
                                                             preferred_element_type=F32)
            n_ref[0, h:h + 1, :] = decay * nst + jnp.sum(kw, axis=0, keepdims=True)
            m_scr[h:h + 1, :] = jnp.broadcast_to(m_new, (1, LANE))
            h_scr[rows, h * ML_DV:(h + 1) * ML_DV] = hh

    for h in range(ML_HEADS):
        sl = slice(h * ML_DV, (h + 1) * ML_DV)
        x = h_scr[:, sl]
        xc = x - jnp.mean(x, axis=1, keepdims=True)
        y = xc * lax.rsqrt(jnp.mean(xc * xc, axis=1, keepdims=True) + 1e-6)
        o_ref[:, sl] = (y * par_ref[:, sl] * _silu(z_ref[:, sl])).astype(BF16)

    @pl.when(t == nt - 1)
    def _():
        m_ref[0] = m_scr[...]


def _mlstm(p, ps, norm_w, bias, c0, n0, m0, mix, acc_c, acc_n, acc_m, layer, row0, batch, seq, tc, chunk):
    nt = seq // tc
    rb0 = row0 // tc
    pspec = lambda cb: pl.BlockSpec((tc, GROUP_W), lambda b, t: (rb0 + b * nt + t, cb))
    c_tail, n_tail, m_tail = (ML_HEADS, ML_DK, ML_DV), (ML_HEADS, ML_DK), (1, ML_HEADS)
    return pl.pallas_call(
        functools.partial(_mlstm_kernel, chunk=chunk, tc=tc, nt=nt),
        grid=(batch, nt),
        in_specs=[pspec(C_MQK // GROUP_W), pspec(C_MV // GROUP_W), pspec(C_MZ // GROUP_W),
                  pl.BlockSpec((tc, LANE), lambda b, t: (rb0 + b * nt + t, CB_MG)),
                  pl.BlockSpec((None, 1, GROUP_W), lambda b, t: (layer, 0, 0)),
                  pl.BlockSpec((None, 1, 2 * ML_HEADS), lambda b, t: (layer, 0, 0)),
                  _state_in_spec(c0, layer, c_tail), _state_in_spec(n0, layer, n_tail),
                  _state_in_spec(m0, layer, m_tail), ANY_SPEC, ANY_SPEC, ANY_SPEC, ANY_SPEC],
        out_specs=[_mix_out_spec(tc, nt, rb0, 1), _state_out_spec(layer, c_tail), _state_out_spec(layer, n_tail),
                   _state_out_spec(layer, (8, LANE))],
        out_shape=[_sds(mix), _sds(acc_c), _sds(acc_n), _sds(acc_m)],
        input_output_aliases={9: 0, 10: 1, 11: 2, 12: 3},
        scratch_shapes=[pltpu.VMEM((8, LANE), F32), pltpu.VMEM((tc, GROUP_W), F32)],
        compiler_params=_cparams(("parallel", "arbitrary")),
        name="mlstm",
    )(p, p, p, ps, norm_w, bias, c0, n0, m0, mix, acc_c, acc_n, acc_m)


RW_CHUNK = 16


def _rwkv_kernel(r_ref, k_ref, v_ref, z_ref, x_ref, buf_ref, bufx_ref, mu_ref, mux_ref, par_ref,
                 w2_ref, a2_ref, s0_ref, _mix_ref, _acc_ref, o_ref, s_ref,
                 sp_scr, prev_scr, prevx_scr, kh_scr, rh_scr, ki_scr, ai_scr, kd_scr, ad_scr, vb_scr,
                 gl_scr, y_scr, bonus_scr, *, tc, tp, nt):
    t = pl.program_id(1)
    L = RW_CHUNK
    W = GROUP_W

    lane2 = lax.broadcasted_iota(jnp.int32, (LANE, LANE), 1)
    row2 = lax.broadcasted_iota(jnp.int32, (LANE, LANE), 0)
    diag_blocks = (lane2 < RW_HEAD) == (row2 < RW_HEAD)
    seg_ones = jnp.where(diag_blocks, 1.0, 0.0).astype(BF16)

    @pl.when(t == 0)
    def _():
        prev_scr[...] = buf_ref[0]
        prevx_scr[...] = bufx_ref[0]
        zero = jnp.zeros((RW_HEAD, RW_HEAD), F32)
        for pr in range(RW_PAIRS):
            top = jnp.concatenate([s0_ref[0, 2 * pr], zero], axis=1)
            bot = jnp.concatenate([zero, s0_ref[0, 2 * pr + 1]], axis=1)
            sp_scr[pr] = jnp.concatenate([top, bot], axis=0)

    def seg_sum(x):
        parts = [_dot_sel_right(x[:, i * LANE:(i + 1) * LANE], seg_ones, 2) for i in range(x.shape[1] // LANE)]
        return jnp.concatenate(parts, axis=1)

    def shifted(cur, prev_row):
        if tc == 1:
            return prev_row
        rid = lax.broadcasted_iota(jnp.int32, cur.shape, 0)
        return jnp.where(rid == 0, prev_row, pltpu.roll(cur, 1, axis=0))

    def mix(cur, prev_row, mu):
        return cur + (shifted(cur, prev_row) - cur) * mu

    pr_ = r_ref[...]
    pk_ = k_ref[...]
    pv_ = v_ref[...]
    px_ = x_ref[...]
    xr = mix(pr_, prev_scr[:, 0:W], mu_ref[:, 0:W])
    xk = mix(pk_, prev_scr[:, W:2 * W], mu_ref[:, W:2 * W])
    xv = mix(pv_, prev_scr[:, 2 * W:3 * W], mu_ref[:, 2 * W:3 * W])
    xx = mix(px_, prevx_scr[...], mux_ref[...])
    prev_scr[:, 0:W] = pr_[tc - 1:tc, :]
    prev_scr[:, W:2 * W] = pk_[tc - 1:tc, :]
    prev_scr[:, 2 * W:3 * W] = pv_[tc - 1:tc, :]
    prevx_scr[...] = px_[tc - 1:tc, :]

    w0 = par_ref[0:1, :]
    a0 = par_ref[1:2, :]
    k_k = par_ref[2:3, :]
    k_a = par_ref[3:4, :]
    r_k = par_ref[4:5, :]
    ln_w = par_ref[5:6, :]
    ln_b = par_ref[6:7, :]

    wlin = w0 + jnp.dot(jnp.tanh(xx).astype(BF16), w2_ref[...], preferred_element_type=F32)
    wdec = -_softplus(-wlin) - 0.5
    logd = -jnp.exp(wdec)
    a = _sigmoid(a0 + jnp.dot(xx.astype(BF16), a2_ref[...], preferred_element_type=F32))
    kk = xk * k_k
    kk = kk / jnp.maximum(jnp.sqrt(seg_sum(kk * kk)), 1e-12)
    kp = xk * (1.0 + (a - 1.0) * k_a)
    alpha = a * kk
    bonus = seg_sum(xr * kp * r_k) * xv

    if tp > tc:
        pad = lambda u: jnp.concatenate([u, jnp.zeros((tp - tc, u.shape[1]), F32)], axis=0)
        logd, kk, kp, alpha, xr, xv = pad(logd), pad(kk), pad(kp), pad(alpha), pad(xr), pad(xv)
        bonus_scr[...] = pad(bonus)
    else:
        bonus_scr[...] = bonus

    tri, ones = _chunk_masks(tp, L)
    g = _dot_sel_left(tri, logd, 3)
    gl = _dot_sel_left(ones, logd, 3)
    einv = jnp.exp(-g)
    egl = jnp.exp(gl - g)
    kh_scr[...] = (kk * jnp.exp(g - logd)).astype(BF16)
    rh_scr[...] = (xr * jnp.exp(g)).astype(BF16)
    ki_scr[...] = (kp * einv).astype(BF16)
    ai_scr[...] = (alpha * einv).astype(BF16)
    kd_scr[...] = (kp * egl).astype(BF16)
    ad_scr[...] = (alpha * egl).astype(BF16)
    vb_scr[...] = xv.astype(BF16)
    gl_scr[...] = jnp.exp(gl)

    lane_l = lax.broadcasted_iota(jnp.int32, (L, LANE), 1)
    head_a = lane_l < RW_HEAD
    rl = lax.broadcasted_iota(jnp.int32, (L, L), 0)
    cl = lax.broadcasted_iota(jnp.int32, (L, L), 1)
    strict = rl > cl
    incl = rl >= cl
    zb = jnp.zeros((L, LANE), BF16)

    def chunk_body(c, carry):
        r0 = c * L if isinstance(c, int) else pl.multiple_of(c * L, L)
        rows = pl.ds(r0, L)
        pairs = range(RW_PAIRS)
        lanes = [slice(pr * LANE, (pr + 1) * LANE) for pr in pairs]
        vb = [vb_scr[rows, sl] for sl in lanes]
        gm, ks = [], []
        for pr in pairs:
            sl = lanes[pr]
            kh = kh_scr[rows, sl]
            rh = rh_scr[rows, sl]
            x4 = jnp.concatenate([jnp.where(head_a, kh, zb), jnp.where(head_a, zb, kh),
                                  jnp.where(head_a, rh, zb), jnp.where(head_a, zb, rh)], axis=0)
            y2 = jnp.concatenate([ai_scr[rows, sl], ki_scr[rows, sl]], axis=0)
            gm.append(lax.dot_general(x4, y2, NT_DIMS, preferred_element_type=F32))
            ks.append(lax.dot_general(jnp.concatenate([kh, rh], axis=0), sp_scr[pr].astype(BF16), NT_DIMS,
                                      preferred_element_type=F32))
        mv = []
        for pr in pairs:
            g = gm[pr]
            m_ab = jnp.concatenate([jnp.where(strict, g[0:L, L:2 * L], 0.0),
                                    jnp.where(strict, g[L:2 * L, L:2 * L], 0.0)], axis=0)
            mv.append(jnp.dot(m_ab.astype(BF16), vb[pr], preferred_element_type=F32))
        vw = []
        for pr in pairs:
            g = gm[pr]
            n_a = jnp.where(strict, g[0:L, 0:L], 0.0)
            n_b = jnp.where(strict, g[L:2 * L, 0:L], 0.0)
            w = ks[pr][0:L] + jnp.where(head_a, mv[pr][0:L], mv[pr][L:2 * L])
            for s in range(L - 1):
                coef = jnp.where(head_a, n_a[:, s:s + 1], n_b[:, s:s + 1])
                w = w - coef * w[s:s + 1, :]
            vw.append(jnp.concatenate([vb[pr], w.astype(BF16)], axis=0))
        yy, upd = [], []
        for pr in pairs:
            g = gm[pr]
            sl = lanes[pr]
            cm = jnp.concatenate([
                jnp.concatenate([jnp.where(incl, g[2 * L:3 * L, L:2 * L], 0.0),
                                 -jnp.where(incl, g[2 * L:3 * L, 0:L], 0.0)], axis=1),
                jnp.concatenate([jnp.where(incl, g[3 * L:4 * L, L:2 * L], 0.0),
                                 -jnp.where(incl, g[3 * L:4 * L, 0:L], 0.0)], axis=1)], axis=0)
            yy.append(jnp.dot(cm.astype(BF16), vw[pr], preferred_element_type=F32))
            kd_ad = jnp.concatenate([kd_scr[rows, sl], -ad_scr[rows, sl]], axis=0)
            upd.append(lax.dot_general(vw[pr], kd_ad, TN_DIMS, preferred_element_type=F32))
        for pr in pairs:
            sl = lanes[pr]
            y_scr[rows, sl] = ks[pr][L:2 * L] + jnp.where(head_a, yy[pr][0:L], yy[pr][L:2 * L])
            sp_scr[pr] = sp_scr[pr] * gl_scr[pl.ds(r0, 1), sl] + jnp.where(diag_blocks, upd[pr], 0.0)
        return carry

    nchunk = tp // L
    if nchunk == 1:
        chunk_body(0, 0)
    else:
        lax.fori_loop(0, nchunk, chunk_body, 0)

    y = y_scr[0:tc, :]
    yc = y - seg_sum(y) * (1.0 / RW_HEAD)
    yn = yc * lax.rsqrt(seg_sum(yc * yc) * (1.0 / RW_HEAD) + RW_GN_EPS)
    out = (yn * ln_w + ln_b + bonus_scr[0:tc, :]) * _silu(z_ref[...])
    o_ref[...] = out.astype(BF16)

    @pl.when(t == nt - 1)
    def _():
        for pr in range(RW_PAIRS):
            sp = sp_scr[pr]
            s_ref[0, 2 * pr] = sp[0:RW_HEAD, 0:RW_HEAD]
            s_ref[0, 2 * pr + 1] = sp[RW_HEAD:LANE, RW_HEAD:LANE]


def _rwkv(p, ps, buf, bufx, mu, mux, par, w2p, a2p, state, mix, acc, layer, row0, batch, seq, tc):
    nt = seq // tc
    rb0 = row0 // tc
    tp = max(tc, RW_CHUNK)
    pspec = lambda cb: pl.BlockSpec((tc, GROUP_W), lambda b, t: (rb0 + b * nt + t, cb))
    tail = (RW_HEADS, RW_HEAD, RW_HEAD)
    lay = lambda shape: pl.BlockSpec((None,) + shape, lambda b, t: (layer,) + (0,) * len(shape))
    big = lambda dt: pltpu.VMEM((tp, GROUP_W), dt)
    return pl.pallas_call(
        functools.partial(_rwkv_kernel, tc=tc, tp=tp, nt=nt),
        grid=(batch, nt),
        in_specs=[pspec(C_RR // GROUP_W), pspec(C_RK // GROUP_W), pspec(C_RV // GROUP_W), pspec(C_RZ // GROUP_W),
                  pl.BlockSpec((tc, LANE), lambda b, t: (rb0 + b * nt + t, CB_RX)),
                  pl.BlockSpec((1, 1, 3 * GROUP_W), lambda b, t: (b, 0, 0)),
                  pl.BlockSpec((1, 1, LANE), lambda b, t: (b, 0, 0)),
                  lay((1, 3 * GROUP_W)), lay((1, LANE)), lay((8, GROUP_W)),
                  lay((LANE, GROUP_W)), lay((LANE, GROUP_W)), _state_in_spec(state, layer, tail),
                  ANY_SPEC, ANY_SPEC],
        out_specs=[_mix_out_spec(tc, nt, rb0, 2), _state_out_spec(layer, tail)],
        out_shape=[_sds(mix), _sds(acc)],
        input_output_aliases={13: 0, 14: 1},
        scratch_shapes=[pltpu.VMEM((RW_PAIRS, LANE, LANE), F32),
                        pltpu.VMEM((1, 3 * GROUP_W), F32), pltpu.VMEM((1, LANE), F32),
                        big(BF16), big(BF16), big(BF16), big(BF16), big(BF16), big(BF16), big(BF16),
                        big(F32), big(F32), big(F32)],
        compiler_params=_cparams(("parallel", "arbitrary")),
        name="rwkv7",
    )(p, p, p, p, ps, buf, bufx, mu, mux, par, w2p, a2p, state, mix, acc)


def _trunk_layer(l, x_f32, x_bf, w_in_t, w_out_b, hg_par, ml_nw, ml_bias, rw_mu, rw_mux, rw_par, rw_w2, rw_a2,
                 ln_g, ln_b, groups, cfg):
    p = _proj_matmul(x_bf, w_in_t, l, cfg["mm_tm"], cfg["mm_tn"])
    ps = _proj_small(x_bf, w_in_t, l, cfg["mm_tm"])
    mix = jnp.zeros(x_bf.shape, BF16)
    for grp in groups:
        row0, batch, seq = grp["row0"], grp["batch"], grp["seq"]
        mix, grp["acc_hg"] = _hgrn(p, hg_par, grp["hg"], mix, grp["acc_hg"], l, row0, batch, seq,
                                   grp["hg_tc"], grp["hg_chunk"])
        mix, grp["acc_c"], grp["acc_n"], grp["acc_m"] = _mlstm(
            p, ps, ml_nw, ml_bias, grp["ml_c"], grp["ml_n"], grp["ml_m"], mix, grp["acc_c"], grp["acc_n"],
            grp["acc_m"], l, row0, batch, seq, grp["ml_tc"], grp["ml_chunk"])
        mix, grp["acc_rw"] = _rwkv(p, ps, grp["rw_buf"][l], grp["rw_bufx"][l], rw_mu, rw_mux, rw_par, rw_w2, rw_a2,
                                   grp["rw"], mix, grp["acc_rw"], l, row0, batch, seq, grp["rw_tc"])
        if grp["cache_layout"]:
            mix = _mem_attn_cache(p, grp["mem_k"], grp["mem_v"], mix, l, row0, batch, seq)
        else:
            mix = _mem_attn(p, grp["mem_k"], grp["mem_v"], mix, l, row0, batch, seq, grp["xa_tc"])
        last_rkv = lax.slice(p, (row0 + seq - 1, C_RR), (row0 + batch * seq, C_RR + 3 * GROUP_W), (seq, 1))
        last_x = lax.slice(ps, (row0 + seq - 1, CB_RX * LANE), (row0 + batch * seq, (CB_RX + 1) * LANE), (seq, 1))
        grp["buf_new"].append(jnp.concatenate([last_rkv, last_x], axis=-1))
    x_f32, x_bf = _outproj_ln(mix, w_out_b, l, x_f32, ln_g, ln_b, cfg["op_tm"], cfg["op_tn"])
    return x_f32, x_bf


def kernel(x_prompt, x_sample, mem_prompt, state_hgrn, state_mlstm_C, state_mlstm_n, state_mlstm_m, state_rwkv, state_rwkv_shift, cache_mem_k, cache_mem_v, w_in, hgrn_lb, hgrn_norm_w, mlstm_ig_b, mlstm_fg_b, mlstm_norm_w, rwkv_mu, rwkv_w0, rwkv_w2, rwkv_a0, rwkv_a2, rwkv_k_k, rwkv_k_a, rwkv_r_k, rwkv_ln_w, rwkv_ln_b, mem_wk, mem_wv, w_out, ln_g, ln_b):
    bp, tp_, _ = x_prompt.shape
    bs, ts, _ = x_sample.shape
    depth = w_in.shape[0]
    mp, ms = bp * tp_, bs * ts

    w_in_t = jnp.swapaxes(w_in, 1, 2)
    w_out_b = w_out.astype(BF16)
    lb_all = jnp.cumsum(jax.nn.softmax(hgrn_lb.astype(F32), axis=0), axis=0)
    lb_all = lb_all - lb_all[0]
    zrow = jnp.zeros_like(lb_all)
    hg_par = jnp.stack([jnp.log(lb_all), jnp.log1p(-lb_all), 1.0 - lb_all, hgrn_norm_w.astype(F32),
                        zrow, zrow, zrow, zrow], axis=1)
    ml_nw = mlstm_norm_w.astype(F32)[:, None, :]
    ml_bias = jnp.concatenate([mlstm_ig_b, mlstm_fg_b], axis=-1).astype(F32)[:, None, :]
    rw_mu = rwkv_mu[:, None, :3 * GROUP_W].astype(F32)
    rw_mux = rwkv_mu[:, None, 3 * GROUP_W:].astype(F32)
    zr = jnp.zeros((depth, GROUP_W), F32)
    rw_par = jnp.stack([rwkv_w0, rwkv_a0, rwkv_k_k, rwkv_k_a, rwkv_r_k.reshape(depth, GROUP_W), rwkv_ln_w,
                        rwkv_ln_b, zr], axis=1).astype(F32)
    zl = jnp.zeros((depth, RW_LORA, GROUP_W), F32)
    rw_w2 = jnp.concatenate([rwkv_w2.astype(F32), zl], axis=1).astype(BF16)
    rw_a2 = jnp.concatenate([zl, rwkv_a2.astype(F32)], axis=1).astype(BF16)
    ln_g3 = ln_g.astype(F32)[:, None, :]
    ln_b3 = ln_b.astype(F32)[:, None, :]

    mem_x = mem_prompt.reshape(bp * N_MEM, D_MODEL).astype(BF16)
    mk_p = _matmul_layers(mem_x, mem_wk, 256)
    mv_p = _matmul_layers(mem_x, mem_wv, 256)
    mk_out = mk_p.reshape(depth, bp, N_MEM, XA_HEADS, XA_DH)
    mv_out = mv_p.reshape(depth, bp, N_MEM, XA_HEADS, XA_DH)

    def split_buf(buf):
        return buf[:, :, None, :3 * GROUP_W].astype(F32), buf[:, :, None, 3 * GROUP_W:].astype(F32)

    def cache_view(c):
        c = c.reshape(depth, bs, N_MEM, XA_HEADS, XA_SUB, LANE)
        return jnp.transpose(c, (0, 1, 2, 4, 3, 5)).reshape(depth, bs * XA_ROWS, LANE)

    def results(b):
        return dict(acc_hg=jnp.zeros((depth, b, HG_HEADS, HG_D, HG_D), F32),
                    acc_c=jnp.zeros((depth, b, ML_HEADS, ML_DK, ML_DV), F32),
                    acc_n=jnp.zeros((depth, b, ML_HEADS, ML_DK), F32),
                    acc_m=jnp.zeros((depth, b, 8, LANE), F32),
                    acc_rw=jnp.zeros((depth, b, RW_HEADS, RW_HEAD, RW_HEAD), F32), buf_new=[])

    zbuf, zbufx = split_buf(jnp.zeros((depth, bp, RW_SHIFT_W), F32))
    sbuf, sbufx = split_buf(state_rwkv_shift)
    prompt = dict(row0=0, batch=bp, seq=tp_,
                  hg=jnp.zeros((bp, HG_HEADS, HG_D, HG_D), F32), hg_tc=256, hg_chunk=16,
                  ml_c=jnp.zeros((bp, ML_HEADS, ML_DK, ML_DV), F32), ml_n=jnp.zeros((bp, ML_HEADS, ML_DK), F32),
                  ml_m=jnp.zeros((bp, 1, ML_HEADS), F32), ml_tc=256, ml_chunk=64,
                  rw=jnp.zeros((bp, RW_HEADS, RW_HEAD, RW_HEAD), F32), rw_buf=zbuf, rw_bufx=zbufx, rw_tc=128,
                  mem_k=mk_p, mem_v=mv_p, xa_tc=512, cache_layout=False, **results(bp))
    sample = dict(row0=mp, batch=bs, seq=ts,
                  hg=state_hgrn, hg_tc=ts, hg_chunk=ts,
                  ml_c=state_mlstm_C, ml_n=state_mlstm_n, ml_m=state_mlstm_m[:, :, None, :], ml_tc=ts, ml_chunk=ts,
                  rw=state_rwkv, rw_buf=sbuf, rw_bufx=sbufx, rw_tc=ts,
                  mem_k=cache_view(cache_mem_k), mem_v=cache_view(cache_mem_v), cache_layout=True, **results(bs))
    cfg = dict(mm_tm=1024, mm_tn=256, op_tm=384, op_tn=512)

    x_f32 = jnp.concatenate([x_prompt.reshape(mp, D_MODEL), x_sample.reshape(ms, D_MODEL)], axis=0).astype(F32)
    x_bf = x_f32.astype(BF16)
    for l in range(depth):
        x_f32, x_bf = _trunk_layer(l, x_f32, x_bf, w_in_t, w_out_b, hg_par, ml_nw, ml_bias, rw_mu, rw_mux,
                                   rw_par, rw_w2, rw_a2, ln_g3, ln_b3, [prompt, sample], cfg)

    def states(g):
        return (g["acc_hg"], g["acc_c"], g["acc_n"], g["acc_m"][:, :, :ML_HEADS, 0], g["acc_rw"],
                jnp.stack(g["buf_new"], axis=0))

    y_prompt = x_f32[:mp].reshape(bp, tp_, D_MODEL)
    y_sample = x_f32[mp:].reshape(bs, ts, D_MODEL)
    return (y_prompt, y_sample) + states(prompt) + (mk_out, mv_out) + states(sample)
```

```python
import functools
import math

import jax
import jax.numpy as jnp
from jax import lax
from jax.experimental import pallas as pl
from jax.experimental.pallas import tpu as pltpu

F32 = jnp.float32
BF16 = jnp.bfloat16
HI = lax.Precision.HIGHEST

D_MODEL = 4096
DEPTH = 4
GROUP_W = D_MODEL // 4
N_MEM = 256
HG_HEADS, HG_D = 8, 128
ML_HEADS, ML_DK, ML_DV = 4, 128, 256
RW_HEADS, RW_HEAD, RW_LORA = 16, 64, 64
RW_PAIRS = RW_HEADS // 2
XA_HEADS, XA_DH = 4, 256
RW_SHIFT_W = 3 * GROUP_W + 2 * RW_LORA
N_IN = 13448
DN_ALPHA = (2.0 * DEPTH) ** 0.25
LN_EPS = 1e-5
RW_GN_EPS = 64e-5

C_HQ, C_HF, C_HI, C_HZ = 0, 1024, 2048, 3072
C_MQK, C_MV, C_MZ = 4096, 5120, 6144
C_RR, C_RK, C_RV, C_RZ = 7168, 8192, 9216, 10240
C_XQ, C_XZ = 11264, 12288
NP_MAIN = 13312
ORIG_MG, ORIG_MZ, ORIG_RX, ORIG_RZ = 6144, 6152, 10248, 10376
CB_RX, CB_MG = 0, 1
LANE = 128

VMEM_LIMIT = 60 * 1024 * 1024

NT_DIMS = (((1,), (1,)), ((), ()))
TN_DIMS = (((0,), (0,)), ((), ()))


def _cparams(sem):
    return pltpu.CompilerParams(dimension_semantics=sem, vmem_limit_bytes=VMEM_LIMIT)


def _sigmoid(x):
    return jax.nn.sigmoid(x)


def _silu(x):
    return x * _sigmoid(x)


def _log_sigmoid(x):
    return jnp.minimum(x, 0.0) - jnp.log1p(jnp.exp(-jnp.abs(x)))


def _softplus(x):
    return jnp.maximum(x, 0.0) + jnp.log1p(jnp.exp(-jnp.abs(x)))


def _chunk_masks(n, chunk):
    r = lax.broadcasted_iota(jnp.int32, (n, n), 0)
    c = lax.broadcasted_iota(jnp.int32, (n, n), 1)
    sh = int(math.log2(chunk))
    same = lax.shift_right_logical(r, sh) == lax.shift_right_logical(c, sh)
    tri = jnp.where(same & (c <= r), 1.0, 0.0).astype(BF16)
    ones = jnp.where(same, 1.0, 0.0).astype(BF16)
    return tri, ones


def _split_bf16(x, terms):
    parts = []
    for i in range(terms):
        part = x.astype(BF16)
        parts.append(part)
        if i + 1 < terms:
            x = x - part.astype(F32)
    return parts


def _dot_sel_left(sel, x, terms):
    return sum(jnp.dot(sel, part, preferred_element_type=F32) for part in _split_bf16(x, terms))


def _dot_sel_right(x, sel, terms):
    return sum(jnp.dot(part, sel, preferred_element_type=F32) for part in _split_bf16(x, terms))


def _chunk_cumsum(x, chunk):
    rid = lax.broadcasted_iota(jnp.int32, x.shape, 0) & (chunk - 1)
    step = 1
    while step < chunk:
        x = x + jnp.where(rid >= step, pltpu.roll(x, step, axis=0), 0.0)
        step *= 2
    return x


def _mm_kernel(x_ref, w_ref, o_ref):
    o_ref[...] = jnp.dot(x_ref[...], w_ref[...].astype(BF16), preferred_element_type=F32)


def _mm_nt_kernel(x_ref, w_ref, o_ref):
    o_ref[...] = lax.dot_general(x_ref[...], w_ref[0].astype(BF16), NT_DIMS, preferred_element_type=F32)


def _proj_matmul(x, w_t, layer, tm, tn):
    m, k = x.shape

    def w_index(i, j):
        col = j * tn
        off = jnp.where(col >= C_RZ, ORIG_RZ - C_RZ, jnp.where(col >= C_MZ, ORIG_MZ - C_MZ, 0))
        return (layer, pl.multiple_of(col + off, 8), 0)

    return pl.pallas_call(
        _mm_nt_kernel,
        grid=(m // tm, NP_MAIN // tn),
        in_specs=[pl.BlockSpec((tm, k), lambda i, j: (i, 0)),
                  pl.BlockSpec((pl.Element(1), pl.Element(tn), pl.Element(k)), w_index)],
        out_specs=pl.BlockSpec((tm, tn), lambda i, j: (i, j)),
        out_shape=jax.ShapeDtypeStruct((m, NP_MAIN), F32),
        compiler_params=_cparams(("parallel", "arbitrary")),
        name="proj_matmul",
    )(x, w_t)


def _proj_small(x, w_t, layer, tm):
    m, k = x.shape

    def w_index(i, j):
        return (layer, pl.multiple_of(jnp.where(j == 0, ORIG_RX, ORIG_MG), 8), 0)

    return pl.pallas_call(
        _mm_nt_kernel,
        grid=(m // tm, 2),
        in_specs=[pl.BlockSpec((tm, k), lambda i, j: (i, 0)),
                  pl.BlockSpec((pl.Element(1), pl.Element(LANE), pl.Element(k)), w_index)],
        out_specs=pl.BlockSpec((tm, LANE), lambda i, j: (i, j)),
        out_shape=jax.ShapeDtypeStruct((m, 2 * LANE), F32),
        compiler_params=_cparams(("parallel", "arbitrary")),
        name="proj_small",
    )(x, w_t)


def _matmul_layers(x, w, tn):
    m, k = x.shape
    depth, _, n = w.shape
    return pl.pallas_call(
        _mm_kernel,
        grid=(depth, n // tn),
        in_specs=[pl.BlockSpec((m, k), lambda l, j: (0, 0)),
                  pl.BlockSpec((None, k, tn), lambda l, j: (l, 0, j))],
        out_specs=pl.BlockSpec((None, m, tn), lambda l, j: (l, 0, j)),
        out_shape=jax.ShapeDtypeStruct((depth, m, n), F32),
        compiler_params=_cparams(("parallel", "arbitrary")),
        name="mem_kv_matmul",
    )(x, w)


def _outproj_kernel(mix_ref, w_ref, x_ref, g_ref, b_ref, xo_ref, xb_ref, acc_ref, *, tn, nj):
    j = pl.program_id(1)
    acc_ref[j] = jnp.dot(mix_ref[...], w_ref[...], preferred_element_type=F32)

    @pl.when(j == nj - 1)
    def _():
        tm = acc_ref.shape[1]
        tot = jnp.zeros((tm, 1), F32)
        for jj in range(nj):
            y = DN_ALPHA * x_ref[:, jj * tn:(jj + 1) * tn] + acc_ref[jj]
            acc_ref[jj] = y
            tot = tot + jnp.sum(y, axis=1, keepdims=True)
        mean = tot * (1.0 / D_MODEL)
        sq = jnp.zeros((tm, 1), F32)
        for jj in range(nj):
            yc = acc_ref[jj] - mean
            sq = sq + jnp.sum(yc * yc, axis=1, keepdims=True)
        rstd = lax.rsqrt(sq * (1.0 / D_MODEL) + LN_EPS)
        for jj in range(nj):
            sl = slice(jj * tn, (jj + 1) * tn)
            out = (acc_ref[jj] - mean) * rstd * g_ref[:, sl] + b_ref[:, sl]
            xo_ref[:, sl] = out
            xb_ref[:, sl] = out.astype(BF16)


def _outproj_ln(mix, w_out, layer, x, ln_g, ln_b, tm, tn):
    m = x.shape[0]
    nj = D_MODEL // tn
    row_spec = pl.BlockSpec((None, 1, D_MODEL), lambda i, j: (layer, 0, 0))
    blk = pl.BlockSpec((tm, D_MODEL), lambda i, j: (i, 0))
    return pl.pallas_call(
        functools.partial(_outproj_kernel, tn=tn, nj=nj),
        grid=(m // tm, nj),
        in_specs=[blk, pl.BlockSpec((None, D_MODEL, tn), lambda i, j: (layer, 0, j)),
                  blk, row_spec, row_spec],
        out_specs=[blk, blk],
        out_shape=[jax.ShapeDtypeStruct((m, D_MODEL), F32), jax.ShapeDtypeStruct((m, D_MODEL), BF16)],
        scratch_shapes=[pltpu.VMEM((nj, tm, tn), F32)],
        compiler_params=_cparams(("parallel", "arbitrary")),
        name="outproj_ln",
    )(mix, w_out, x, ln_g, ln_b)


def _attn_kernel(q_ref, z_ref, k_ref, v_ref, _mix_ref, o_ref):
    lanes = [slice(h * XA_DH, (h + 1) * XA_DH) for h in range(XA_HEADS)]
    scores = [lax.dot_general(q_ref[:, sl].astype(BF16), k_ref[:, sl].astype(BF16), NT_DIMS,
                              preferred_element_type=F32) * (XA_DH ** -0.5)
              for sl in lanes]
    outs = []
    for sl, s in zip(lanes, scores):
        e = jnp.exp(s - jnp.max(s, axis=1, keepdims=True))
        pr = e / jnp.sum(e, axis=1, keepdims=True)
        outs.append(jnp.dot(pr.astype(BF16), v_ref[:, sl].astype(BF16), preferred_element_type=F32))
    for sl, o in zip(lanes, outs):
        o_ref[:, sl] = (o * _silu(z_ref[:, sl])).astype(BF16)


def _mem_attn(p, mem_k, mem_v, mix, layer, row0, batch, seq, tc):
    nt = seq // tc
    rb0 = row0 // tc
    pspec = lambda cb: pl.BlockSpec((tc, GROUP_W), lambda b, t: (rb0 + b * nt + t, cb))
    kvspec = pl.BlockSpec((None, N_MEM, GROUP_W), lambda b, t: (layer, b, 0))
    return pl.pallas_call(
        _attn_kernel,
        grid=(batch, nt),
        in_specs=[pspec(C_XQ // GROUP_W), pspec(C_XZ // GROUP_W), kvspec, kvspec, ANY_SPEC],
        out_specs=_mix_out_spec(tc, nt, rb0, 3),
        out_shape=_sds(mix),
        input_output_aliases={4: 0},
        compiler_params=_cparams(("parallel", "arbitrary")),
        name="mem_attn",
    )(p, p, mem_k, mem_v, mix)


XA_SUB = XA_DH // LANE
XA_ROWS = N_MEM * XA_SUB * XA_HEADS


def _attn_cache_kernel(q_ref, z_ref, k_ref, v_ref, _mix_ref, o_ref):
    t = q_ref.shape[0]
    tiles = XA_HEADS * XA_SUB
    qx = jnp.concatenate([q_ref[:, c * LANE:(c + 1) * LANE] for c in range(tiles)], axis=0).astype(BF16)
    s_all = lax.dot_general(qx, k_ref[...].astype(BF16), NT_DIMS, preferred_element_type=F32)
    col = lax.broadcasted_iota(jnp.int32, (t, XA_ROWS), 1)
    col_head = col & (XA_HEADS - 1)
    col_sub = lax.shift_right_logical(col, 2) & (XA_SUB - 1)
    probs = []
    for h in range(XA_HEADS):
        mine = col_head == h
        part = jnp.zeros((t, XA_ROWS), F32)
        for s in range(XA_SUB):
            rows = slice((h * XA_SUB + s) * t, (h * XA_SUB + s + 1) * t)
            part = part + jnp.where(mine & (col_sub == s), s_all[rows], 0.0)
        other = jnp.where(col_sub == 0, pltpu.roll(part, XA_ROWS - XA_HEADS, axis=1),
                          pltpu.roll(part, XA_HEADS, axis=1))
        sc = jnp.where(mine, (part + other) * (XA_DH ** -0.5), -jnp.inf)
        e = jnp.exp(sc - jnp.max(sc, axis=1, keepdims=True))
        probs.append(e / (jnp.sum(e, axis=1, keepdims=True) * (1.0 / XA_SUB)))
    pr = jnp.concatenate(probs, axis=0)
    sub_all = lax.shift_right_logical(lax.broadcasted_iota(jnp.int32, pr.shape, 1), 2) & (XA_SUB - 1)
    vb = v_ref[...].astype(BF16)
    outs = [jnp.dot(jnp.where(sub_all == s, pr, 0.0).astype(BF16), vb, preferred_element_type=F32)
            for s in range(XA_SUB)]
    for h in range(XA_HEADS):
        for s in range(XA_SUB):
            sl = slice((h * XA_SUB + s) * LANE, (h * XA_SUB + s + 1) * LANE)
            o_ref[:, sl] = (outs[s][h * t:(h + 1) * t] * _silu(z_ref[:, sl])).astype(BF16)


def _mem_attn_cache(p, cache_k, cache_v, mix, layer, row0, batch, seq):
    assert XA_SUB == 2 and XA_HEADS == 4
    rb0 = row0 // seq
    pspec = lambda cb: pl.BlockSpec((seq, GROUP_W), lambda b, t: (rb0 + b, cb))
    kvspec = pl.BlockSpec((None, XA_ROWS, LANE), lambda b, t: (layer, b, 0))
    return pl.pallas_call(
        _attn_cache_kernel,
        grid=(batch, 1),
        in_specs=[pspec(C_XQ // GROUP_W), pspec(C_XZ // GROUP_W), kvspec, kvspec, ANY_SPEC],
        out_specs=_mix_out_spec(seq, 1, rb0, 3),
        out_shape=_sds(mix),
        input_output_aliases={4: 0},
        compiler_params=_cparams(("parallel", "arbitrary")),
        name="mem_attn_cache",
    )(p, p, cache_k, cache_v, mix)


def _hgrn_kernel(q_ref, f_ref, i_ref, z_ref, par_ref, s0_ref, _mix_ref, _acc_ref, o_ref, s_ref,
                 st_scr, b_scr, qs_scr, kk_scr, qt_scr, vb_scr, h_scr, *, chunk, tc, nt):
    t = pl.program_id(1)
    d = HG_D
    sub = 8

    @pl.when(t == 0)
    def _():
        for h in range(HG_HEADS):
            st_scr[h] = s0_ref[0, h].T

    log_lb = par_ref[0:1, :]
    log1m_lb = par_ref[1:2, :]
    one_m_lb = par_ref[2:3, :]
    norm_w = par_ref[3:4, :]

    fpre = f_ref[...]
    bt = log1m_lb + _log_sigmoid(fpre)
    logf = jnp.maximum(log_lb, bt) + jnp.log1p(jnp.exp(-jnp.abs(log_lb - bt)))
    b = _chunk_cumsum(logf, chunk)
    qs = _silu(q_ref[...])
    b_scr[...] = b
    qs_scr[...] = qs
    kk_scr[...] = one_m_lb * _sigmoid(-fpre)
    qt_scr[...] = (qs * jnp.exp(b)).astype(BF16)
    vb_scr[...] = i_ref[...].astype(BF16)

    rid = lax.broadcasted_iota(jnp.int32, (sub, d), 0)

    def intra(bc, qc, kc, vc):
        blocks = []
        for rb in range(chunk // sub):
            rs = slice(rb * sub, (rb + 1) * sub)
            bb, qb = bc[rs], qc[rs]
            o = jnp.zeros((sub, d), F32)
            for s in range((rb + 1) * sub):
                diff = bb - bc[s:s + 1, :]
                if s >= rb * sub:
                    diff = jnp.where(rid >= s - rb * sub, diff, -jnp.inf)
                a = jnp.sum(qb * kc[s:s + 1, :] * jnp.exp(diff), axis=1, keepdims=True)
                o = o + a * vc[s:s + 1, :]
            blocks.append(o)
        return blocks[0] if len(blocks) == 1 else jnp.concatenate(blocks, axis=0)

    def chunk_body(c, carry):
        r0 = c * chunk if isinstance(c, int) else pl.multiple_of(c * chunk, chunk)
        rows = pl.ds(r0, chunk)
        lanes = [slice(h * d, (h + 1) * d) for h in range(HG_HEADS)]
        inter, upd, decay = [], [], []
        for h, sl in enumerate(lanes):
            bc = b_scr[rows, sl]
            b_last = bc[chunk - 1:chunk, :]
            kt = (kk_scr[rows, sl] * jnp.exp(b_last - bc)).astype(BF16)
            inter.append(lax.dot_general(qt_scr[rows, sl], st_scr[h].astype(BF16), NT_DIMS,
                                         preferred_element_type=F32))
            upd.append(lax.dot_general(vb_scr[rows, sl], kt, TN_DIMS, preferred_element_type=F32))
            decay.append(jnp.exp(b_last))
        for h, sl in enumerate(lanes):
            o = intra(b_scr[rows, sl], qs_scr[rows, sl], kk_scr[rows, sl], i_ref[rows, sl])
            h_scr[rows, sl] = o + inter[h]
            st_scr[h] = st_scr[h] * decay[h] + upd[h]
        return carry

    nchunk = tc // chunk
    if nchunk == 1:
        chunk_body(0, 0)
    else:
        lax.fori_loop(0, nchunk, chunk_body, 0)

    for h in range(HG_HEADS):
        sl = slice(h * d, (h + 1) * d)
        hh = h_scr[:, sl]
        ms = jnp.mean(hh * hh, axis=1, keepdims=True)
        y = hh * lax.rsqrt(ms + 1e-5) * norm_w[:, sl] * _silu(z_ref[:, sl])
        o_ref[:, sl] = y.astype(BF16)

    @pl.when(t == nt - 1)
    def _():
        for h in range(HG_HEADS):
            s_ref[0, h] = st_scr[h].T


ANY_SPEC = pl.BlockSpec(memory_space=pl.ANY)


def _state_in_spec(state, layer, tail):
    zeros = (0,) * len(tail)
    if state.ndim == len(tail) + 2:
        return pl.BlockSpec((None, 1) + tail, lambda b, t: (layer, b) + zeros)
    return pl.BlockSpec((1,) + tail, lambda b, t: (b,) + zeros)


def _state_out_spec(layer, tail):
    zeros = (0,) * len(tail)
    return pl.BlockSpec((None, 1) + tail, lambda b, t: (layer, b) + zeros)


def _mix_out_spec(tc, nt, rb0, group):
    return pl.BlockSpec((tc, GROUP_W), lambda b, t: (rb0 + b * nt + t, group))


def _sds(x):
    return jax.ShapeDtypeStruct(x.shape, x.dtype)


def _hgrn(p, par, state, mix, acc, layer, row0, batch, seq, tc, chunk):
    nt = seq // tc
    rb0 = row0 // tc
    pspec = lambda cb: pl.BlockSpec((tc, GROUP_W), lambda b, t: (rb0 + b * nt + t, cb))
    tail = (HG_HEADS, HG_D, HG_D)
    big = lambda dt: pltpu.VMEM((tc, GROUP_W), dt)
    return pl.pallas_call(
        functools.partial(_hgrn_kernel, chunk=chunk, tc=tc, nt=nt),
        grid=(batch, nt),
        in_specs=[pspec(C_HQ // GROUP_W), pspec(C_HF // GROUP_W), pspec(C_HI // GROUP_W), pspec(C_HZ // GROUP_W),
                  pl.BlockSpec((None, 8, GROUP_W), lambda b, t: (layer, 0, 0)),
                  _state_in_spec(state, layer, tail), ANY_SPEC, ANY_SPEC],
        out_specs=[_mix_out_spec(tc, nt, rb0, 0), _state_out_spec(layer, tail)],
        out_shape=[_sds(mix), _sds(acc)],
        input_output_aliases={6: 0, 7: 1},
        scratch_shapes=[pltpu.VMEM((HG_HEADS, HG_D, HG_D), F32),
                        big(F32), big(F32), big(F32), big(BF16), big(BF16), big(F32)],
        compiler_params=_cparams(("parallel", "arbitrary")),
        name="hgrn2",
    )(p, p, p, p, par, state, mix, acc)


def _mlstm_kernel(qk_ref, v_ref, z_ref, g_ref, par_ref, bias_ref, c0_ref, n0_ref, m0_ref,
                  _mix_ref, _acc_c_ref, _acc_n_ref, _acc_m_ref,
                  o_ref, c_ref, n_ref, m_ref, m_scr, h_scr, *, chunk, tc, nt):
    t = pl.program_id(1)
    L = chunk

    @pl.when(t == 0)
    def _():
        c_ref[...] = c0_ref[...]
        n_ref[...] = n0_ref[...]
        m_scr[...] = jnp.zeros(m_scr.shape, F32)
        for h in range(ML_HEADS):
            m_scr[h:h + 1, :] = jnp.broadcast_to(m0_ref[0, :, h:h + 1], (1, LANE))

    r = lax.broadcasted_iota(jnp.int32, (L, L), 0)
    c = lax.broadcasted_iota(jnp.int32, (L, L), 1)
    tril = r >= c
    triu = r <= c

    heads = range(ML_HEADS)
    nchunk = tc // L
    pre = {}
    for ci in range(nchunk):
        rows = slice(ci * L, (ci + 1) * L)
        g = g_ref[rows, :]
        gt = g.T
        for h in heads:
            ib = bias_ref[:, h:h + 1]
            fb = bias_ref[:, ML_HEADS + h:ML_HEADS + h + 1]
            i_col = g[:, h:h + 1] + ib
            i_row = gt[h:h + 1, 0:L] + ib
            lf_col = _log_sigmoid(g[:, ML_HEADS + h:ML_HEADS + h + 1] + fb)
            lf_row = _log_sigmoid(gt[ML_HEADS + h:ML_HEADS + h + 1, 0:L] + fb)
            b_col = jnp.sum(jnp.where(tril, lf_row, 0.0), axis=1, keepdims=True)
            b_row = jnp.sum(jnp.where(triu, lf_col, 0.0), axis=0, keepdims=True)
            dmat = jnp.where(tril, b_col - b_row + i_row, -jnp.inf)
            q = qk_ref[rows, h * ML_DK:(h + 1) * ML_DK]
            k = qk_ref[rows, ML_HEADS * ML_DK + h * ML_DK:ML_HEADS * ML_DK + (h + 1) * ML_DK] * (ML_DK ** -0.5)
            qb = q.astype(BF16)
            pre[ci, h] = dict(
                i_col=i_col, b_col=b_col, dmat=dmat, dmax=jnp.max(dmat, axis=1, keepdims=True),
                q=q, k=k, qb=qb, vb=v_ref[rows, h * ML_DV:(h + 1) * ML_DV].astype(BF16),
                s_qk=lax.dot_general(qb, k.astype(BF16), NT_DIMS, preferred_element_type=F32))

    for ci in range(nchunk):
        rows = slice(ci * L, (ci + 1) * L)
        mid = []
        for h in heads:
            e = pre[ci, h]
            m = m_scr[h:h + 1, 0:1]
            m_t = jnp.maximum(e["b_col"] + m, e["dmax"])
            pm = jnp.exp(e["dmat"] - m_t) * e["s_qk"]
            cst = c_ref[0, h]
            mid.append(dict(m=m, m_t=m_t, pm=pm, cst=cst,
                            pv=jnp.dot(pm.astype(BF16), e["vb"], preferred_element_type=F32),
                            qc=jnp.dot(e["qb"], cst.astype(BF16), preferred_element_type=F32)))
        for h in heads:
            e, u = pre[ci, h], mid[h]
            m, m_t, b_col = u["m"], u["m_t"], e["b_col"]
            nst = n_ref[0, h:h + 1, :]
            inter = jnp.exp(b_col + m - m_t)
            num = u["pv"] + inter * u["qc"]
            den = jnp.sum(u["pm"], axis=1, keepdims=True) + inter * jnp.sum(e["q"] * nst, axis=1, keepdims=True)
            hh = num / jnp.maximum(jnp.abs(den), jnp.exp(-m_t))
            m_new = m_t[L - 1:L, :]
            b_last = b_col[L - 1:L, :]
            wgt = jnp.exp(b_last - b_col + e["i_col"] - m_new)
            decay = jnp.exp(b_last + m - m_new)
            kw = e["k"] * wgt
            c_ref[0, h] = decay * u["cst"] + lax.dot_general(kw.astype(BF16), e["vb"], TN_DIMS,
                                                             preferred_element_type=F32)
            n_ref[0, h:h + 1, :] = decay * nst + jnp.sum(kw, axis=0, keepdims=True)
            m_scr[h:h + 1, :] = jnp.broadcast_to(m_new, (1, LANE))
            h_scr[rows, h * ML_DV:(h + 1) * ML_DV] = hh

    for h in range(ML_HEADS):
        sl = slice(h * ML_DV, (h + 1) * ML_DV)
        x = h_scr[:, sl]
        xc = x - jnp.mean(x, axis=1, keepdims=True)
        y = xc * lax.rsqrt(jnp.mean(xc * xc, axis=1, keepdims=True) + 1e-6)
        o_ref[:, sl] = (y * par_ref[:, sl] * _silu(z_ref[:, sl])).astype(BF16)

    @pl.when(t == nt - 1)
    def _():
        m_ref[0] = m_scr[...]


def _mlstm(p, ps, norm_w, bias, c0, n0, m0, mix, acc_c, acc_n, acc_m, layer, row0, batch, seq, tc, chunk):
    nt = seq // tc
    rb0 = row0 // tc
    pspec = lambda cb: pl.BlockSpec((tc, GROUP_W), lambda b, t: (rb0 + b * nt + t, cb))
    c_tail, n_tail, m_tail = (ML_HEADS, ML_DK, ML_DV), (ML_HEADS, ML_DK), (1, ML_HEADS)
    return pl.pallas_call(
        functools.partial(_mlstm_kernel, chunk=chunk, tc=tc, nt=nt),
        grid=(batch, nt),
        in_specs=[pspec(C_MQK // GROUP_W), pspec(C_MV // GROUP_W), pspec(C_MZ // GROUP_W),
                  pl.BlockSpec((tc, LANE), lambda b, t: (rb0 + b * nt + t, CB_MG)),
                  pl.BlockSpec((None, 1, GROUP_W), lambda b, t: (layer, 0, 0)),
                  pl.BlockSpec((None, 1, 2 * ML_HEADS), lambda b, t: (layer, 0, 0)),
                  _state_in_spec(c0, layer, c_tail), _state_in_spec(n0, layer, n_tail),
                  _state_in_spec(m0, layer, m_tail), ANY_SPEC, ANY_SPEC, ANY_SPEC, ANY_SPEC],
        out_specs=[_mix_out_spec(tc, nt, rb0, 1), _state_out_spec(layer, c_tail), _state_out_spec(layer, n_tail),
                   _state_out_spec(layer, (8, LANE))],
        out_shape=[_sds(mix), _sds(acc_c), _sds(acc_n), _sds(acc_m)],
        input_output_aliases={9: 0, 10: 1, 11: 2, 12: 3},
        scratch_shapes=[pltpu.VMEM((8, LANE), F32), pltpu.VMEM((tc, GROUP_W), F32)],
        compiler_params=_cparams(("parallel", "arbitrary")),
        name="mlstm",
    )(p, p, p, ps, norm_w, bias, c0, n0, m0, mix, acc_c, acc_n, acc_m)


RW_CHUNK = 16


def _rwkv_kernel(r_ref, k_ref, v_ref, z_ref, x_ref, buf_ref, bufx_ref, mu_ref, mux_ref, par_ref,
                 w2_ref, a2_ref, s0_ref, _mix_ref, _acc_ref, o_ref, s_ref,
                 sp_scr, prev_scr, prevx_scr, kh_scr, rh_scr, ki_scr, ai_scr, kd_scr, ad_scr, vb_scr,
                 gl_scr, y_scr, bonus_scr, *, tc, tp, nt):
    t = pl.program_id(1)
    L = RW_CHUNK
    W = GROUP_W

    lane2 = lax.broadcasted_iota(jnp.int32, (LANE, LANE), 1)
    row2 = lax.broadcasted_iota(jnp.int32, (LANE, LANE), 0)
    diag_blocks = (lane2 < RW_HEAD) == (row2 < RW_HEAD)
    seg_ones = jnp.where(diag_blocks, 1.0, 0.0).astype(BF16)

    @pl.when(t == 0)
    def _():
        prev_scr[...] = buf_ref[0]
        prevx_scr[...] = bufx_ref[0]
        zero = jnp.zeros((RW_HEAD, RW_HEAD), F32)
        for pr in range(RW_PAIRS):
            top = jnp.concatenate([s0_ref[0, 2 * pr], zero], axis=1)
            bot = jnp.concatenate([zero, s0_ref[0, 2 * pr + 1]], axis=1)
            sp_scr[pr] = jnp.concatenate([top, bot], axis=0)

    def seg_sum(x):
        parts = [_dot_sel_right(x[:, i * LANE:(i + 1) * LANE], seg_ones, 2) for i in range(x.shape[1] // LANE)]
        return jnp.concatenate(parts, axis=1)

    def shifted(cur, prev_row):
        if tc == 1:
            return prev_row
        rid = lax.broadcasted_iota(jnp.int32, cur.shape, 0)
        return jnp.where(rid == 0, prev_row, pltpu.roll(cur, 1, axis=0))

    def mix(cur, prev_row, mu):
        return cur + (shifted(cur, prev_row) - cur) * mu

    pr_ = r_ref[...]
    pk_ = k_ref[...]
    pv_ = v_ref[...]
    px_ = x_ref[...]
    xr = mix(pr_, prev_scr[:, 0:W], mu_ref[:, 0:W])
    xk = mix(pk_, prev_scr[:, W:2 * W], mu_ref[:, W:2 * W])
    xv = mix(pv_, prev_scr[:, 2 * W:3 * W], mu_ref[:, 2 * W:3 * W])
    xx = mix(px_, prevx_scr[...], mux_ref[...])
    prev_scr[:, 0:W] = pr_[tc - 1:tc, :]
    prev_scr[:, W:2 * W] = pk_[tc - 1:tc, :]
    prev_scr[:, 2 * W:3 * W] = pv_[tc - 1:tc, :]
    prevx_scr[...] = px_[tc - 1:tc, :]

    w0 = par_ref[0:1, :]
    a0 = par_ref[1:2, :]
    k_k = par_ref[2:3, :]
    k_a = par_ref[3:4, :]
    r_k = par_ref[4:5, :]
    ln_w = par_ref[5:6, :]
    ln_b = par_ref[6:7, :]

    wlin = w0 + jnp.dot(jnp.tanh(xx).astype(BF16), w2_ref[...], preferred_element_type=F32)
    wdec = -_softplus(-wlin) - 0.5
    logd = -jnp.exp(wdec)
    a = _sigmoid(a0 + jnp.dot(xx.astype(BF16), a2_ref[...], preferred_element_type=F32))
    kk = xk * k_k
    kk = kk / jnp.maximum(jnp.sqrt(seg_sum(kk * kk)), 1e-12)
    kp = xk * (1.0 + (a - 1.0) * k_a)
    alpha = a * kk
    bonus = seg_sum(xr * kp * r_k) * xv

    if tp > tc:
        pad = lambda u: jnp.concatenate([u, jnp.zeros((tp - tc, u.shape[1]), F32)], axis=0)
        logd, kk, kp, alpha, xr, xv = pad(logd), pad(kk), pad(kp), pad(alpha), pad(xr), pad(xv)
        bonus_scr[...] = pad(bonus)
    else:
        bonus_scr[...] = bonus

    tri, ones = _chunk_masks(tp, L)
    g = _dot_sel_left(tri, logd, 3)
    gl = _dot_sel_left(ones, logd, 3)
    einv = jnp.exp(-g)
    egl = jnp.exp(gl - g)
    kh_scr[...] = (kk * jnp.exp(g - logd)).astype(BF16)
    rh_scr[...] = (xr * jnp.exp(g)).astype(BF16)
    ki_scr[...] = (kp * einv).astype(BF16)
    ai_scr[...] = (alpha * einv).astype(BF16)
    kd_scr[...] = (kp * egl).astype(BF16)
    ad_scr[...] = (alpha * egl).astype(BF16)
    vb_scr[...] = xv.astype(BF16)
    gl_scr[...] = jnp.exp(gl)

    lane_l = lax.broadcasted_iota(jnp.int32, (L, LANE), 1)
    head_a = lane_l < RW_HEAD
    rl = lax.broadcasted_iota(jnp.int32, (L, L), 0)
    cl = lax.broadcasted_iota(jnp.int32, (L, L), 1)
    incl = rl >= cl
    rl2 = lax.broadcasted_iota(jnp.int32, (2 * L, L), 0) & (L - 1)
    strict2 = rl2 > lax.broadcasted_iota(jnp.int32, (2 * L, L), 1)
    spread_shape = (L, (L - 1) * LANE)
    spread = jnp.where(lax.broadcasted_iota(jnp.int32, spread_shape, 0)
                       == lax.shift_right_logical(lax.broadcasted_iota(jnp.int32, spread_shape, 1), 7),
                       1.0, 0.0).astype(BF16)
    zb = jnp.zeros((L, LANE), BF16)

    def chunk_body(c, carry):
        r0 = c * L if isinstance(c, int) else pl.multiple_of(c * L, L)
        rows = pl.ds(r0, L)
        pairs = range(RW_PAIRS)
        lanes = [slice(pr * LANE, (pr + 1) * LANE) for pr in pairs]
        vb = [vb_scr[rows, sl] for sl in lanes]
        gm, ks = [], []
        for pr in pairs:
            sl = lanes[pr]
            kh = kh_scr[rows, sl]
            rh = rh_scr[rows, sl]
            x4 = jnp.concatenate([jnp.where(head_a, kh, zb), jnp.where(head_a, zb, kh),
                                  jnp.where(head_a, rh, zb), jnp.where(head_a, zb, rh)], axis=0)
            y2 = jnp.concatenate([ai_scr[rows, sl], ki_scr[rows, sl]], axis=0)
            gm.append(lax.dot_general(x4, y2, NT_DIMS, preferred_element_type=F32))
            ks.append(lax.dot_general(jnp.concatenate([kh, rh], axis=0), sp_scr[pr].astype(BF16), NT_DIMS,
                                      preferred_element_type=F32))
        mv, coef = [], []
        for pr in pairs:
            g = gm[pr]
            m_ab = jnp.where(strict2, g[0:2 * L, L:2 * L], 0.0)
            mv.append(jnp.dot(m_ab.astype(BF16), vb[pr], preferred_element_type=F32))
            coef.append(_dot_sel_right(jnp.where(strict2, g[0:2 * L, 0:L], 0.0), spread, 1))
        vw = []
        for pr in pairs:
            w = ks[pr][0:L] + jnp.where(head_a, mv[pr][0:L], mv[pr][L:2 * L])
            for s in range(L - 1):
                tile = slice(s * LANE, (s + 1) * LANE)
                w = w - jnp.where(head_a, coef[pr][0:L, tile], coef[pr][L:2 * L, tile]) * w[s:s + 1, :]
            vw.append(jnp.concatenate([vb[pr], w.astype(BF16)], axis=0))
        yy, upd = [], []
        for pr in pairs:
            g = gm[pr]
            sl = lanes[pr]
            cm = jnp.concatenate([
                jnp.concatenate([jnp.where(incl, g[2 * L:3 * L, L:2 * L], 0.0),
                                 -jnp.where(incl, g[2 * L:3 * L, 0:L], 0.0)], axis=1),
                jnp.concatenate([jnp.where(incl, g[3 * L:4 * L, L:2 * L], 0.0),
                                 -jnp.where(incl, g[3 * L:4 * L, 0:L], 0.0)], axis=1)], axis=0)
            yy.append(jnp.dot(cm.astype(BF16), vw[pr], preferred_element_type=F32))
        grp = LANE // (2 * L)
        zblk = jnp.zeros((2 * L, LANE), BF16)
        for g0 in range(0, RW_PAIRS, grp):
            members = range(g0, g0 + grp)
            vw_t = jnp.concatenate([vw[pr] for pr in members], axis=0).T
            kd_rows = []
            for q, pr in enumerate(members):
                sl = lanes[pr]
                kd_ad = jnp.concatenate([kd_scr[rows, sl], -ad_scr[rows, sl]], axis=0)
                kd_rows.append(jnp.concatenate([kd_ad if c == q else zblk for c in range(grp)], axis=1))
            u_all = jnp.dot(vw_t, jnp.concatenate(kd_rows, axis=0), preferred_element_type=F32)
            for q in range(grp):
                upd.append(u_all[:, q * LANE:(q + 1) * LANE])
        for pr in pairs:
            sl = lanes[pr]
            y_scr[rows, sl] = ks[pr][L:2 * L] + jnp.where(head_a, yy[pr][0:L], yy[pr][L:2 * L])
            sp_scr[pr] = sp_scr[pr] * gl_scr[pl.ds(r0, 1), sl] + jnp.where(diag_blocks, upd[pr], 0.0)
        return carry

    nchunk = tp // L
    if nchunk == 1:
        chunk_body(0, 0)
    else:
        lax.fori_loop(0, nchunk, chunk_body, 0)

    y = y_scr[0:tc, :]
    yc = y - seg_sum(y) * (1.0 / RW_HEAD)
    yn = yc * lax.rsqrt(seg_sum(yc * yc) * (1.0 / RW_HEAD) + RW_GN_EPS)
    out = (yn * ln_w + ln_b + bonus_scr[0:tc, :]) * _silu(z_ref[...])
    o_ref[...] = out.astype(BF16)

    @pl.when(t == nt - 1)
    def _():
        for pr in range(RW_PAIRS):
            sp = sp_scr[pr]
            s_ref[0, 2 * pr] = sp[0:RW_HEAD, 0:RW_HEAD]
            s_ref[0, 2 * pr + 1] = sp[RW_HEAD:LANE, RW_HEAD:LANE]


def _rwkv(p, ps, buf, bufx, mu, mux, par, w2p, a2p, state, mix, acc, layer, row0, batch, seq, tc):
    nt = seq // tc
    rb0 = row0 // tc
    tp = max(tc, RW_CHUNK)
    pspec = lambda cb: pl.BlockSpec((tc, GROUP_W), lambda b, t: (rb0 + b * nt + t, cb))
    tail = (RW_HEADS, RW_HEAD, RW_HEAD)
    lay = lambda shape: pl.BlockSpec((None,) + shape, lambda b, t: (layer,) + (0,) * len(shape))
    big = lambda dt: pltpu.VMEM((tp, GROUP_W), dt)
    return pl.pallas_call(
        functools.partial(_rwkv_kernel, tc=tc, tp=tp, nt=nt),
        grid=(batch, nt),
        in_specs=[pspec(C_RR // GROUP_W), pspec(C_RK // GROUP_W), pspec(C_RV // GROUP_W), pspec(C_RZ // GROUP_W),
                  pl.BlockSpec((tc, LANE), lambda b, t: (rb0 + b * nt + t, CB_RX)),
                  pl.BlockSpec((1, 1, 3 * GROUP_W), lambda b, t: (b, 0, 0)),
                  pl.BlockSpec((1, 1, LANE), lambda b, t: (b, 0, 0)),
                  lay((1, 3 * GROUP_W)), lay((1, LANE)), lay((8, GROUP_W)),
                  lay((LANE, GROUP_W)), lay((LANE, GROUP_W)), _state_in_spec(state, layer, tail),
                  ANY_SPEC, ANY_SPEC],
        out_specs=[_mix_out_spec(tc, nt, rb0, 2), _state_out_spec(layer, tail)],
        out_shape=[_sds(mix), _sds(acc)],
        input_output_aliases={13: 0, 14: 1},
        scratch_shapes=[pltpu.VMEM((RW_PAIRS, LANE, LANE), F32),
                        pltpu.VMEM((1, 3 * GROUP_W), F32), pltpu.VMEM((1, LANE), F32),
                        big(BF16), big(BF16), big(BF16), big(BF16), big(BF16), big(BF16), big(BF16),
                        big(F32), big(F32), big(F32)],
        compiler_params=_cparams(("parallel", "arbitrary")),
        name="rwkv7",
    )(p, p, p, p, ps, buf, bufx, mu, mux, par, w2p, a2p, state, mix, acc)


def _trunk_layer(l, x_f32, x_bf, w_in_t, w_out_b, hg_par, ml_nw, ml_bias, rw_mu, rw_mux, rw_par, rw_w2, rw_a2,
                 ln_g, ln_b, groups, cfg):
    p = _proj_matmul(x_bf, w_in_t, l, cfg["mm_tm"], cfg["mm_tn"])
    ps = _proj_small(x_bf, w_in_t, l, cfg["mm_tm"])
    mix = jnp.zeros(x_bf.shape, BF16)
    for grp in groups:
        row0, batch, seq = grp["row0"], grp["batch"], grp["seq"]
        mix, grp["acc_hg"] = _hgrn(p, hg_par, grp["hg"], mix, grp["acc_hg"], l, row0, batch, seq,
                                   grp["hg_tc"], grp["hg_chunk"])
        mix, grp["acc_c"], grp["acc_n"], grp["acc_m"] = _mlstm(
            p, ps, ml_nw, ml_bias, grp["ml_c"], grp["ml_n"], grp["ml_m"], mix, grp["acc_c"], grp["acc_n"],
            grp["acc_m"], l, row0, batch, seq, grp["ml_tc"], grp["ml_chunk"])
        mix, grp["acc_rw"] = _rwkv(p, ps, grp["rw_buf"][l], grp["rw_bufx"][l], rw_mu, rw_mux, rw_par, rw_w2, rw_a2,
                                   grp["rw"], mix, grp["acc_rw"], l, row0, batch, seq, grp["rw_tc"])
        if grp["cache_layout"]:
            mix = _mem_attn_cache(p, grp["mem_k"], grp["mem_v"], mix, l, row0, batch, seq)
        else:
            mix = _mem_attn(p, grp["mem_k"], grp["mem_v"], mix, l, row0, batch, seq, grp["xa_tc"])
        last_rkv = lax.slice(p, (row0 + seq - 1, C_RR), (row0 + batch * seq, C_RR + 3 * GROUP_W), (seq, 1))
        last_x = lax.slice(ps, (row0 + seq - 1, CB_RX * LANE), (row0 + batch * seq, (CB_RX + 1) * LANE), (seq, 1))
        grp["buf_new"].append(jnp.concatenate([last_rkv, last_x], axis=-1))
    x_f32, x_bf = _outproj_ln(mix, w_out_b, l, x_f32, ln_g, ln_b, cfg["op_tm"], cfg["op_tn"])
    return x_f32, x_bf


def kernel(x_prompt, x_sample, mem_prompt, state_hgrn, state_mlstm_C, state_mlstm_n, state_mlstm_m, state_rwkv, state_rwkv_shift, cache_mem_k, cache_mem_v, w_in, hgrn_lb, hgrn_norm_w, mlstm_ig_b, mlstm_fg_b, mlstm_norm_w, rwkv_mu, rwkv_w0, rwkv_w2, rwkv_a0, rwkv_a2, rwkv_k_k, rwkv_k_a, rwkv_r_k, rwkv_ln_w, rwkv_ln_b, mem_wk, mem_wv, w_out, ln_g, ln_b):
    bp, tp_, _ = x_prompt.shape
    bs, ts, _ = x_sample.shape
    depth = w_in.shape[0]
    mp, ms = bp * tp_, bs * ts

    w_in_t = jnp.swapaxes(w_in, 1, 2)
    w_out_b = w_out.astype(BF16)
    lb_all = jnp.cumsum(jax.nn.softmax(hgrn_lb.astype(F32), axis=0), axis=0)
    lb_all = lb_all - lb_all[0]
    zrow = jnp.zeros_like(lb_all)
    hg_par = jnp.stack([jnp.log(lb_all), jnp.log1p(-lb_all), 1.0 - lb_all, hgrn_norm_w.astype(F32),
                        zrow, zrow, zrow, zrow], axis=1)
    ml_nw = mlstm_norm_w.astype(F32)[:, None, :]
    ml_bias = jnp.concatenate([mlstm_ig_b, mlstm_fg_b], axis=-1).astype(F32)[:, None, :]
    rw_mu = rwkv_mu[:, None, :3 * GROUP_W].astype(F32)
    rw_mux = rwkv_mu[:, None, 3 * GROUP_W:].astype(F32)
    zr = jnp.zeros((depth, GROUP_W), F32)
    rw_par = jnp.stack([rwkv_w0, rwkv_a0, rwkv_k_k, rwkv_k_a, rwkv_r_k.reshape(depth, GROUP_W), rwkv_ln_w,
                        rwkv_ln_b, zr], axis=1).astype(F32)
    zl = jnp.zeros((depth, RW_LORA, GROUP_W), F32)
    rw_w2 = jnp.concatenate([rwkv_w2.astype(F32), zl], axis=1).astype(BF16)
    rw_a2 = jnp.concatenate([zl, rwkv_a2.astype(F32)], axis=1).astype(BF16)
    ln_g3 = ln_g.astype(F32)[:, None, :]
    ln_b3 = ln_b.astype(F32)[:, None, :]

    mem_x = mem_prompt.reshape(bp * N_MEM, D_MODEL).astype(BF16)
    mk_p = _matmul_layers(mem_x, mem_wk, 256)
    mv_p = _matmul_layers(mem_x, mem_wv, 256)
    mk_out = mk_p.reshape(depth, bp, N_MEM, XA_HEADS, XA_DH)
    mv_out = mv_p.reshape(depth, bp, N_MEM, XA_HEADS, XA_DH)

    def split_buf(buf):
        return buf[:, :, None, :3 * GROUP_W].astype(F32), buf[:, :, None, 3 * GROUP_W:].astype(F32)

    def cache_view(c):
        c = c.reshape(depth, bs, N_MEM, XA_HEADS, XA_SUB, LANE)
        return jnp.transpose(c, (0, 1, 2, 4, 3, 5)).reshape(depth, bs * XA_ROWS, LANE)

    def results(b):
        return dict(acc_hg=jnp.zeros((depth, b, HG_HEADS, HG_D, HG_D), F32),
                    acc_c=jnp.zeros((depth, b, ML_HEADS, ML_DK, ML_DV), F32),
                    acc_n=jnp.zeros((depth, b, ML_HEADS, ML_DK), F32),
                    acc_m=jnp.zeros((depth, b, 8, LANE), F32),
                    acc_rw=jnp.zeros((depth, b, RW_HEADS, RW_HEAD, RW_HEAD), F32), buf_new=[])

    zbuf, zbufx = split_buf(jnp.zeros((depth, bp, RW_SHIFT_W), F32))
    sbuf, sbufx = split_buf(state_rwkv_shift)
    prompt = dict(row0=0, batch=bp, seq=tp_,
                  hg=jnp.zeros((bp, HG_HEADS, HG_D, HG_D), F32), hg_tc=256, hg_chunk=16,
                  ml_c=jnp.zeros((bp, ML_HEADS, ML_DK, ML_DV), F32), ml_n=jnp.zeros((bp, ML_HEADS, ML_DK), F32),
                  ml_m=jnp.zeros((bp, 1, ML_HEADS), F32), ml_tc=256, ml_chunk=64,
                  rw=jnp.zeros((bp, RW_HEADS, RW_HEAD, RW_HEAD), F32), rw_buf=zbuf, rw_bufx=zbufx, rw_tc=128,
                  mem_k=mk_p, mem_v=mv_p, xa_tc=512, cache_layout=False, **results(bp))
    sample = dict(row0=mp, batch=bs, seq=ts,
                  hg=state_hgrn, hg_tc=ts, hg_chunk=ts,
                  ml_c=state_mlstm_C, ml_n=state_mlstm_n, ml_m=state_mlstm_m[:, :, None, :], ml_tc=ts, ml_chunk=ts,
                  rw=state_rwkv, rw_buf=sbuf, rw_bufx=sbufx, rw_tc=ts,
                  mem_k=cache_view(cache_mem_k), mem_v=cache_view(cache_mem_v), cache_layout=True, **results(bs))
    cfg = dict(mm_tm=1536, mm_tn=512, op_tm=384, op_tn=512)

    x_f32 = jnp.concatenate([x_prompt.reshape(mp, D_MODEL), x_sample.reshape(ms, D_MODEL)], axis=0).astype(F32)
    x_bf = x_f32.astype(BF16)
    for l in range(depth):
        x_f32, x_bf = _trunk_layer(l, x_f32, x_bf, w_in_t, w_out_b, hg_par, ml_nw, ml_bias, rw_mu, rw_mux,
                                   rw_par, rw_w2, rw_a2, ln_g3, ln_b3, [prompt, sample], cfg)

    def states(g):
        return (g["acc_hg"], g["acc_c"], g["acc_n"], g["acc_m"][:, :, :ML_HEADS, 0], g["acc_rw"],
                jnp.stack(g["buf_new"], axis=0))

    y_prompt = x_f32[:mp].reshape(bp, tp_, D_MODEL)
    y_sample = x_f32[mp:].reshape(bs, ts, D_MODEL)
    return (y_prompt, y_sample) + states(prompt) + (mk_out, mv_out) + states(sample)
```

```python
import functools
import math

import jax
import jax.numpy as jnp
from jax import lax
from jax.experimental import pallas as pl
from jax.experimental.pallas import tpu as pltpu

F32 = jnp.float32
BF16 = jnp.bfloat16
HI = lax.Precision.HIGHEST

D_MODEL = 4096
DEPTH = 4
GROUP_W = D_MODEL // 4
N_MEM = 256
HG_HEADS, HG_D = 8, 128
ML_HEADS, ML_DK, ML_DV = 4, 128, 256
RW_HEADS, RW_HEAD, RW_LORA = 16, 64, 64
RW_PAIRS = RW_HEADS // 2
XA_HEADS, XA_DH = 4, 256
RW_SHIFT_W = 3 * GROUP_W + 2 * RW_LORA
N_IN = 13448
DN_ALPHA = (2.0 * DEPTH) ** 0.25
LN_EPS = 1e-5
RW_GN_EPS = 64e-5

C_HQ, C_HF, C_HI, C_HZ = 0, 1024, 2048, 3072
C_MQK, C_MV, C_MZ = 4096, 5120, 6144
C_RR, C_RK, C_RV, C_RZ = 7168, 8192, 9216, 10240
C_XQ, C_XZ = 11264, 12288
NP_MAIN = 13312
ORIG_MG, ORIG_MZ, ORIG_RX, ORIG_RZ = 6144, 6152, 10248, 10376
CB_RX, CB_MG = 0, 1
LANE = 128

VMEM_LIMIT = 60 * 1024 * 1024

NT_DIMS = (((1,), (1,)), ((), ()))
TN_DIMS = (((0,), (0,)), ((), ()))


def _cparams(sem):
    return pltpu.CompilerParams(dimension_semantics=sem, vmem_limit_bytes=VMEM_LIMIT)


def _sigmoid(x):
    return jax.nn.sigmoid(x)


def _silu(x):
    return x * _sigmoid(x)


def _log_sigmoid(x):
    return jnp.minimum(x, 0.0) - jnp.log1p(jnp.exp(-jnp.abs(x)))


def _softplus(x):
    return jnp.maximum(x, 0.0) + jnp.log1p(jnp.exp(-jnp.abs(x)))


def _chunk_masks(n, chunk):
    r = lax.broadcasted_iota(jnp.int32, (n, n), 0)
    c = lax.broadcasted_iota(jnp.int32, (n, n), 1)
    sh = int(math.log2(chunk))
    same = lax.shift_right_logical(r, sh) == lax.shift_right_logical(c, sh)
    tri = jnp.where(same & (c <= r), 1.0, 0.0).astype(BF16)
    ones = jnp.where(same, 1.0, 0.0).astype(BF16)
    return tri, ones


def _split_bf16(x, terms):
    parts = []
    for i in range(terms):
        part = x.astype(BF16)
        parts.append(part)
        if i + 1 < terms:
            x = x - part.astype(F32)
    return parts


def _dot_sel_left(sel, x, terms):
    return sum(jnp.dot(sel, part, preferred_element_type=F32) for part in _split_bf16(x, terms))


def _dot_sel_right(x, sel, terms):
    return sum(jnp.dot(part, sel, preferred_element_type=F32) for part in _split_bf16(x, terms))


def _chunk_cumsum(x, chunk):
    rid = lax.broadcasted_iota(jnp.int32, x.shape, 0) & (chunk - 1)
    step = 1
    while step < chunk:
        x = x + jnp.where(rid >= step, pltpu.roll(x, step, axis=0), 0.0)
        step *= 2
    return x


def _mm_kernel(x_ref, w_ref, o_ref):
    o_ref[...] = jnp.dot(x_ref[...], w_ref[...].astype(BF16), preferred_element_type=F32)


def _mm_nt_kernel(x_ref, w_ref, o_ref):
    o_ref[...] = lax.dot_general(x_ref[...], w_ref[0].astype(BF16), NT_DIMS, preferred_element_type=F32)


def _proj_matmul(x, w_t, layer, tm, tn):
    m, k = x.shape

    def w_index(i, j):
        col = j * tn
        off = jnp.where(col >= C_RZ, ORIG_RZ - C_RZ, jnp.where(col >= C_MZ, ORIG_MZ - C_MZ, 0))
        return (layer, pl.multiple_of(col + off, 8), 0)

    return pl.pallas_call(
        _mm_nt_kernel,
        grid=(m // tm, NP_MAIN // tn),
        in_specs=[pl.BlockSpec((tm, k), lambda i, j: (i, 0)),
                  pl.BlockSpec((pl.Element(1), pl.Element(tn), pl.Element(k)), w_index)],
        out_specs=pl.BlockSpec((tm, tn), lambda i, j: (i, j)),
        out_shape=jax.ShapeDtypeStruct((m, NP_MAIN), F32),
        compiler_params=_cparams(("parallel", "arbitrary")),
        name="proj_matmul",
    )(x, w_t)


def _proj_small(x, w_t, layer, tm):
    m, k = x.shape

    def w_index(i, j):
        return (layer, pl.multiple_of(jnp.where(j == 0, ORIG_RX, ORIG_MG), 8), 0)

    return pl.pallas_call(
        _mm_nt_kernel,
        grid=(m // tm, 2),
        in_specs=[pl.BlockSpec((tm, k), lambda i, j: (i, 0)),
                  pl.BlockSpec((pl.Element(1), pl.Element(LANE), pl.Element(k)), w_index)],
        out_specs=pl.BlockSpec((tm, LANE), lambda i, j: (i, j)),
        out_shape=jax.ShapeDtypeStruct((m, 2 * LANE), F32),
        compiler_params=_cparams(("parallel", "arbitrary")),
        name="proj_small",
    )(x, w_t)


def _matmul_layers(x, w, tn):
    m, k = x.shape
    depth, _, n = w.shape
    return pl.pallas_call(
        _mm_kernel,
        grid=(depth, n // tn),
        in_specs=[pl.BlockSpec((m, k), lambda l, j: (0, 0)),
                  pl.BlockSpec((None, k, tn), lambda l, j: (l, 0, j))],
        out_specs=pl.BlockSpec((None, m, tn), lambda l, j: (l, 0, j)),
        out_shape=jax.ShapeDtypeStruct((depth, m, n), F32),
        compiler_params=_cparams(("parallel", "arbitrary")),
        name="mem_kv_matmul",
    )(x, w)


def _outproj_kernel(mix_ref, w_ref, x_ref, g_ref, b_ref, xo_ref, xb_ref, acc_ref, *, tn, nj):
    j = pl.program_id(1)
    acc_ref[j] = jnp.dot(mix_ref[...], w_ref[...], preferred_element_type=F32)

    @pl.when(j == nj - 1)
    def _():
        tm = acc_ref.shape[1]
        tot = jnp.zeros((tm, 1), F32)
        for jj in range(nj):
            y = DN_ALPHA * x_ref[:, jj * tn:(jj + 1) * tn] + acc_ref[jj]
            acc_ref[jj] = y
            tot = tot + jnp.sum(y, axis=1, keepdims=True)
        mean = tot * (1.0 / D_MODEL)
        sq = jnp.zeros((tm, 1), F32)
        for jj in range(nj):
            yc = acc_ref[jj] - mean
            sq = sq + jnp.sum(yc * yc, axis=1, keepdims=True)
        rstd = lax.rsqrt(sq * (1.0 / D_MODEL) + LN_EPS)
        for jj in range(nj):
            sl = slice(jj * tn, (jj + 1) * tn)
            out = (acc_ref[jj] - mean) * rstd * g_ref[:, sl] + b_ref[:, sl]
            xo_ref[:, sl] = out
            xb_ref[:, sl] = out.astype(BF16)


def _outproj_ln(mix, w_out, layer, x, ln_g, ln_b, tm, tn):
    m = x.shape[0]
    nj = D_MODEL // tn
    row_spec = pl.BlockSpec((None, 1, D_MODEL), lambda i, j: (layer, 0, 0))
    blk = pl.BlockSpec((tm, D_MODEL), lambda i, j: (i, 0))
    return pl.pallas_call(
        functools.partial(_outproj_kernel, tn=tn, nj=nj),
        grid=(m // tm, nj),
        in_specs=[blk, pl.BlockSpec((None, D_MODEL, tn), lambda i, j: (layer, 0, j)),
                  blk, row_spec, row_spec],
        out_specs=[blk, blk],
        out_shape=[jax.ShapeDtypeStruct((m, D_MODEL), F32), jax.ShapeDtypeStruct((m, D_MODEL), BF16)],
        scratch_shapes=[pltpu.VMEM((nj, tm, tn), F32)],
        compiler_params=_cparams(("parallel", "arbitrary")),
        name="outproj_ln",
    )(mix, w_out, x, ln_g, ln_b)


def _attn_kernel(q_ref, z_ref, k_ref, v_ref, _mix_ref, o_ref):
    lanes = [slice(h * XA_DH, (h + 1) * XA_DH) for h in range(XA_HEADS)]
    scores = [lax.dot_general(q_ref[:, sl].astype(BF16), k_ref[:, sl].astype(BF16), NT_DIMS,
                              preferred_element_type=F32) * (XA_DH ** -0.5)
              for sl in lanes]
    outs = []
    for sl, s in zip(lanes, scores):
        e = jnp.exp(s - jnp.max(s, axis=1, keepdims=True))
        pr = e / jnp.sum(e, axis=1, keepdims=True)
        outs.append(jnp.dot(pr.astype(BF16), v_ref[:, sl].astype(BF16), preferred_element_type=F32))
    for sl, o in zip(lanes, outs):
        o_ref[:, sl] = (o * _silu(z_ref[:, sl])).astype(BF16)


def _mem_attn(p, mem_k, mem_v, mix, layer, rows):
    pspec = lambda cb: rows.spec(GROUP_W, cb)
    kvspec = pl.BlockSpec((None, N_MEM, GROUP_W), lambda b, t: (layer, b, 0))
    return pl.pallas_call(
        _attn_kernel,
        grid=rows.grid,
        in_specs=[pspec(C_XQ // GROUP_W), pspec(C_XZ // GROUP_W), kvspec, kvspec, ANY_SPEC],
        out_specs=_mix_out_spec(rows, 3),
        out_shape=_sds(mix),
        input_output_aliases={4: 0},
        compiler_params=_cparams(("parallel", "arbitrary")),
        name="mem_attn",
    )(p, p, mem_k, mem_v, mix)


XA_SUB = XA_DH // LANE
XA_ROWS = N_MEM * XA_SUB * XA_HEADS


def _attn_cache_kernel(q_ref, z_ref, k_ref, v_ref, _mix_ref, o_ref):
    t = q_ref.shape[0]
    tiles = XA_HEADS * XA_SUB
    qx = jnp.concatenate([q_ref[:, c * LANE:(c + 1) * LANE] for c in range(tiles)], axis=0).astype(BF16)
    s_all = lax.dot_general(qx, k_ref[...].astype(BF16), NT_DIMS, preferred_element_type=F32)
    col = lax.broadcasted_iota(jnp.int32, (t, XA_ROWS), 1)
    col_head = col & (XA_HEADS - 1)
    col_sub = lax.shift_right_logical(col, 2) & (XA_SUB - 1)
    probs = []
    for h in range(XA_HEADS):
        mine = col_head == h
        part = jnp.zeros((t, XA_ROWS), F32)
        for s in range(XA_SUB):
            rows = slice((h * XA_SUB + s) * t, (h * XA_SUB + s + 1) * t)
            part = part + jnp.where(mine & (col_sub == s), s_all[rows], 0.0)
        other = jnp.where(col_sub == 0, pltpu.roll(part, XA_ROWS - XA_HEADS, axis=1),
                          pltpu.roll(part, XA_HEADS, axis=1))
        sc = jnp.where(mine, (part + other) * (XA_DH ** -0.5), -jnp.inf)
        e = jnp.exp(sc - jnp.max(sc, axis=1, keepdims=True))
        probs.append(e / (jnp.sum(e, axis=1, keepdims=True) * (1.0 / XA_SUB)))
    pr = jnp.concatenate(probs, axis=0)
    sub_all = lax.shift_right_logical(lax.broadcasted_iota(jnp.int32, pr.shape, 1), 2) & (XA_SUB - 1)
    vb = v_ref[...].astype(BF16)
    outs = [jnp.dot(jnp.where(sub_all == s, pr, 0.0).astype(BF16), vb, preferred_element_type=F32)
            for s in range(XA_SUB)]
    for h in range(XA_HEADS):
        for s in range(XA_SUB):
            sl = slice((h * XA_SUB + s) * LANE, (h * XA_SUB + s + 1) * LANE)
            o_ref[:, sl] = (outs[s][h * t:(h + 1) * t] * _silu(z_ref[:, sl])).astype(BF16)


def _mem_attn_cache(p, cache_k, cache_v, mix, layer, rows):
    assert XA_SUB == 2 and XA_HEADS == 4 and rows.nt == 1 and rows.nb == 1
    pspec = lambda cb: rows.spec(GROUP_W, cb)
    kvspec = pl.BlockSpec((None, XA_ROWS, LANE), lambda b, t: (layer, b, 0))
    return pl.pallas_call(
        _attn_cache_kernel,
        grid=rows.grid,
        in_specs=[pspec(C_XQ // GROUP_W), pspec(C_XZ // GROUP_W), kvspec, kvspec, ANY_SPEC],
        out_specs=_mix_out_spec(rows, 3),
        out_shape=_sds(mix),
        input_output_aliases={4: 0},
        compiler_params=_cparams(("parallel", "arbitrary")),
        name="mem_attn_cache",
    )(p, p, cache_k, cache_v, mix)


def _hgrn_kernel(q_ref, f_ref, i_ref, z_ref, par_ref, s0_ref, _mix_ref, _acc_ref, o_ref, s_ref,
                 st_scr, b_scr, qs_scr, kk_scr, qt_scr, vb_scr, h_scr, *, chunk, tc, nt):
    t = pl.program_id(1)
    d = HG_D
    sub = 8

    @pl.when(t == 0)
    def _():
        for h in range(HG_HEADS):
            st_scr[h] = s0_ref[0, h].T

    log_lb = par_ref[0:1, :]
    log1m_lb = par_ref[1:2, :]
    one_m_lb = par_ref[2:3, :]
    norm_w = par_ref[3:4, :]

    fpre = f_ref[...]
    bt = log1m_lb + _log_sigmoid(fpre)
    logf = jnp.maximum(log_lb, bt) + jnp.log1p(jnp.exp(-jnp.abs(log_lb - bt)))
    b = _chunk_cumsum(logf, chunk)
    qs = _silu(q_ref[...])
    b_scr[...] = b
    qs_scr[...] = qs
    kk_scr[...] = one_m_lb * _sigmoid(-fpre)
    qt_scr[...] = (qs * jnp.exp(b)).astype(BF16)
    vb_scr[...] = i_ref[...].astype(BF16)

    rid = lax.broadcasted_iota(jnp.int32, (sub, d), 0)

    def intra(bc, qc, kc, vc):
        blocks = []
        for rb in range(chunk // sub):
            rs = slice(rb * sub, (rb + 1) * sub)
            bb, qb = bc[rs], qc[rs]
            o = jnp.zeros((sub, d), F32)
            for s in range((rb + 1) * sub):
                diff = bb - bc[s:s + 1, :]
                if s >= rb * sub:
                    diff = jnp.where(rid >= s - rb * sub, diff, -jnp.inf)
                a = jnp.sum(qb * kc[s:s + 1, :] * jnp.exp(diff), axis=1, keepdims=True)
                o = o + a * vc[s:s + 1, :]
            blocks.append(o)
        return blocks[0] if len(blocks) == 1 else jnp.concatenate(blocks, axis=0)

    def chunk_body(c, carry):
        r0 = c * chunk if isinstance(c, int) else pl.multiple_of(c * chunk, chunk)
        rows = pl.ds(r0, chunk)
        lanes = [slice(h * d, (h + 1) * d) for h in range(HG_HEADS)]
        inter, upd, decay = [], [], []
        for h, sl in enumerate(lanes):
            bc = b_scr[rows, sl]
            b_last = bc[chunk - 1:chunk, :]
            kt = (kk_scr[rows, sl] * jnp.exp(b_last - bc)).astype(BF16)
            inter.append(lax.dot_general(qt_scr[rows, sl], st_scr[h].astype(BF16), NT_DIMS,
                                         preferred_element_type=F32))
            upd.append(lax.dot_general(vb_scr[rows, sl], kt, TN_DIMS, preferred_element_type=F32))
            decay.append(jnp.exp(b_last))
        for h, sl in enumerate(lanes):
            o = intra(b_scr[rows, sl], qs_scr[rows, sl], kk_scr[rows, sl], i_ref[rows, sl])
            h_scr[rows, sl] = o + inter[h]
            st_scr[h] = st_scr[h] * decay[h] + upd[h]
        return carry

    nchunk = tc // chunk
    if nchunk == 1:
        chunk_body(0, 0)
    else:
        lax.fori_loop(0, nchunk, chunk_body, 0)

    for h in range(HG_HEADS):
        sl = slice(h * d, (h + 1) * d)
        hh = h_scr[:, sl]
        ms = jnp.mean(hh * hh, axis=1, keepdims=True)
        y = hh * lax.rsqrt(ms + 1e-5) * norm_w[:, sl] * _silu(z_ref[:, sl])
        o_ref[:, sl] = y.astype(BF16)

    @pl.when(t == nt - 1)
    def _():
        for h in range(HG_HEADS):
            s_ref[0, h] = st_scr[h].T


ANY_SPEC = pl.BlockSpec(memory_space=pl.ANY)


def _state_in_spec(state, layer, tail, nb=1):
    zeros = (0,) * len(tail)
    if state.ndim == len(tail) + 2:
        return pl.BlockSpec((None, nb) + tail, lambda g, t: (layer, g) + zeros)
    return pl.BlockSpec((nb,) + tail, lambda g, t: (g,) + zeros)


def _state_out_spec(layer, tail, nb=1):
    zeros = (0,) * len(tail)
    return pl.BlockSpec((None, nb) + tail, lambda g, t: (layer, g) + zeros)


class _Rows:
    def __init__(self, row0, batch, seq, te, nb=1):
        self.nb, self.te, self.rows = nb, te, nb * te
        self.groups, self.nt = batch // nb, seq // te
        self.rb0 = row0 // self.rows
        self.grid = (self.groups, self.nt)

    def spec(self, width, col_block):
        groups, rb0 = self.groups, self.rb0
        return pl.BlockSpec((self.rows, width), lambda g, t: (rb0 + t * groups + g, col_block))


def _mix_out_spec(rows, group):
    return rows.spec(GROUP_W, group)


def _sds(x):
    return jax.ShapeDtypeStruct(x.shape, x.dtype)


def _hgrn(p, par, state, mix, acc, layer, rows, chunk):
    tc, nt = rows.te, rows.nt
    pspec = lambda cb: rows.spec(GROUP_W, cb)
    tail = (HG_HEADS, HG_D, HG_D)
    big = lambda dt: pltpu.VMEM((tc, GROUP_W), dt)
    return pl.pallas_call(
        functools.partial(_hgrn_kernel, chunk=chunk, tc=tc, nt=nt),
        grid=rows.grid,
        in_specs=[pspec(C_HQ // GROUP_W), pspec(C_HF // GROUP_W), pspec(C_HI // GROUP_W), pspec(C_HZ // GROUP_W),
                  pl.BlockSpec((None, 8, GROUP_W), lambda b, t: (layer, 0, 0)),
                  _state_in_spec(state, layer, tail), ANY_SPEC, ANY_SPEC],
        out_specs=[_mix_out_spec(rows, 0), _state_out_spec(layer, tail)],
        out_shape=[_sds(mix), _sds(acc)],
        input_output_aliases={6: 0, 7: 1},
        scratch_shapes=[pltpu.VMEM((HG_HEADS, HG_D, HG_D), F32),
                        big(F32), big(F32), big(F32), big(BF16), big(BF16), big(F32)],
        compiler_params=_cparams(("parallel", "arbitrary")),
        name="hgrn2",
    )(p, p, p, p, par, state, mix, acc)


def _mlstm_kernel(qk_ref, v_ref, z_ref, g_ref, par_ref, bias_ref, c0_ref, n0_ref, m0_ref,
                  _mix_ref, _acc_c_ref, _acc_n_ref, _acc_m_ref,
                  o_ref, c_ref, n_ref, m_ref, m_scr, h_scr, *, chunk, tc, nt):
    t = pl.program_id(1)
    L = chunk

    @pl.when(t == 0)
    def _():
        c_ref[...] = c0_ref[...]
        n_ref[...] = n0_ref[...]
        m_scr[...] = jnp.zeros(m_scr.shape, F32)
        for h in range(ML_HEADS):
            m_scr[h:h + 1, :] = jnp.broadcast_to(m0_ref[0, :, h:h + 1], (1, LANE))

    r = lax.broadcasted_iota(jnp.int32, (L, L), 0)
    c = lax.broadcasted_iota(jnp.int32, (L, L), 1)
    tril = r >= c
    triu = r <= c

    heads = range(ML_HEADS)
    nchunk = tc // L
    pre = {}
    for ci in range(nchunk):
        rows = slice(ci * L, (ci + 1) * L)
        g = g_ref[rows, :]
        gt = g.T
        for h in heads:
            ib = bias_ref[:, h:h + 1]
            fb = bias_ref[:, ML_HEADS + h:ML_HEADS + h + 1]
            i_col = g[:, h:h + 1] + ib
            i_row = gt[h:h + 1, 0:L] + ib
            lf_col = _log_sigmoid(g[:, ML_HEADS + h:ML_HEADS + h + 1] + fb)
            lf_row = _log_sigmoid(gt[ML_HEADS + h:ML_HEADS + h + 1, 0:L] + fb)
            b_col = jnp.sum(jnp.where(tril, lf_row, 0.0), axis=1, keepdims=True)
            b_row = jnp.sum(jnp.where(triu, lf_col, 0.0), axis=0, keepdims=True)
            dmat = jnp.where(tril, b_col - b_row + i_row, -jnp.inf)
            q = qk_ref[rows, h * ML_DK:(h + 1) * ML_DK]
            k = qk_ref[rows, ML_HEADS * ML_DK + h * ML_DK:ML_HEADS * ML_DK + (h + 1) * ML_DK] * (ML_DK ** -0.5)
            qb = q.astype(BF16)
            pre[ci, h] = dict(
                i_col=i_col, b_col=b_col, dmat=dmat, dmax=jnp.max(dmat, axis=1, keepdims=True),
                q=q, k=k, qb=qb, vb=v_ref[rows, h * ML_DV:(h + 1) * ML_DV].astype(BF16),
                s_qk=lax.dot_general(qb, k.astype(BF16), NT_DIMS, preferred_element_type=F32))

    for ci in range(nchunk):
        rows = slice(ci * L, (ci + 1) * L)
        mid = []
        for h in heads:
            e = pre[ci, h]
            m = m_scr[h:h + 1, 0:1]
            m_t = jnp.maximum(e["b_col"] + m, e["dmax"])
            pm = jnp.exp(e["dmat"] - m_t) * e["s_qk"]
            cst = c_ref[0, h]
            mid.append(dict(m=m, m_t=m_t, pm=pm, cst=cst,
                            pv=jnp.dot(pm.astype(BF16), e["vb"], preferred_element_type=F32),
                            qc=jnp.dot(e["qb"], cst.astype(BF16), preferred_element_type=F32)))
        for h in heads:
            e, u = pre[ci, h], mid[h]
            m, m_t, b_col = u["m"], u["m_t"], e["b_col"]
            nst = n_ref[0, h:h + 1, :]
            inter = jnp.exp(b_col + m - m_t)
            num = u["pv"] + inter * u["qc"]
            den = jnp.sum(u["pm"], axis=1, keepdims=True) + inter * jnp.sum(e["q"] * nst, axis=1, keepdims=True)
            hh = num / jnp.maximum(jnp.abs(den), jnp.exp(-m_t))
            m_new = m_t[L - 1:L, :]
            b_last = b_col[L - 1:L, :]
            wgt = jnp.exp(b_last - b_col + e["i_col"] - m_new)
            decay = jnp.exp(b_last + m - m_new)
            kw = e["k"] * wgt
            c_ref[0, h] = decay * u["cst"] + lax.dot_general(kw.astype(BF16), e["vb"], TN_DIMS,
                                                             preferred_element_type=F32)
            n_ref[0, h:h + 1, :] = decay * nst + jnp.sum(kw, axis=0, keepdims=True)
            m_scr[h:h + 1, :] = jnp.broadcast_to(m_new, (1, LANE))
            h_scr[rows, h * ML_DV:(h + 1) * ML_DV] = hh

    for h in range(ML_HEADS):
        sl = slice(h * ML_DV, (h + 1) * ML_DV)
        x = h_scr[:, sl]
        xc = x - jnp.mean(x, axis=1, keepdims=True)
        y = xc * lax.rsqrt(jnp.mean(xc * xc, axis=1, keepdims=True) + 1e-6)
        o_ref[:, sl] = (y * par_ref[:, sl] * _silu(z_ref[:, sl])).astype(BF16)

    @pl.when(t == nt - 1)
    def _():
        m_ref[0] = m_scr[...]


def _mlstm(p, ps, norm_w, bias, c0, n0, m0, mix, acc_c, acc_n, acc_m, layer, rows, chunk):
    tc, nt = rows.te, rows.nt
    pspec = lambda cb: rows.spec(GROUP_W, cb)
    c_tail, n_tail, m_tail = (ML_HEADS, ML_DK, ML_DV), (ML_HEADS, ML_DK), (1, ML_HEADS)
    return pl.pallas_call(
        functools.partial(_mlstm_kernel, chunk=chunk, tc=tc, nt=nt),
        grid=rows.grid,
        in_specs=[pspec(C_MQK // GROUP_W), pspec(C_MV // GROUP_W), pspec(C_MZ // GROUP_W),
                  rows.spec(LANE, CB_MG),
                  pl.BlockSpec((None, 1, GROUP_W), lambda b, t: (layer, 0, 0)),
                  pl.BlockSpec((None, 1, 2 * ML_HEADS), lambda b, t: (layer, 0, 0)),
                  _state_in_spec(c0, layer, c_tail), _state_in_spec(n0, layer, n_tail),
                  _state_in_spec(m0, layer, m_tail), ANY_SPEC, ANY_SPEC, ANY_SPEC, ANY_SPEC],
        out_specs=[_mix_out_spec(rows, 1), _state_out_spec(layer, c_tail), _state_out_spec(layer, n_tail),
                   _state_out_spec(layer, (8, LANE))],
        out_shape=[_sds(mix), _sds(acc_c), _sds(acc_n), _sds(acc_m)],
        input_output_aliases={9: 0, 10: 1, 11: 2, 12: 3},
        scratch_shapes=[pltpu.VMEM((8, LANE), F32), pltpu.VMEM((tc, GROUP_W), F32)],
        compiler_params=_cparams(("parallel", "arbitrary")),
        name="mlstm",
    )(p, p, p, ps, norm_w, bias, c0, n0, m0, mix, acc_c, acc_n, acc_m)


RW_CHUNK = 16


def _rwkv_kernel(r_ref, k_ref, v_ref, z_ref, x_ref, buf_ref, bufx_ref, mu_ref, mux_ref, par_ref,
                 w2_ref, a2_ref, s0_ref, _mix_ref, _acc_ref, o_ref, s_ref,
                 sp_scr, prev_scr, prevx_scr, kh_scr, rh_scr, ki_scr, ai_scr, kd_scr, ad_scr, vb_scr,
                 gl_scr, y_scr, bonus_scr, *, nb, te, nt):
    t = pl.program_id(1)
    L = RW_CHUNK
    W = GROUP_W
    tc = nb * te
    tep = max(te, L)
    tp = nb * tep

    lane2 = lax.broadcasted_iota(jnp.int32, (LANE, LANE), 1)
    row2 = lax.broadcasted_iota(jnp.int32, (LANE, LANE), 0)
    diag_blocks = (lane2 < RW_HEAD) == (row2 < RW_HEAD)
    seg_ones = jnp.where(diag_blocks, 1.0, 0.0).astype(BF16)

    @pl.when(t == 0)
    def _():
        zero = jnp.zeros((RW_HEAD, RW_HEAD), F32)
        for e in range(nb):
            prev_scr[e:e + 1, :] = buf_ref[e]
            prevx_scr[e:e + 1, :] = bufx_ref[e]
            for pr in range(RW_PAIRS):
                top = jnp.concatenate([s0_ref[e, 2 * pr], zero], axis=1)
                bot = jnp.concatenate([zero, s0_ref[e, 2 * pr + 1]], axis=1)
                sp_scr[e * RW_PAIRS + pr] = jnp.concatenate([top, bot], axis=0)

    def seg_sum(x):
        parts = [_dot_sel_right(x[:, i * LANE:(i + 1) * LANE], seg_ones, 2) for i in range(x.shape[1] // LANE)]
        return jnp.concatenate(parts, axis=1)

    def mix(cur, prev_rows, mu):
        rid = lax.broadcasted_iota(jnp.int32, cur.shape, 0)
        prev = pltpu.roll(cur, 1, axis=0) if tc > 1 else cur
        for e in range(nb):
            prev = jnp.where(rid == e * te, prev_rows[e:e + 1, :], prev)
        return cur + (prev - cur) * mu

    pr_ = r_ref[...]
    pk_ = k_ref[...]
    pv_ = v_ref[...]
    px_ = x_ref[...]
    xr = mix(pr_, prev_scr[:, 0:W], mu_ref[:, 0:W])
    xk = mix(pk_, prev_scr[:, W:2 * W], mu_ref[:, W:2 * W])
    xv = mix(pv_, prev_scr[:, 2 * W:3 * W], mu_ref[:, 2 * W:3 * W])
    xx = mix(px_, prevx_scr[...], mux_ref[...])
    for e in range(nb):
        last = slice((e + 1) * te - 1, (e + 1) * te)
        prev_scr[e:e + 1, 0:W] = pr_[last, :]
        prev_scr[e:e + 1, W:2 * W] = pk_[last, :]
        prev_scr[e:e + 1, 2 * W:3 * W] = pv_[last, :]
        prevx_scr[e:e + 1, :] = px_[last, :]

    w0 = par_ref[0:1, :]
    a0 = par_ref[1:2, :]
    k_k = par_ref[2:3, :]
    k_a = par_ref[3:4, :]
    r_k = par_ref[4:5, :]
    ln_w = par_ref[5:6, :]
    ln_b = par_ref[6:7, :]

    wlin = w0 + jnp.dot(jnp.tanh(xx).astype(BF16), w2_ref[...], preferred_element_type=F32)
    wdec = -_softplus(-wlin) - 0.5
    logd = -jnp.exp(wdec)
    a = _sigmoid(a0 + jnp.dot(xx.astype(BF16), a2_ref[...], preferred_element_type=F32))
    kk = xk * k_k
    kk = kk / jnp.maximum(jnp.sqrt(seg_sum(kk * kk)), 1e-12)
    kp = xk * (1.0 + (a - 1.0) * k_a)
    alpha = a * kk
    bonus = seg_sum(xr * kp * r_k) * xv

    if tep > te:
        def pad(u):
            zeros = jnp.zeros((tep - te, u.shape[1]), F32)
            return jnp.concatenate([piece for e in range(nb) for piece in (u[e * te:(e + 1) * te], zeros)], axis=0)
        logd, kk, kp, alpha, xr, xv = pad(logd), pad(kk), pad(kp), pad(alpha), pad(xr), pad(xv)
    bonus_scr[...] = bonus

    tri, ones = _chunk_masks(tp, L)
    g = _dot_sel_left(tri, logd, 3)
    gl = _dot_sel_left(ones, logd, 3)
    einv = jnp.exp(-g)
    egl = jnp.exp(gl - g)
    kh_scr[...] = (kk * jnp.exp(g - logd)).astype(BF16)
    rh_scr[...] = (xr * jnp.exp(g)).astype(BF16)
    ki_scr[...] = (kp * einv).astype(BF16)
    ai_scr[...] = (alpha * einv).astype(BF16)
    kd_scr[...] = (kp * egl).astype(BF16)
    ad_scr[...] = (alpha * egl).astype(BF16)
    vb_scr[...] = xv.astype(BF16)
    gl_scr[...] = jnp.exp(gl)

    lane_l = lax.broadcasted_iota(jnp.int32, (L, LANE), 1)
    head_a = lane_l < RW_HEAD
    rl = lax.broadcasted_iota(jnp.int32, (L, L), 0)
    cl = lax.broadcasted_iota(jnp.int32, (L, L), 1)
    incl = rl >= cl
    rl2 = lax.broadcasted_iota(jnp.int32, (2 * L, L), 0) & (L - 1)
    strict2 = rl2 > lax.broadcasted_iota(jnp.int32, (2 * L, L), 1)
    cl_ab = lax.broadcasted_iota(jnp.int32, (L, 2 * L), 1) & (L - 1)
    strict_ab = lax.broadcasted_iota(jnp.int32, (L, 2 * L), 0) > cl_ab
    sp_r = lax.broadcasted_iota(jnp.int32, (2 * L, (L - 1) * LANE), 0)
    sp_c = lax.broadcasted_iota(jnp.int32, (2 * L, (L - 1) * LANE), 1)
    sp_head = jnp.where((sp_c & (LANE - 1)) < RW_HEAD, 0, L)
    spread = jnp.where(sp_r == lax.shift_right_logical(sp_c, 7) + sp_head, 1.0, 0.0).astype(BF16)
    zb = jnp.zeros((L, LANE), BF16)

    def chunk_body(c, carry):
        pairs = range(nb * RW_PAIRS)
        lanes = [slice((i % RW_PAIRS) * LANE, (i % RW_PAIRS + 1) * LANE) for i in pairs]
        starts = []
        for e in range(nb):
            r0 = e * tep + c * L
            starts.append(r0 if isinstance(c, int) else pl.multiple_of(r0, L))
        rws = [pl.ds(starts[i // RW_PAIRS], L) for i in pairs]
        vb = [vb_scr[rws[i], lanes[i]] for i in pairs]
        gm, ks, nab = [], [], []
        for pr in pairs:
            sl, rows = lanes[pr], rws[pr]
            kh = kh_scr[rows, sl]
            rh = rh_scr[rows, sl]
            x4 = jnp.concatenate([jnp.where(head_a, kh, zb), jnp.where(head_a, zb, kh),
                                  jnp.where(head_a, rh, zb), jnp.where(head_a, zb, rh)], axis=0)
            ai = ai_scr[rows, sl]
            y2 = jnp.concatenate([ai, ki_scr[rows, sl]], axis=0)
            gm.append(lax.dot_general(x4, y2, NT_DIMS, preferred_element_type=F32))
            ai2 = jnp.concatenate([jnp.where(head_a, ai, zb), jnp.where(head_a, zb, ai)], axis=0)
            nab.append(lax.dot_general(kh, ai2, NT_DIMS, preferred_element_type=F32))
            ks.append(lax.dot_general(jnp.concatenate([kh, rh], axis=0), sp_scr[pr].astype(BF16), NT_DIMS,
                                      preferred_element_type=F32))
        mv, coef = [], []
        for pr in pairs:
            g = gm[pr]
            m_ab = jnp.where(strict2, g[0:2 * L, L:2 * L], 0.0)
            mv.append(jnp.dot(m_ab.astype(BF16), vb[pr], preferred_element_type=F32))
            coef.append(_dot_sel_right(jnp.where(strict_ab, nab[pr], 0.0), spread, 1))
        vw = []
        for pr in pairs:
            w = ks[pr][0:L] + jnp.where(head_a, mv[pr][0:L], mv[pr][L:2 * L])
            for s in range(L - 1):
                w = w - coef[pr][:, s * LANE:(s + 1) * LANE] * w[s:s + 1, :]
            vw.append(jnp.concatenate([vb[pr], w.astype(BF16)], axis=0))
        yy, upd = [], []
        for pr in pairs:
            g = gm[pr]
            cm = jnp.concatenate([
                jnp.concatenate([jnp.where(incl, g[2 * L:3 * L, L:2 * L], 0.0),
                                 -jnp.where(incl, g[2 * L:3 * L, 0:L], 0.0)], axis=1),
                jnp.concatenate([jnp.where(incl, g[3 * L:4 * L, L:2 * L], 0.0),
                                 -jnp.where(incl, g[3 * L:4 * L, 0:L], 0.0)], axis=1)], axis=0)
            yy.append(jnp.dot(cm.astype(BF16), vw[pr], preferred_element_type=F32))
        grp = LANE // (2 * L)
        zblk = jnp.zeros((2 * L, LANE), BF16)
        for g0 in range(0, nb * RW_PAIRS, grp):
            members = range(g0, g0 + grp)
            vw_t = jnp.concatenate([vw[pr] for pr in members], axis=0).T
            kd_rows = []
            for q, pr in enumerate(members):
                sl, rows = lanes[pr], rws[pr]
                kd_ad = jnp.concatenate([kd_scr[rows, sl], -ad_scr[rows, sl]], axis=0)
                kd_rows.append(jnp.concatenate([kd_ad if col == q else zblk for col in range(grp)], axis=1))
            u_all = jnp.dot(vw_t, jnp.concatenate(kd_rows, axis=0), preferred_element_type=F32)
            for q in range(grp):
                upd.append(u_all[:, q * LANE:(q + 1) * LANE])
        for pr in pairs:
            sl, rows = lanes[pr], rws[pr]
            y_scr[rows, sl] = ks[pr][L:2 * L] + jnp.where(head_a, yy[pr][0:L], yy[pr][L:2 * L])
            gl_row = gl_scr[pl.ds(starts[pr // RW_PAIRS], 1), sl]
            sp_scr[pr] = sp_scr[pr] * gl_row + jnp.where(diag_blocks, upd[pr], 0.0)
        return carry

    nchunk = tep // L
    if nchunk == 1:
        chunk_body(0, 0)
    else:
        lax.fori_loop(0, nchunk, chunk_body, 0)

    if tep > te:
        y = jnp.concatenate([y_scr[e * tep:e * tep + te, :] for e in range(nb)], axis=0)
    else:
        y = y_scr[...]
    yc = y - seg_sum(y) * (1.0 / RW_HEAD)
    yn = yc * lax.rsqrt(seg_sum(yc * yc) * (1.0 / RW_HEAD) + RW_GN_EPS)
    out = (yn * ln_w + ln_b + bonus_scr[...]) * _silu(z_ref[...])
    o_ref[...] = out.astype(BF16)

    @pl.when(t == nt - 1)
    def _():
        for e in range(nb):
            for pr in range(RW_PAIRS):
                sp = sp_scr[e * RW_PAIRS + pr]
                s_ref[e, 2 * pr] = sp[0:RW_HEAD, 0:RW_HEAD]
                s_ref[e, 2 * pr + 1] = sp[RW_HEAD:LANE, RW_HEAD:LANE]


def _rwkv(p, ps, buf, bufx, mu, mux, par, w2p, a2p, state, mix, acc, layer, rows):
    nb, te = rows.nb, rows.te
    tp = nb * max(te, RW_CHUNK)
    pspec = lambda cb: rows.spec(GROUP_W, cb)
    tail = (RW_HEADS, RW_HEAD, RW_HEAD)
    lay = lambda shape: pl.BlockSpec((None,) + shape, lambda b, t: (layer,) + (0,) * len(shape))
    big = lambda dt: pltpu.VMEM((tp, GROUP_W), dt)
    return pl.pallas_call(
        functools.partial(_rwkv_kernel, nb=nb, te=te, nt=rows.nt),
        grid=rows.grid,
        in_specs=[pspec(C_RR // GROUP_W), pspec(C_RK // GROUP_W), pspec(C_RV // GROUP_W), pspec(C_RZ // GROUP_W),
                  rows.spec(LANE, CB_RX),
                  pl.BlockSpec((nb, 1, 3 * GROUP_W), lambda g, t: (g, 0, 0)),
                  pl.BlockSpec((nb, 1, LANE), lambda g, t: (g, 0, 0)),
                  lay((1, 3 * GROUP_W)), lay((1, LANE)), lay((8, GROUP_W)),
                  lay((LANE, GROUP_W)), lay((LANE, GROUP_W)), _state_in_spec(state, layer, tail, nb),
                  ANY_SPEC, ANY_SPEC],
        out_specs=[_mix_out_spec(rows, 2), _state_out_spec(layer, tail, nb)],
        out_shape=[_sds(mix), _sds(acc)],
        input_output_aliases={13: 0, 14: 1},
        scratch_shapes=[pltpu.VMEM((nb * RW_PAIRS, LANE, LANE), F32),
                        pltpu.VMEM((nb, 3 * GROUP_W), F32), pltpu.VMEM((nb, LANE), F32),
                        big(BF16), big(BF16), big(BF16), big(BF16), big(BF16), big(BF16), big(BF16),
                        big(F32), big(F32), pltpu.VMEM((rows.rows, GROUP_W), F32)],
        compiler_params=_cparams(("parallel", "arbitrary")),
        name="rwkv7",
    )(p, p, p, p, ps, buf, bufx, mu, mux, par, w2p, a2p, state, mix, acc)


def _trunk_layer(l, x_f32, x_bf, w_in_t, w_out_b, hg_par, ml_nw, ml_bias, rw_mu, rw_mux, rw_par, rw_w2, rw_a2,
                 ln_g, ln_b, groups, cfg):
    p = _proj_matmul(x_bf, w_in_t, l, cfg["mm_tm"], cfg["mm_tn"])
    ps = _proj_small(x_bf, w_in_t, l, cfg["mm_tm"])
    mix = jnp.zeros(x_bf.shape, BF16)
    for grp in groups:
        row0, batch, seq, te = grp["row0"], grp["batch"], grp["seq"], grp["te"]
        one = _Rows(row0, batch, seq, te)
        mix, grp["acc_hg"] = _hgrn(p, hg_par, grp["hg"], mix, grp["acc_hg"], l, one, grp["hg_chunk"])
        mix, grp["acc_c"], grp["acc_n"], grp["acc_m"] = _mlstm(
            p, ps, ml_nw, ml_bias, grp["ml_c"], grp["ml_n"], grp["ml_m"], mix, grp["acc_c"], grp["acc_n"],
            grp["acc_m"], l, one, grp["ml_chunk"])
        mix, grp["acc_rw"] = _rwkv(p, ps, grp["rw_buf"][l], grp["rw_bufx"][l], rw_mu, rw_mux, rw_par, rw_w2, rw_a2,
                                   grp["rw"], mix, grp["acc_rw"], l, _Rows(row0, batch, seq, te, grp["rw_nb"]))
        if grp["cache_layout"]:
            mix = _mem_attn_cache(p, grp["mem_k"], grp["mem_v"], mix, l, one)
        else:
            mix = _mem_attn(p, grp["mem_k"], grp["mem_v"], mix, l, one)
        first = row0 + (seq // te - 1) * batch * te + te - 1
        last_rkv = lax.slice(p, (first, C_RR), (row0 + batch * seq, C_RR + 3 * GROUP_W), (te, 1))
        last_x = lax.slice(ps, (first, CB_RX * LANE), (row0 + batch * seq, (CB_RX + 1) * LANE), (te, 1))
        grp["buf_new"].append(jnp.concatenate([last_rkv, last_x], axis=-1))
    x_f32, x_bf = _outproj_ln(mix, w_out_b, l, x_f32, ln_g, ln_b, cfg["op_tm"], cfg["op_tn"])
    return x_f32, x_bf


def kernel(x_prompt, x_sample, mem_prompt, state_hgrn, state_mlstm_C, state_mlstm_n, state_mlstm_m, state_rwkv, state_rwkv_shift, cache_mem_k, cache_mem_v, w_in, hgrn_lb, hgrn_norm_w, mlstm_ig_b, mlstm_fg_b, mlstm_norm_w, rwkv_mu, rwkv_w0, rwkv_w2, rwkv_a0, rwkv_a2, rwkv_k_k, rwkv_k_a, rwkv_r_k, rwkv_ln_w, rwkv_ln_b, mem_wk, mem_wv, w_out, ln_g, ln_b):
    bp, tp_, _ = x_prompt.shape
    bs, ts, _ = x_sample.shape
    depth = w_in.shape[0]
    mp, ms = bp * tp_, bs * ts

    w_in_t = jnp.swapaxes(w_in, 1, 2)
    w_out_b = w_out.astype(BF16)
    lb_all = jnp.cumsum(jax.nn.softmax(hgrn_lb.astype(F32), axis=0), axis=0)
    lb_all = lb_all - lb_all[0]
    zrow = jnp.zeros_like(lb_all)
    hg_par = jnp.stack([jnp.log(lb_all), jnp.log1p(-lb_all), 1.0 - lb_all, hgrn_norm_w.astype(F32),
                        zrow, zrow, zrow, zrow], axis=1)
    ml_nw = mlstm_norm_w.astype(F32)[:, None, :]
    ml_bias = jnp.concatenate([mlstm_ig_b, mlstm_fg_b], axis=-1).astype(F32)[:, None, :]
    rw_mu = rwkv_mu[:, None, :3 * GROUP_W].astype(F32)
    rw_mux = rwkv_mu[:, None, 3 * GROUP_W:].astype(F32)
    zr = jnp.zeros((depth, GROUP_W), F32)
    rw_par = jnp.stack([rwkv_w0, rwkv_a0, rwkv_k_k, rwkv_k_a, rwkv_r_k.reshape(depth, GROUP_W), rwkv_ln_w,
                        rwkv_ln_b, zr], axis=1).astype(F32)
    zl = jnp.zeros((depth, RW_LORA, GROUP_W), F32)
    rw_w2 = jnp.concatenate([rwkv_w2.astype(F32), zl], axis=1).astype(BF16)
    rw_a2 = jnp.concatenate([zl, rwkv_a2.astype(F32)], axis=1).astype(BF16)
    ln_g3 = ln_g.astype(F32)[:, None, :]
    ln_b3 = ln_b.astype(F32)[:, None, :]

    mem_x = mem_prompt.reshape(bp * N_MEM, D_MODEL).astype(BF16)
    mk_p = _matmul_layers(mem_x, mem_wk, 256)
    mv_p = _matmul_layers(mem_x, mem_wv, 256)
    mk_out = mk_p.reshape(depth, bp, N_MEM, XA_HEADS, XA_DH)
    mv_out = mv_p.reshape(depth, bp, N_MEM, XA_HEADS, XA_DH)

    def split_buf(buf):
        return buf[:, :, None, :3 * GROUP_W].astype(F32), buf[:, :, None, 3 * GROUP_W:].astype(F32)

    def cache_view(c):
        c = c.reshape(depth, bs, N_MEM, XA_HEADS, XA_SUB, LANE)
        return jnp.transpose(c, (0, 1, 2, 4, 3, 5)).reshape(depth, bs * XA_ROWS, LANE)

    def results(b):
        return dict(acc_hg=jnp.zeros((depth, b, HG_HEADS, HG_D, HG_D), F32),
                    acc_c=jnp.zeros((depth, b, ML_HEADS, ML_DK, ML_DV), F32),
                    acc_n=jnp.zeros((depth, b, ML_HEADS, ML_DK), F32),
                    acc_m=jnp.zeros((depth, b, 8, LANE), F32),
                    acc_rw=jnp.zeros((depth, b, RW_HEADS, RW_HEAD, RW_HEAD), F32), buf_new=[])

    zbuf, zbufx = split_buf(jnp.zeros((depth, bp, RW_SHIFT_W), F32))
    sbuf, sbufx = split_buf(state_rwkv_shift)
    te_p = 128
    prompt = dict(row0=0, batch=bp, seq=tp_, te=te_p,
                  hg=jnp.zeros((bp, HG_HEADS, HG_D, HG_D), F32), hg_chunk=16,
                  ml_c=jnp.zeros((bp, ML_HEADS, ML_DK, ML_DV), F32), ml_n=jnp.zeros((bp, ML_HEADS, ML_DK), F32),
                  ml_m=jnp.zeros((bp, 1, ML_HEADS), F32), ml_chunk=64,
                  rw=jnp.zeros((bp, RW_HEADS, RW_HEAD, RW_HEAD), F32), rw_buf=zbuf, rw_bufx=zbufx, rw_nb=2,
                  mem_k=mk_p, mem_v=mv_p, cache_layout=False, **results(bp))
    sample = dict(row0=mp, batch=bs, seq=ts, te=ts,
                  hg=state_hgrn, hg_chunk=ts,
                  ml_c=state_mlstm_C, ml_n=state_mlstm_n, ml_m=state_mlstm_m[:, :, None, :], ml_chunk=ts,
                  rw=state_rwkv, rw_buf=sbuf, rw_bufx=sbufx, rw_nb=4,
                  mem_k=cache_view(cache_mem_k), mem_v=cache_view(cache_mem_v), cache_layout=True, **results(bs))
    cfg = dict(mm_tm=1536, mm_tn=512, op_tm=384, op_tn=512)

    nblk = tp_ // te_p
    xp = jnp.transpose(x_prompt.reshape(bp, nblk, te_p, D_MODEL), (1, 0, 2, 3)).reshape(mp, D_MODEL)
    x_f32 = jnp.concatenate([xp, x_sample.reshape(ms, D_MODEL)], axis=0).astype(F32)
    x_bf = x_f32.astype(BF16)
    for l in range(depth):
        x_f32, x_bf = _trunk_layer(l, x_f32, x_bf, w_in_t, w_out_b, hg_par, ml_nw, ml_bias, rw_mu, rw_mux,
                                   rw_par, rw_w2, rw_a2, ln_g3, ln_b3, [prompt, sample], cfg)

    def states(g):
        return (g["acc_hg"], g["acc_c"], g["acc_n"], g["acc_m"][:, :, :ML_HEADS, 0], g["acc_rw"],
                jnp.stack(g["buf_new"], axis=0))

    y_prompt = jnp.transpose(x_f32[:mp].reshape(nblk, bp, te_p, D_MODEL), (1, 0, 2, 3)).reshape(bp, tp_, D_MODEL)
    y_sample = x_f32[mp:].reshape(bs, ts, D_MODEL)
    return (y_prompt, y_sample) + states(prompt) + (mk_out, mv_out) + states(sample)
```

```python
import functools
import math

import jax
import jax.numpy as jnp
from jax import lax
from jax.experimental import pallas as pl
from jax.experimental.pallas import tpu as pltpu

F32 = jnp.float32
BF16 = jnp.bfloat16
HI = lax.Precision.HIGHEST

D_MODEL = 4096
DEPTH = 4
GROUP_W = D_MODEL // 4
N_MEM = 256
HG_HEADS, HG_D = 8, 128
ML_HEADS, ML_DK, ML_DV = 4, 128, 256
RW_HEADS, RW_HEAD, RW_LORA = 16, 64, 64
RW_PAIRS = RW_HEADS // 2
XA_HEADS, XA_DH = 4, 256
RW_SHIFT_W = 3 * GROUP_W + 2 * RW_LORA
N_IN = 13448
DN_ALPHA = (2.0 * DEPTH) ** 0.25
LN_EPS = 1e-5
RW_GN_EPS = 64e-5

C_HQ, C_HF, C_HI, C_HZ = 0, 1024, 2048, 3072
C_MQK, C_MV, C_MZ = 4096, 5120, 6144
C_RR, C_RK, C_RV, C_RZ = 7168, 8192, 9216, 10240
C_XQ, C_XZ = 11264, 12288
NP_MAIN = 13312
ORIG_MG, ORIG_MZ, ORIG_RX, ORIG_RZ = 6144, 6152, 10248, 10376
CB_RX, CB_MG = 0, 1
LANE = 128

VMEM_LIMIT = 60 * 1024 * 1024

NT_DIMS = (((1,), (1,)), ((), ()))
TN_DIMS = (((0,), (0,)), ((), ()))


def _cparams(sem):
    return pltpu.CompilerParams(dimension_semantics=sem, vmem_limit_bytes=VMEM_LIMIT)


def _sigmoid(x):
    return jax.nn.sigmoid(x)


def _silu(x):
    return x * _sigmoid(x)


def _log_sigmoid(x):
    return jnp.minimum(x, 0.0) - jnp.log1p(jnp.exp(-jnp.abs(x)))


def _softplus(x):
    return jnp.maximum(x, 0.0) + jnp.log1p(jnp.exp(-jnp.abs(x)))


def _chunk_masks(n, chunk):
    r = lax.broadcasted_iota(jnp.int32, (n, n), 0)
    c = lax.broadcasted_iota(jnp.int32, (n, n), 1)
    sh = int(math.log2(chunk))
    same = lax.shift_right_logical(r, sh) == lax.shift_right_logical(c, sh)
    tri = jnp.where(same & (c <= r), 1.0, 0.0).astype(BF16)
    ones = jnp.where(same, 1.0, 0.0).astype(BF16)
    return tri, ones


def _split_bf16(x, terms):
    parts = []
    for i in range(terms):
        part = x.astype(BF16)
        parts.append(part)
        if i + 1 < terms:
            x = x - part.astype(F32)
    return parts


def _dot_sel_left(sel, x, terms):
    return sum(jnp.dot(sel, part, preferred_element_type=F32) for part in _split_bf16(x, terms))


def _dot_sel_right(x, sel, terms):
    return sum(jnp.dot(part, sel, preferred_element_type=F32) for part in _split_bf16(x, terms))


def _chunk_cumsum(x, chunk):
    rid = lax.broadcasted_iota(jnp.int32, x.shape, 0) & (chunk - 1)
    step = 1
    while step < chunk:
        x = x + jnp.where(rid >= step, pltpu.roll(x, step, axis=0), 0.0)
        step *= 2
    return x


def _mm_kernel(x_ref, w_ref, o_ref):
    o_ref[...] = jnp.dot(x_ref[...], w_ref[...].astype(BF16), preferred_element_type=F32)


def _mm_nt_kernel(x_ref, w_ref, o_ref):
    o_ref[...] = lax.dot_general(x_ref[...], w_ref[0].astype(BF16), NT_DIMS, preferred_element_type=F32)


def _proj_matmul(x, w_t, layer, tm, tn):
    m, k = x.shape

    def w_index(i, j):
        col = j * tn
        off = jnp.where(col >= C_RZ, ORIG_RZ - C_RZ, jnp.where(col >= C_MZ, ORIG_MZ - C_MZ, 0))
        return (layer, pl.multiple_of(col + off, 8), 0)

    return pl.pallas_call(
        _mm_nt_kernel,
        grid=(m // tm, NP_MAIN // tn),
        in_specs=[pl.BlockSpec((tm, k), lambda i, j: (i, 0)),
                  pl.BlockSpec((pl.Element(1), pl.Element(tn), pl.Element(k)), w_index)],
        out_specs=pl.BlockSpec((tm, tn), lambda i, j: (i, j)),
        out_shape=jax.ShapeDtypeStruct((m, NP_MAIN), F32),
        compiler_params=_cparams(("parallel", "arbitrary")),
        name="proj_matmul",
    )(x, w_t)


def _proj_small(x, w_t, layer, tm):
    m, k = x.shape

    def w_index(i, j):
        return (layer, pl.multiple_of(jnp.where(j == 0, ORIG_RX, ORIG_MG), 8), 0)

    return pl.pallas_call(
        _mm_nt_kernel,
        grid=(m // tm, 2),
        in_specs=[pl.BlockSpec((tm, k), lambda i, j: (i, 0)),
                  pl.BlockSpec((pl.Element(1), pl.Element(LANE), pl.Element(k)), w_index)],
        out_specs=pl.BlockSpec((tm, LANE), lambda i, j: (i, j)),
        out_shape=jax.ShapeDtypeStruct((m, 2 * LANE), F32),
        compiler_params=_cparams(("parallel", "arbitrary")),
        name="proj_small",
    )(x, w_t)


def _matmul_layers(x, w, tn):
    m, k = x.shape
    depth, _, n = w.shape
    return pl.pallas_call(
        _mm_kernel,
        grid=(depth, n // tn),
        in_specs=[pl.BlockSpec((m, k), lambda l, j: (0, 0)),
                  pl.BlockSpec((None, k, tn), lambda l, j: (l, 0, j))],
        out_specs=pl.BlockSpec((None, m, tn), lambda l, j: (l, 0, j)),
        out_shape=jax.ShapeDtypeStruct((depth, m, n), F32),
        compiler_params=_cparams(("parallel", "arbitrary")),
        name="mem_kv_matmul",
    )(x, w)


def _outproj_kernel(mix_ref, w_ref, x_ref, g_ref, b_ref, xo_ref, xb_ref, acc_ref, *, tn, nj):
    j = pl.program_id(1)
    acc_ref[j] = jnp.dot(mix_ref[...], w_ref[...], preferred_element_type=F32)

    @pl.when(j == nj - 1)
    def _():
        tm = acc_ref.shape[1]
        tot = jnp.zeros((tm, 1), F32)
        for jj in range(nj):
            y = DN_ALPHA * x_ref[:, jj * tn:(jj + 1) * tn] + acc_ref[jj]
            acc_ref[jj] = y
            tot = tot + jnp.sum(y, axis=1, keepdims=True)
        mean = tot * (1.0 / D_MODEL)
        sq = jnp.zeros((tm, 1), F32)
        for jj in range(nj):
            yc = acc_ref[jj] - mean
            sq = sq + jnp.sum(yc * yc, axis=1, keepdims=True)
        rstd = lax.rsqrt(sq * (1.0 / D_MODEL) + LN_EPS)
        for jj in range(nj):
            sl = slice(jj * tn, (jj + 1) * tn)
            out = (acc_ref[jj] - mean) * rstd * g_ref[:, sl] + b_ref[:, sl]
            xo_ref[:, sl] = out
            xb_ref[:, sl] = out.astype(BF16)


def _outproj_ln(mix, w_out, layer, x, ln_g, ln_b, tm, tn):
    m = x.shape[0]
    nj = D_MODEL // tn
    row_spec = pl.BlockSpec((None, 1, D_MODEL), lambda i, j: (layer, 0, 0))
    blk = pl.BlockSpec((tm, D_MODEL), lambda i, j: (i, 0))
    return pl.pallas_call(
        functools.partial(_outproj_kernel, tn=tn, nj=nj),
        grid=(m // tm, nj),
        in_specs=[blk, pl.BlockSpec((None, D_MODEL, tn), lambda i, j: (layer, 0, j)),
                  blk, row_spec, row_spec],
        out_specs=[blk, blk],
        out_shape=[jax.ShapeDtypeStruct((m, D_MODEL), F32), jax.ShapeDtypeStruct((m, D_MODEL), BF16)],
        scratch_shapes=[pltpu.VMEM((nj, tm, tn), F32)],
        compiler_params=_cparams(("parallel", "arbitrary")),
        name="outproj_ln",
    )(mix, w_out, x, ln_g, ln_b)


def _attn_kernel(q_ref, z_ref, k_ref, v_ref, _mix_ref, o_ref):
    lanes = [slice(h * XA_DH, (h + 1) * XA_DH) for h in range(XA_HEADS)]
    scores = [lax.dot_general(q_ref[:, sl].astype(BF16), k_ref[:, sl].astype(BF16), NT_DIMS,
                              preferred_element_type=F32) * (XA_DH ** -0.5)
              for sl in lanes]
    outs = []
    for sl, s in zip(lanes, scores):
        e = jnp.exp(s - jnp.max(s, axis=1, keepdims=True))
        pr = e / jnp.sum(e, axis=1, keepdims=True)
        outs.append(jnp.dot(pr.astype(BF16), v_ref[:, sl].astype(BF16), preferred_element_type=F32))
    for sl, o in zip(lanes, outs):
        o_ref[:, sl] = (o * _silu(z_ref[:, sl])).astype(BF16)


def _mem_attn(p, mem_k, mem_v, mix, layer, rows):
    pspec = lambda cb: rows.spec(GROUP_W, cb)
    kvspec = pl.BlockSpec((None, N_MEM, GROUP_W), lambda b, t: (layer, b, 0))
    return pl.pallas_call(
        _attn_kernel,
        grid=rows.grid,
        in_specs=[pspec(C_XQ // GROUP_W), pspec(C_XZ // GROUP_W), kvspec, kvspec, ANY_SPEC],
        out_specs=_mix_out_spec(rows, 3),
        out_shape=_sds(mix),
        input_output_aliases={4: 0},
        compiler_params=_cparams(("parallel", "arbitrary")),
        name="mem_attn",
    )(p, p, mem_k, mem_v, mix)


XA_SUB = XA_DH // LANE
XA_ROWS = N_MEM * XA_SUB * XA_HEADS


def _attn_cache_kernel(q_ref, z_ref, k_ref, v_ref, _mix_ref, o_ref, *, nb, te):
    t = te
    tiles = XA_HEADS * XA_SUB
    col = lax.broadcasted_iota(jnp.int32, (t, XA_ROWS), 1)
    col_head = col & (XA_HEADS - 1)
    col_sub = lax.shift_right_logical(col, 2) & (XA_SUB - 1)
    sub_all = lax.shift_right_logical(lax.broadcasted_iota(jnp.int32, (XA_HEADS * t, XA_ROWS), 1), 2) & (XA_SUB - 1)
    scores = []
    for e in range(nb):
        qr = slice(e * t, (e + 1) * t)
        qx = jnp.concatenate([q_ref[qr, c * LANE:(c + 1) * LANE] for c in range(tiles)], axis=0).astype(BF16)
        kx = k_ref[e * XA_ROWS:(e + 1) * XA_ROWS, :].astype(BF16)
        scores.append(lax.dot_general(qx, kx, NT_DIMS, preferred_element_type=F32))
    outs = []
    for e in range(nb):
        s_all = scores[e]
        probs = []
        for h in range(XA_HEADS):
            mine = col_head == h
            part = jnp.zeros((t, XA_ROWS), F32)
            for s in range(XA_SUB):
                rows = slice((h * XA_SUB + s) * t, (h * XA_SUB + s + 1) * t)
                part = part + jnp.where(mine & (col_sub == s), s_all[rows], 0.0)
            other = jnp.where(col_sub == 0, pltpu.roll(part, XA_ROWS - XA_HEADS, axis=1),
                              pltpu.roll(part, XA_HEADS, axis=1))
            sc = jnp.where(mine, (part + other) * (XA_DH ** -0.5), -jnp.inf)
            ex = jnp.exp(sc - jnp.max(sc, axis=1, keepdims=True))
            probs.append(ex / (jnp.sum(ex, axis=1, keepdims=True) * (1.0 / XA_SUB)))
        pr = jnp.concatenate(probs, axis=0)
        vb = v_ref[e * XA_ROWS:(e + 1) * XA_ROWS, :].astype(BF16)
        outs.append([jnp.dot(jnp.where(sub_all == s, pr, 0.0).astype(BF16), vb, preferred_element_type=F32)
                     for s in range(XA_SUB)])
    for e in range(nb):
        qr = slice(e * t, (e + 1) * t)
        for h in range(XA_HEADS):
            for s in range(XA_SUB):
                sl = slice((h * XA_SUB + s) * LANE, (h * XA_SUB + s + 1) * LANE)
                o_ref[qr, sl] = (outs[e][s][h * t:(h + 1) * t] * _silu(z_ref[qr, sl])).astype(BF16)


def _mem_attn_cache(p, cache_k, cache_v, mix, layer, rows):
    assert XA_SUB == 2 and XA_HEADS == 4 and rows.nt == 1
    pspec = lambda cb: rows.spec(GROUP_W, cb)
    kvspec = pl.BlockSpec((None, rows.nb * XA_ROWS, LANE), lambda g, t: (layer, g, 0))
    return pl.pallas_call(
        functools.partial(_attn_cache_kernel, nb=rows.nb, te=rows.te),
        grid=rows.grid,
        in_specs=[pspec(C_XQ // GROUP_W), pspec(C_XZ // GROUP_W), kvspec, kvspec, ANY_SPEC],
        out_specs=_mix_out_spec(rows, 3),
        out_shape=_sds(mix),
        input_output_aliases={4: 0},
        compiler_params=_cparams(("parallel", "arbitrary")),
        name="mem_attn_cache",
    )(p, p, cache_k, cache_v, mix)


def _hgrn_kernel(q_ref, f_ref, i_ref, z_ref, par_ref, s0_ref, _mix_ref, _acc_ref, o_ref, s_ref,
                 st_scr, b_scr, qs_scr, kk_scr, qt_scr, vb_scr, h_scr, *, chunk, nb, te, nt):
    t = pl.program_id(1)
    d = HG_D
    sub = 8

    @pl.when(t == 0)
    def _():
        for e in range(nb):
            for h in range(HG_HEADS):
                st_scr[e * HG_HEADS + h] = s0_ref[e, h].T

    log_lb = par_ref[0:1, :]
    log1m_lb = par_ref[1:2, :]
    one_m_lb = par_ref[2:3, :]
    norm_w = par_ref[3:4, :]

    fpre = f_ref[...]
    bt = log1m_lb + _log_sigmoid(fpre)
    logf = jnp.maximum(log_lb, bt) + jnp.log1p(jnp.exp(-jnp.abs(log_lb - bt)))
    b = _chunk_cumsum(logf, chunk)
    qs = _silu(q_ref[...])
    b_scr[...] = b
    qs_scr[...] = qs
    kk_scr[...] = one_m_lb * _sigmoid(-fpre)
    qt_scr[...] = (qs * jnp.exp(b)).astype(BF16)
    vb_scr[...] = i_ref[...].astype(BF16)

    rid = lax.broadcasted_iota(jnp.int32, (sub, d), 0)

    def intra(bc, qc, kc, vc):
        blocks = []
        for rb in range(chunk // sub):
            rs = slice(rb * sub, (rb + 1) * sub)
            bb, qb = bc[rs], qc[rs]
            o = jnp.zeros((sub, d), F32)
            for s in range((rb + 1) * sub):
                diff = bb - bc[s:s + 1, :]
                if s >= rb * sub:
                    diff = jnp.where(rid >= s - rb * sub, diff, -jnp.inf)
                a = jnp.sum(qb * kc[s:s + 1, :] * jnp.exp(diff), axis=1, keepdims=True)
                o = o + a * vc[s:s + 1, :]
            blocks.append(o)
        return blocks[0] if len(blocks) == 1 else jnp.concatenate(blocks, axis=0)

    def chunk_body(c, carry):
        items = range(nb * HG_HEADS)
        lanes = [slice((i % HG_HEADS) * d, (i % HG_HEADS + 1) * d) for i in items]
        starts = []
        for e in range(nb):
            r0 = e * te + c * chunk
            starts.append(r0 if isinstance(c, int) else pl.multiple_of(r0, chunk))
        rws = [pl.ds(starts[i // HG_HEADS], chunk) for i in items]
        inter, upd, decay = [], [], []
        for i in items:
            sl, rows = lanes[i], rws[i]
            bc = b_scr[rows, sl]
            b_last = bc[chunk - 1:chunk, :]
            kt = (kk_scr[rows, sl] * jnp.exp(b_last - bc)).astype(BF16)
            inter.append(lax.dot_general(qt_scr[rows, sl], st_scr[i].astype(BF16), NT_DIMS,
                                         preferred_element_type=F32))
            upd.append(lax.dot_general(vb_scr[rows, sl], kt, TN_DIMS, preferred_element_type=F32))
            decay.append(jnp.exp(b_last))
        for i in items:
            sl, rows = lanes[i], rws[i]
            o = intra(b_scr[rows, sl], qs_scr[rows, sl], kk_scr[rows, sl], i_ref[rows, sl])
            h_scr[rows, sl] = o + inter[i]
            st_scr[i] = st_scr[i] * decay[i] + upd[i]
        return carry

    nchunk = te // chunk
    if nchunk == 1:
        chunk_body(0, 0)
    else:
        lax.fori_loop(0, nchunk, chunk_body, 0)

    for h in range(HG_HEADS):
        sl = slice(h * d, (h + 1) * d)
        hh = h_scr[:, sl]
        ms = jnp.mean(hh * hh, axis=1, keepdims=True)
        y = hh * lax.rsqrt(ms + 1e-5) * norm_w[:, sl] * _silu(z_ref[:, sl])
        o_ref[:, sl] = y.astype(BF16)

    @pl.when(t == nt - 1)
    def _():
        for e in range(nb):
            for h in range(HG_HEADS):
                s_ref[e, h] = st_scr[e * HG_HEADS + h].T


ANY_SPEC = pl.BlockSpec(memory_space=pl.ANY)


def _state_in_spec(state, layer, tail, nb=1):
    zeros = (0,) * len(tail)
    if state.ndim == len(tail) + 2:
        return pl.BlockSpec((None, nb) + tail, lambda g, t: (layer, g) + zeros)
    return pl.BlockSpec((nb,) + tail, lambda g, t: (g,) + zeros)


def _state_out_spec(layer, tail, nb=1):
    zeros = (0,) * len(tail)
    return pl.BlockSpec((None, nb) + tail, lambda g, t: (layer, g) + zeros)


class _Rows:
    def __init__(self, row0, batch, seq, te, nb=1):
        self.nb, self.te, self.rows = nb, te, nb * te
        self.groups, self.nt = batch // nb, seq // te
        self.rb0 = row0 // self.rows
        self.grid = (self.groups, self.nt)

    def spec(self, width, col_block):
        groups, rb0 = self.groups, self.rb0
        return pl.BlockSpec((self.rows, width), lambda g, t: (rb0 + t * groups + g, col_block))


def _mix_out_spec(rows, group):
    return rows.spec(GROUP_W, group)


def _sds(x):
    return jax.ShapeDtypeStruct(x.shape, x.dtype)


def _hgrn(p, par, state, mix, acc, layer, rows, chunk):
    nb = rows.nb
    pspec = lambda cb: rows.spec(GROUP_W, cb)
    tail = (HG_HEADS, HG_D, HG_D)
    big = lambda dt: pltpu.VMEM((rows.rows, GROUP_W), dt)
    return pl.pallas_call(
        functools.partial(_hgrn_kernel, chunk=chunk, nb=nb, te=rows.te, nt=rows.nt),
        grid=rows.grid,
        in_specs=[pspec(C_HQ // GROUP_W), pspec(C_HF // GROUP_W), pspec(C_HI // GROUP_W), pspec(C_HZ // GROUP_W),
                  pl.BlockSpec((None, 8, GROUP_W), lambda b, t: (layer, 0, 0)),
                  _state_in_spec(state, layer, tail, nb), ANY_SPEC, ANY_SPEC],
        out_specs=[_mix_out_spec(rows, 0), _state_out_spec(layer, tail, nb)],
        out_shape=[_sds(mix), _sds(acc)],
        input_output_aliases={6: 0, 7: 1},
        scratch_shapes=[pltpu.VMEM((nb * HG_HEADS, HG_D, HG_D), F32),
                        big(F32), big(F32), big(F32), big(BF16), big(BF16), big(F32)],
        compiler_params=_cparams(("parallel", "arbitrary")),
        name="hgrn2",
    )(p, p, p, p, par, state, mix, acc)


def _mlstm_kernel(qk_ref, v_ref, z_ref, g_ref, par_ref, bias_ref, c0_ref, n0_ref, m0_ref,
                  _mix_ref, _acc_c_ref, _acc_n_ref, _acc_m_ref,
                  o_ref, c_ref, n_ref, m_ref, m_scr, h_scr, *, chunk, nb, te, nt):
    t = pl.program_id(1)
    L = chunk
    sub = 8

    @pl.when(t == 0)
    def _():
        c_ref[...] = c0_ref[...]
        n_ref[...] = n0_ref[...]
        m_scr[...] = jnp.zeros(m_scr.shape, F32)
        for e in range(nb):
            for h in range(ML_HEADS):
                m_scr[e * sub + h:e * sub + h + 1, :] = jnp.broadcast_to(m0_ref[e, :, h:h + 1], (1, LANE))

    r = lax.broadcasted_iota(jnp.int32, (L, L), 0)
    c = lax.broadcasted_iota(jnp.int32, (L, L), 1)
    tril = r >= c
    triu = r <= c

    items = [(e, h) for e in range(nb) for h in range(ML_HEADS)]
    nchunk = te // L
    pre = {}
    for ci in range(nchunk):
        for e in range(nb):
            rows = slice(e * te + ci * L, e * te + (ci + 1) * L)
            g = g_ref[rows, :]
            gt = g.T
            for h in range(ML_HEADS):
                ib = bias_ref[:, h:h + 1]
                fb = bias_ref[:, ML_HEADS + h:ML_HEADS + h + 1]
                i_col = g[:, h:h + 1] + ib
                i_row = gt[h:h + 1, 0:L] + ib
                lf_col = _log_sigmoid(g[:, ML_HEADS + h:ML_HEADS + h + 1] + fb)
                lf_row = _log_sigmoid(gt[ML_HEADS + h:ML_HEADS + h + 1, 0:L] + fb)
                b_col = jnp.sum(jnp.where(tril, lf_row, 0.0), axis=1, keepdims=True)
                b_row = jnp.sum(jnp.where(triu, lf_col, 0.0), axis=0, keepdims=True)
                dmat = jnp.where(tril, b_col - b_row + i_row, -jnp.inf)
                q = qk_ref[rows, h * ML_DK:(h + 1) * ML_DK]
                k = qk_ref[rows, ML_HEADS * ML_DK + h * ML_DK:ML_HEADS * ML_DK + (h + 1) * ML_DK] * (ML_DK ** -0.5)
                qb = q.astype(BF16)
                pre[ci, e, h] = dict(
                    rows=rows, i_col=i_col, b_col=b_col, dmat=dmat, dmax=jnp.max(dmat, axis=1, keepdims=True),
                    q=q, k=k, qb=qb, vb=v_ref[rows, h * ML_DV:(h + 1) * ML_DV].astype(BF16),
                    s_qk=lax.dot_general(qb, k.astype(BF16), NT_DIMS, preferred_element_type=F32))

    for ci in range(nchunk):
        mid = {}
        for e, h in items:
            a = pre[ci, e, h]
            m = m_scr[e * sub + h:e * sub + h + 1, 0:1]
            m_t = jnp.maximum(a["b_col"] + m, a["dmax"])
            pm = jnp.exp(a["dmat"] - m_t) * a["s_qk"]
            cst = c_ref[e, h]
            mid[e, h] = dict(m=m, m_t=m_t, pm=pm, cst=cst,
                             pv=jnp.dot(pm.astype(BF16), a["vb"], preferred_element_type=F32),
                             qc=jnp.dot(a["qb"], cst.astype(BF16), preferred_element_type=F32))
        for e, h in items:
            a, u = pre[ci, e, h], mid[e, h]
            m, m_t, b_col = u["m"], u["m_t"], a["b_col"]
            nst = n_ref[e, h:h + 1, :]
            inter = jnp.exp(b_col + m - m_t)
            num = u["pv"] + inter * u["qc"]
            den = jnp.sum(u["pm"], axis=1, keepdims=True) + inter * jnp.sum(a["q"] * nst, axis=1, keepdims=True)
            hh = num / jnp.maximum(jnp.abs(den), jnp.exp(-m_t))
            m_new = m_t[L - 1:L, :]
            b_last = b_col[L - 1:L, :]
            wgt = jnp.exp(b_last - b_col + a["i_col"] - m_new)
            decay = jnp.exp(b_last + m - m_new)
            kw = a["k"] * wgt
            c_ref[e, h] = decay * u["cst"] + lax.dot_general(kw.astype(BF16), a["vb"], TN_DIMS,
                                                             preferred_element_type=F32)
            n_ref[e, h:h + 1, :] = decay * nst + jnp.sum(kw, axis=0, keepdims=True)
            m_scr[e * sub + h:e * sub + h + 1, :] = jnp.broadcast_to(m_new, (1, LANE))
            h_scr[a["rows"], h * ML_DV:(h + 1) * ML_DV] = hh

    for h in range(ML_HEADS):
        sl = slice(h * ML_DV, (h + 1) * ML_DV)
        x = h_scr[:, sl]
        xc = x - jnp.mean(x, axis=1, keepdims=True)
        y = xc * lax.rsqrt(jnp.mean(xc * xc, axis=1, keepdims=True) + 1e-6)
        o_ref[:, sl] = (y * par_ref[:, sl] * _silu(z_ref[:, sl])).astype(BF16)

    @pl.when(t == nt - 1)
    def _():
        for e in range(nb):
            m_ref[e] = m_scr[e * sub:(e + 1) * sub, :]


def _mlstm(p, ps, norm_w, bias, c0, n0, m0, mix, acc_c, acc_n, acc_m, layer, rows, chunk):
    nb = rows.nb
    pspec = lambda cb: rows.spec(GROUP_W, cb)
    c_tail, n_tail, m_tail = (ML_HEADS, ML_DK, ML_DV), (ML_HEADS, ML_DK), (1, ML_HEADS)
    return pl.pallas_call(
        functools.partial(_mlstm_kernel, chunk=chunk, nb=nb, te=rows.te, nt=rows.nt),
        grid=rows.grid,
        in_specs=[pspec(C_MQK // GROUP_W), pspec(C_MV // GROUP_W), pspec(C_MZ // GROUP_W),
                  rows.spec(LANE, CB_MG),
                  pl.BlockSpec((None, 1, GROUP_W), lambda b, t: (layer, 0, 0)),
                  pl.BlockSpec((None, 1, 2 * ML_HEADS), lambda b, t: (layer, 0, 0)),
                  _state_in_spec(c0, layer, c_tail, nb), _state_in_spec(n0, layer, n_tail, nb),
                  _state_in_spec(m0, layer, m_tail, nb), ANY_SPEC, ANY_SPEC, ANY_SPEC, ANY_SPEC],
        out_specs=[_mix_out_spec(rows, 1), _state_out_spec(layer, c_tail, nb), _state_out_spec(layer, n_tail, nb),
                   _state_out_spec(layer, (8, LANE), nb)],
        out_shape=[_sds(mix), _sds(acc_c), _sds(acc_n), _sds(acc_m)],
        input_output_aliases={9: 0, 10: 1, 11: 2, 12: 3},
        scratch_shapes=[pltpu.VMEM((nb * 8, LANE), F32), pltpu.VMEM((rows.rows, GROUP_W), F32)],
        compiler_params=_cparams(("parallel", "arbitrary")),
        name="mlstm",
    )(p, p, p, ps, norm_w, bias, c0, n0, m0, mix, acc_c, acc_n, acc_m)


RW_CHUNK = 16


def _rwkv_kernel(r_ref, k_ref, v_ref, z_ref, x_ref, buf_ref, bufx_ref, mu_ref, mux_ref, par_ref,
                 w2_ref, a2_ref, s0_ref, _mix_ref, _acc_ref, o_ref, s_ref,
                 sp_scr, prev_scr, prevx_scr, kh_scr, rh_scr, ki_scr, ai_scr, kd_scr, ad_scr, vb_scr,
                 gl_scr, y_scr, bonus_scr, *, nb, te, nt):
    t = pl.program_id(1)
    L = RW_CHUNK
    W = GROUP_W
    tc = nb * te
    tep = max(te, L)
    tp = nb * tep

    lane2 = lax.broadcasted_iota(jnp.int32, (LANE, LANE), 1)
    row2 = lax.broadcasted_iota(jnp.int32, (LANE, LANE), 0)
    diag_blocks = (lane2 < RW_HEAD) == (row2 < RW_HEAD)
    seg_ones = jnp.where(diag_blocks, 1.0, 0.0).astype(BF16)

    @pl.when(t == 0)
    def _():
        zero = jnp.zeros((RW_HEAD, RW_HEAD), F32)
        for e in range(nb):
            prev_scr[e:e + 1, :] = buf_ref[e]
            prevx_scr[e:e + 1, :] = bufx_ref[e]
            for pr in range(RW_PAIRS):
                top = jnp.concatenate([s0_ref[e, 2 * pr], zero], axis=1)
                bot = jnp.concatenate([zero, s0_ref[e, 2 * pr + 1]], axis=1)
                sp_scr[e * RW_PAIRS + pr] = jnp.concatenate([top, bot], axis=0)

    def seg_sum(x):
        parts = [_dot_sel_right(x[:, i * LANE:(i + 1) * LANE], seg_ones, 2) for i in range(x.shape[1] // LANE)]
        return jnp.concatenate(parts, axis=1)

    def mix(cur, prev_rows, mu):
        rid = lax.broadcasted_iota(jnp.int32, cur.shape, 0)
        prev = pltpu.roll(cur, 1, axis=0) if tc > 1 else cur
        for e in range(nb):
            prev = jnp.where(rid == e * te, prev_rows[e:e + 1, :], prev)
        return cur + (prev - cur) * mu

    pr_ = r_ref[...]
    pk_ = k_ref[...]
    pv_ = v_ref[...]
    px_ = x_ref[...]
    xr = mix(pr_, prev_scr[:, 0:W], mu_ref[:, 0:W])
    xk = mix(pk_, prev_scr[:, W:2 * W], mu_ref[:, W:2 * W])
    xv = mix(pv_, prev_scr[:, 2 * W:3 * W], mu_ref[:, 2 * W:3 * W])
    xx = mix(px_, prevx_scr[...], mux_ref[...])
    for e in range(nb):
        last = slice((e + 1) * te - 1, (e + 1) * te)
        prev_scr[e:e + 1, 0:W] = pr_[last, :]
        prev_scr[e:e + 1, W:2 * W] = pk_[last, :]
        prev_scr[e:e + 1, 2 * W:3 * W] = pv_[last, :]
        prevx_scr[e:e + 1, :] = px_[last, :]

    w0 = par_ref[0:1, :]
    a0 = par_ref[1:2, :]
    k_k = par_ref[2:3, :]
    k_a = par_ref[3:4, :]
    r_k = par_ref[4:5, :]
    ln_w = par_ref[5:6, :]
    ln_b = par_ref[6:7, :]

    wlin = w0 + jnp.dot(jnp.tanh(xx).astype(BF16), w2_ref[...], preferred_element_type=F32)
    wdec = -_softplus(-wlin) - 0.5
    logd = -jnp.exp(wdec)
    a = _sigmoid(a0 + jnp.dot(xx.astype(BF16), a2_ref[...], preferred_element_type=F32))
    kk = xk * k_k
    kk = kk / jnp.maximum(jnp.sqrt(seg_sum(kk * kk)), 1e-12)
    kp = xk * (1.0 + (a - 1.0) * k_a)
    alpha = a * kk
    bonus = seg_sum(xr * kp * r_k) * xv

    if tep > te:
        def pad(u):
            zeros = jnp.zeros((tep - te, u.shape[1]), F32)
            return jnp.concatenate([piece for e in range(nb) for piece in (u[e * te:(e + 1) * te], zeros)], axis=0)
        logd, kk, kp, alpha, xr, xv = pad(logd), pad(kk), pad(kp), pad(alpha), pad(xr), pad(xv)
    bonus_scr[...] = bonus

    tri, ones = _chunk_masks(tp, L)
    g = _dot_sel_left(tri, logd, 3)
    gl = _dot_sel_left(ones, logd, 3)
    einv = jnp.exp(-g)
    egl = jnp.exp(gl - g)
    kh_scr[...] = (kk * jnp.exp(g - logd)).astype(BF16)
    rh_scr[...] = (xr * jnp.exp(g)).astype(BF16)
    ki_scr[...] = (kp * einv).astype(BF16)
    ai_scr[...] = (alpha * einv).astype(BF16)
    kd_scr[...] = (kp * egl).astype(BF16)
    ad_scr[...] = (alpha * egl).astype(BF16)
    vb_scr[...] = xv.astype(BF16)
    gl_scr[...] = jnp.exp(gl)

    lane_l = lax.broadcasted_iota(jnp.int32, (L, LANE), 1)
    head_a = lane_l < RW_HEAD
    rl = lax.broadcasted_iota(jnp.int32, (L, L), 0)
    cl = lax.broadcasted_iota(jnp.int32, (L, L), 1)
    incl = rl >= cl
    rl2 = lax.broadcasted_iota(jnp.int32, (2 * L, L), 0) & (L - 1)
    strict2 = rl2 > lax.broadcasted_iota(jnp.int32, (2 * L, L), 1)
    cl_ab = lax.broadcasted_iota(jnp.int32, (L, 2 * L), 1) & (L - 1)
    strict_ab = lax.broadcasted_iota(jnp.int32, (L, 2 * L), 0) > cl_ab
    sp_r = lax.broadcasted_iota(jnp.int32, (2 * L, (L - 1) * LANE), 0)
    sp_c = lax.broadcasted_iota(jnp.int32, (2 * L, (L - 1) * LANE), 1)
    sp_head = jnp.where((sp_c & (LANE - 1)) < RW_HEAD, 0, L)
    spread = jnp.where(sp_r == lax.shift_right_logical(sp_c, 7) + sp_head, 1.0, 0.0).astype(BF16)
    zb = jnp.zeros((L, LANE), BF16)

    def chunk_body(c, carry):
        pairs = range(nb * RW_PAIRS)
        lanes = [slice((i % RW_PAIRS) * LANE, (i % RW_PAIRS + 1) * LANE) for i in pairs]
        starts = []
        for e in range(nb):
            r0 = e * tep + c * L
            starts.append(r0 if isinstance(c, int) else pl.multiple_of(r0, L))
        rws = [pl.ds(starts[i // RW_PAIRS], L) for i in pairs]
        vb = [vb_scr[rws[i], lanes[i]] for i in pairs]
        gm, ks, nab = [], [], []
        for pr in pairs:
            sl, rows = lanes[pr], rws[pr]
            kh = kh_scr[rows, sl]
            rh = rh_scr[rows, sl]
            x4 = jnp.concatenate([jnp.where(head_a, kh, zb), jnp.where(head_a, zb, kh),
                                  jnp.where(head_a, rh, zb), jnp.where(head_a, zb, rh)], axis=0)
            ai = ai_scr[rows, sl]
            y2 = jnp.concatenate([ai, ki_scr[rows, sl]], axis=0)
            gm.append(lax.dot_general(x4, y2, NT_DIMS, preferred_element_type=F32))
            ai2 = jnp.concatenate([jnp.where(head_a, ai, zb), jnp.where(head_a, zb, ai)], axis=0)
            nab.append(lax.dot_general(kh, ai2, NT_DIMS, preferred_element_type=F32))
            ks.append(lax.dot_general(jnp.concatenate([kh, rh], axis=0), sp_scr[pr].astype(BF16), NT_DIMS,
                                      preferred_element_type=F32))
        mv, coef = [], []
        for pr in pairs:
            g = gm[pr]
            m_ab = jnp.where(strict2, g[0:2 * L, L:2 * L], 0.0)
            mv.append(jnp.dot(m_ab.astype(BF16), vb[pr], preferred_element_type=F32))
            coef.append(_dot_sel_right(jnp.where(strict_ab, nab[pr], 0.0), spread, 1))
        vw = []
        for pr in pairs:
            w = ks[pr][0:L] + jnp.where(head_a, mv[pr][0:L], mv[pr][L:2 * L])
            for s in range(L - 1):
                w = w - coef[pr][:, s * LANE:(s + 1) * LANE] * w[s:s + 1, :]
            vw.append(jnp.concatenate([vb[pr], w.astype(BF16)], axis=0))
        yy, upd = [], []
        for pr in pairs:
            g = gm[pr]
            cm = jnp.concatenate([
                jnp.concatenate([jnp.where(incl, g[2 * L:3 * L, L:2 * L], 0.0),
                                 -jnp.where(incl, g[2 * L:3 * L, 0:L], 0.0)], axis=1),
                jnp.concatenate([jnp.where(incl, g[3 * L:4 * L, L:2 * L], 0.0),
                                 -jnp.where(incl, g[3 * L:4 * L, 0:L], 0.0)], axis=1)], axis=0)
            yy.append(jnp.dot(cm.astype(BF16), vw[pr], preferred_element_type=F32))
        grp = LANE // (2 * L)
        zblk = jnp.zeros((2 * L, LANE), BF16)
        for g0 in range(0, nb * RW_PAIRS, grp):
            members = range(g0, g0 + grp)
            vw_t = jnp.concatenate([vw[pr] for pr in members], axis=0).T
            kd_rows = []
            for q, pr in enumerate(members):
                sl, rows = lanes[pr], rws[pr]
                kd_ad = jnp.concatenate([kd_scr[rows, sl], -ad_scr[rows, sl]], axis=0)
                kd_rows.append(jnp.concatenate([kd_ad if col == q else zblk for col in range(grp)], axis=1))
            u_all = jnp.dot(vw_t, jnp.concatenate(kd_rows, axis=0), preferred_element_type=F32)
            for q in range(grp):
                upd.append(u_all[:, q * LANE:(q + 1) * LANE])
        for pr in pairs:
            sl, rows = lanes[pr], rws[pr]
            y_scr[rows, sl] = ks[pr][L:2 * L] + jnp.where(head_a, yy[pr][0:L], yy[pr][L:2 * L])
            gl_row = gl_scr[pl.ds(starts[pr // RW_PAIRS], 1), sl]
            sp_scr[pr] = sp_scr[pr] * gl_row + jnp.where(diag_blocks, upd[pr], 0.0)
        return carry

    nchunk = tep // L
    if nchunk == 1:
        chunk_body(0, 0)
    else:
        lax.fori_loop(0, nchunk, chunk_body, 0)

    if tep > te:
        y = jnp.concatenate([y_scr[e * tep:e * tep + te, :] for e in range(nb)], axis=0)
    else:
        y = y_scr[...]
    yc = y - seg_sum(y) * (1.0 / RW_HEAD)
    yn = yc * lax.rsqrt(seg_sum(yc * yc) * (1.0 / RW_HEAD) + RW_GN_EPS)
    out = (yn * ln_w + ln_b + bonus_scr[...]) * _silu(z_ref[...])
    o_ref[...] = out.astype(BF16)

    @pl.when(t == nt - 1)
    def _():
        for e in range(nb):
            for pr in range(RW_PAIRS):
                sp = sp_scr[e * RW_PAIRS + pr]
                s_ref[e, 2 * pr] = sp[0:RW_HEAD, 0:RW_HEAD]
                s_ref[e, 2 * pr + 1] = sp[RW_HEAD:LANE, RW_HEAD:LANE]


def _rwkv(p, ps, buf, bufx, mu, mux, par, w2p, a2p, state, mix, acc, layer, rows):
    nb, te = rows.nb, rows.te
    tp = nb * max(te, RW_CHUNK)
    pspec = lambda cb: rows.spec(GROUP_W, cb)
    tail = (RW_HEADS, RW_HEAD, RW_HEAD)
    lay = lambda shape: pl.BlockSpec((None,) + shape, lambda b, t: (layer,) + (0,) * len(shape))
    big = lambda dt: pltpu.VMEM((tp, GROUP_W), dt)
    return pl.pallas_call(
        functools.partial(_rwkv_kernel, nb=nb, te=te, nt=rows.nt),
        grid=rows.grid,
        in_specs=[pspec(C_RR // GROUP_W), pspec(C_RK // GROUP_W), pspec(C_RV // GROUP_W), pspec(C_RZ // GROUP_W),
                  rows.spec(LANE, CB_RX),
                  pl.BlockSpec((nb, 1, 3 * GROUP_W), lambda g, t: (g, 0, 0)),
                  pl.BlockSpec((nb, 1, LANE), lambda g, t: (g, 0, 0)),
                  lay((1, 3 * GROUP_W)), lay((1, LANE)), lay((8, GROUP_W)),
                  lay((LANE, GROUP_W)), lay((LANE, GROUP_W)), _state_in_spec(state, layer, tail, nb),
                  ANY_SPEC, ANY_SPEC],
        out_specs=[_mix_out_spec(rows, 2), _state_out_spec(layer, tail, nb)],
        out_shape=[_sds(mix), _sds(acc)],
        input_output_aliases={13: 0, 14: 1},
        scratch_shapes=[pltpu.VMEM((nb * RW_PAIRS, LANE, LANE), F32),
                        pltpu.VMEM((nb, 3 * GROUP_W), F32), pltpu.VMEM((nb, LANE), F32),
                        big(BF16), big(BF16), big(BF16), big(BF16), big(BF16), big(BF16), big(BF16),
                        big(F32), big(F32), pltpu.VMEM((rows.rows, GROUP_W), F32)],
        compiler_params=_cparams(("parallel", "arbitrary")),
        name="rwkv7",
    )(p, p, p, p, ps, buf, bufx, mu, mux, par, w2p, a2p, state, mix, acc)


def _trunk_layer(l, x_f32, x_bf, w_in_t, w_out_b, hg_par, ml_nw, ml_bias, rw_mu, rw_mux, rw_par, rw_w2, rw_a2,
                 ln_g, ln_b, groups, cfg):
    p = _proj_matmul(x_bf, w_in_t, l, cfg["mm_tm"], cfg["mm_tn"])
    ps = _proj_small(x_bf, w_in_t, l, cfg["mm_tm"])
    mix = jnp.zeros(x_bf.shape, BF16)
    for grp in groups:
        row0, batch, seq, te = grp["row0"], grp["batch"], grp["seq"], grp["te"]
        rows = lambda nb: _Rows(row0, batch, seq, te, nb)
        mix, grp["acc_hg"] = _hgrn(p, hg_par, grp["hg"], mix, grp["acc_hg"], l, rows(grp["hg_nb"]), grp["hg_chunk"])
        mix, grp["acc_c"], grp["acc_n"], grp["acc_m"] = _mlstm(
            p, ps, ml_nw, ml_bias, grp["ml_c"], grp["ml_n"], grp["ml_m"], mix, grp["acc_c"], grp["acc_n"],
            grp["acc_m"], l, rows(grp["ml_nb"]), grp["ml_chunk"])
        mix, grp["acc_rw"] = _rwkv(p, ps, grp["rw_buf"][l], grp["rw_bufx"][l], rw_mu, rw_mux, rw_par, rw_w2, rw_a2,
                                   grp["rw"], mix, grp["acc_rw"], l, rows(grp["rw_nb"]))
        if grp["cache_layout"]:
            mix = _mem_attn_cache(p, grp["mem_k"], grp["mem_v"], mix, l, rows(grp["xa_nb"]))
        else:
            mix = _mem_attn(p, grp["mem_k"], grp["mem_v"], mix, l, rows(1))
        first = row0 + (seq // te - 1) * batch * te + te - 1
        last_rkv = lax.slice(p, (first, C_RR), (row0 + batch * seq, C_RR + 3 * GROUP_W), (te, 1))
        last_x = lax.slice(ps, (first, CB_RX * LANE), (row0 + batch * seq, (CB_RX + 1) * LANE), (te, 1))
        grp["buf_new"].append(jnp.concatenate([last_rkv, last_x], axis=-1))
    x_f32, x_bf = _outproj_ln(mix, w_out_b, l, x_f32, ln_g, ln_b, cfg["op_tm"], cfg["op_tn"])
    return x_f32, x_bf


def kernel(x_prompt, x_sample, mem_prompt, state_hgrn, state_mlstm_C, state_mlstm_n, state_mlstm_m, state_rwkv, state_rwkv_shift, cache_mem_k, cache_mem_v, w_in, hgrn_lb, hgrn_norm_w, mlstm_ig_b, mlstm_fg_b, mlstm_norm_w, rwkv_mu, rwkv_w0, rwkv_w2, rwkv_a0, rwkv_a2, rwkv_k_k, rwkv_k_a, rwkv_r_k, rwkv_ln_w, rwkv_ln_b, mem_wk, mem_wv, w_out, ln_g, ln_b):
    bp, tp_, _ = x_prompt.shape
    bs, ts, _ = x_sample.shape
    depth = w_in.shape[0]
    mp, ms = bp * tp_, bs * ts

    w_in_t = jnp.swapaxes(w_in, 1, 2)
    w_out_b = w_out.astype(BF16)
    lb_all = jnp.cumsum(jax.nn.softmax(hgrn_lb.astype(F32), axis=0), axis=0)
    lb_all = lb_all - lb_all[0]
    zrow = jnp.zeros_like(lb_all)
    hg_par = jnp.stack([jnp.log(lb_all), jnp.log1p(-lb_all), 1.0 - lb_all, hgrn_norm_w.astype(F32),
                        zrow, zrow, zrow, zrow], axis=1)
    ml_nw = mlstm_norm_w.astype(F32)[:, None, :]
    ml_bias = jnp.concatenate([mlstm_ig_b, mlstm_fg_b], axis=-1).astype(F32)[:, None, :]
    rw_mu = rwkv_mu[:, None, :3 * GROUP_W].astype(F32)
    rw_mux = rwkv_mu[:, None, 3 * GROUP_W:].astype(F32)
    zr = jnp.zeros((depth, GROUP_W), F32)
    rw_par = jnp.stack([rwkv_w0, rwkv_a0, rwkv_k_k, rwkv_k_a, rwkv_r_k.reshape(depth, GROUP_W), rwkv_ln_w,
                        rwkv_ln_b, zr], axis=1).astype(F32)
    zl = jnp.zeros((depth, RW_LORA, GROUP_W), F32)
    rw_w2 = jnp.concatenate([rwkv_w2.astype(F32), zl], axis=1).astype(BF16)
    rw_a2 = jnp.concatenate([zl, rwkv_a2.astype(F32)], axis=1).astype(BF16)
    ln_g3 = ln_g.astype(F32)[:, None, :]
    ln_b3 = ln_b.astype(F32)[:, None, :]

    mem_x = mem_prompt.reshape(bp * N_MEM, D_MODEL).astype(BF16)
    mk_p = _matmul_layers(mem_x, mem_wk, 256)
    mv_p = _matmul_layers(mem_x, mem_wv, 256)
    mk_out = mk_p.reshape(depth, bp, N_MEM, XA_HEADS, XA_DH)
    mv_out = mv_p.reshape(depth, bp, N_MEM, XA_HEADS, XA_DH)

    def split_buf(buf):
        return buf[:, :, None, :3 * GROUP_W].astype(F32), buf[:, :, None, 3 * GROUP_W:].astype(F32)

    def cache_view(c):
        c = c.reshape(depth, bs, N_MEM, XA_HEADS, XA_SUB, LANE)
        return jnp.transpose(c, (0, 1, 2, 4, 3, 5)).reshape(depth, bs * XA_ROWS, LANE)

    def results(b):
        return dict(acc_hg=jnp.zeros((depth, b, HG_HEADS, HG_D, HG_D), F32),
                    acc_c=jnp.zeros((depth, b, ML_HEADS, ML_DK, ML_DV), F32),
                    acc_n=jnp.zeros((depth, b, ML_HEADS, ML_DK), F32),
                    acc_m=jnp.zeros((depth, b, 8, LANE), F32),
                    acc_rw=jnp.zeros((depth, b, RW_HEADS, RW_HEAD, RW_HEAD), F32), buf_new=[])

    zbuf, zbufx = split_buf(jnp.zeros((depth, bp, RW_SHIFT_W), F32))
    sbuf, sbufx = split_buf(state_rwkv_shift)
    te_p = 128
    prompt = dict(row0=0, batch=bp, seq=tp_, te=te_p, hg_nb=1, ml_nb=2, xa_nb=1,
                  hg=jnp.zeros((bp, HG_HEADS, HG_D, HG_D), F32), hg_chunk=16,
                  ml_c=jnp.zeros((bp, ML_HEADS, ML_DK, ML_DV), F32), ml_n=jnp.zeros((bp, ML_HEADS, ML_DK), F32),
                  ml_m=jnp.zeros((bp, 1, ML_HEADS), F32), ml_chunk=64,
                  rw=jnp.zeros((bp, RW_HEADS, RW_HEAD, RW_HEAD), F32), rw_buf=zbuf, rw_bufx=zbufx, rw_nb=2,
                  mem_k=mk_p, mem_v=mv_p, cache_layout=False, **results(bp))
    sample = dict(row0=mp, batch=bs, seq=ts, te=ts, hg_nb=4, ml_nb=4, xa_nb=2,
                  hg=state_hgrn, hg_chunk=ts,
                  ml_c=state_mlstm_C, ml_n=state_mlstm_n, ml_m=state_mlstm_m[:, :, None, :], ml_chunk=ts,
                  rw=state_rwkv, rw_buf=sbuf, rw_bufx=sbufx, rw_nb=4,
                  mem_k=cache_view(cache_mem_k), mem_v=cache_view(cache_mem_v), cache_layout=True, **results(bs))
    cfg = dict(mm_tm=1536, mm_tn=512, op_tm=384, op_tn=512)

    nblk = tp_ // te_p
    xp = jnp.transpose(x_prompt.reshape(bp, nblk, te_p, D_MODEL), (1, 0, 2, 3)).reshape(mp, D_MODEL)
    x_f32 = jnp.concatenate([xp, x_sample.reshape(ms, D_MODEL)], axis=0).astype(F32)
    x_bf = x_f32.astype(BF16)
    for l in range(depth):
        x_f32, x_bf = _trunk_layer(l, x_f32, x_bf, w_in_t, w_out_b, hg_par, ml_nw, ml_bias, rw_mu, rw_mux,
                                   rw_par, rw_w2, rw_a2, ln_g3, ln_b3, [prompt, sample], cfg)

    def states(g):
        return (g["acc_hg"], g["acc_c"], g["acc_n"], g["acc_m"][:, :, :ML_HEADS, 0], g["acc_rw"],
                jnp.stack(g["buf_new"], axis=0))

    y_prompt = jnp.transpose(x_f32[:mp].reshape(nblk, bp, te_p, D_MODEL), (1, 0, 2, 3)).reshape(bp, tp_, D_MODEL)
    y_sample = x_f32[mp:].reshape(bs, ts, D_MODEL)
    return (y_prompt, y_sample) + states(prompt) + (mk_out, mv_out) + states(sample)
```

```python
import functools
import math

import jax
import jax.numpy as jnp
from jax import lax
from jax.experimental import pallas as pl
from jax.experimental.pallas import tpu as pltpu

F32 = jnp.float32
BF16 = jnp.bfloat16
HI = lax.Precision.HIGHEST

D_MODEL = 4096
DEPTH = 4
GROUP_W = D_MODEL // 4
N_MEM = 256
HG_HEADS, HG_D = 8, 128
ML_HEADS, ML_DK, ML_DV = 4, 128, 256
RW_HEADS, RW_HEAD, RW_LORA = 16, 64, 64
RW_PAIRS = RW_HEADS // 2
XA_HEADS, XA_DH = 4, 256
RW_SHIFT_W = 3 * GROUP_W + 2 * RW_LORA
N_IN = 13448
DN_ALPHA = (2.0 * DEPTH) ** 0.25
LN_EPS = 1e-5
RW_GN_EPS = 64e-5

C_HQ, C_HF, C_HI, C_HZ = 0, 1024, 2048, 3072
C_MQK, C_MV, C_MZ = 4096, 5120, 6144
C_RR, C_RK, C_RV, C_RZ = 7168, 8192, 9216, 10240
C_XQ, C_XZ = 11264, 12288
NP_MAIN = 13312
ORIG_MG, ORIG_MZ, ORIG_RX, ORIG_RZ = 6144, 6152, 10248, 10376
CB_RX, CB_MG = 0, 1
LANE = 128

VMEM_LIMIT = 60 * 1024 * 1024

NT_DIMS = (((1,), (1,)), ((), ()))
TN_DIMS = (((0,), (0,)), ((), ()))


def _cparams(sem):
    return pltpu.CompilerParams(dimension_semantics=sem, vmem_limit_bytes=VMEM_LIMIT)


def _sigmoid(x):
    return jax.nn.sigmoid(x)


def _silu(x):
    return x * _sigmoid(x)


def _log_sigmoid(x):
    return jnp.minimum(x, 0.0) - jnp.log1p(jnp.exp(-jnp.abs(x)))


def _softplus(x):
    return jnp.maximum(x, 0.0) + jnp.log1p(jnp.exp(-jnp.abs(x)))


def _chunk_masks(n, chunk):
    r = lax.broadcasted_iota(jnp.int32, (n, n), 0)
    c = lax.broadcasted_iota(jnp.int32, (n, n), 1)
    sh = int(math.log2(chunk))
    same = lax.shift_right_logical(r, sh) == lax.shift_right_logical(c, sh)
    tri = jnp.where(same & (c <= r), 1.0, 0.0).astype(BF16)
    ones = jnp.where(same, 1.0, 0.0).astype(BF16)
    return tri, ones


def _split_bf16(x, terms):
    parts = []
    for i in range(terms):
        part = x.astype(BF16)
        parts.append(part)
        if i + 1 < terms:
            x = x - part.astype(F32)
    return parts


def _dot_sel_left(sel, x, terms):
    return sum(jnp.dot(sel, part, preferred_element_type=F32) for part in _split_bf16(x, terms))


def _dot_sel_right(x, sel, terms):
    return sum(jnp.dot(part, sel, preferred_element_type=F32) for part in _split_bf16(x, terms))


def _chunk_cumsum(x, chunk):
    rid = lax.broadcasted_iota(jnp.int32, x.shape, 0) & (chunk - 1)
    step = 1
    while step < chunk:
        x = x + jnp.where(rid >= step, pltpu.roll(x, step, axis=0), 0.0)
        step *= 2
    return x


def _mm_kernel(x_ref, w_ref, o_ref):
    o_ref[...] = jnp.dot(x_ref[...], w_ref[...].astype(BF16), preferred_element_type=F32)


def _mm_nt_kernel(x_ref, w_ref, o_ref):
    o_ref[...] = lax.dot_general(x_ref[...], w_ref[0].astype(BF16), NT_DIMS, preferred_element_type=F32)


def _proj_matmul(x, w_t, layer, tm, tn):
    m, k = x.shape

    def w_index(i, j):
        col = j * tn
        off = jnp.where(col >= C_RZ, ORIG_RZ - C_RZ, jnp.where(col >= C_MZ, ORIG_MZ - C_MZ, 0))
        return (layer, pl.multiple_of(col + off, 8), 0)

    return pl.pallas_call(
        _mm_nt_kernel,
        grid=(m // tm, NP_MAIN // tn),
        in_specs=[pl.BlockSpec((tm, k), lambda i, j: (i, 0)),
                  pl.BlockSpec((pl.Element(1), pl.Element(tn), pl.Element(k)), w_index)],
        out_specs=pl.BlockSpec((tm, tn), lambda i, j: (i, j)),
        out_shape=jax.ShapeDtypeStruct((m, NP_MAIN), F32),
        compiler_params=_cparams(("parallel", "arbitrary")),
        name="proj_matmul",
    )(x, w_t)


def _proj_small(x, w_t, layer, tm):
    m, k = x.shape

    def kernel(x_ref, wa_ref, wb_ref, o_ref):
        w = jnp.concatenate([wa_ref[0], wb_ref[0]], axis=0).astype(BF16)
        o_ref[...] = lax.dot_general(x_ref[...], w, NT_DIMS, preferred_element_type=F32)

    wspec = lambda row: pl.BlockSpec((pl.Element(1), pl.Element(LANE), pl.Element(k)), lambda i: (layer, row, 0))
    return pl.pallas_call(
        kernel,
        grid=(m // tm,),
        in_specs=[pl.BlockSpec((tm, k), lambda i: (i, 0)), wspec(ORIG_RX), wspec(ORIG_MG)],
        out_specs=pl.BlockSpec((tm, 2 * LANE), lambda i: (i, 0)),
        out_shape=jax.ShapeDtypeStruct((m, 2 * LANE), F32),
        compiler_params=_cparams(("parallel",)),
        name="proj_small",
    )(x, w_t, w_t)


def _matmul_layers(x, w, tn):
    m, k = x.shape
    depth, _, n = w.shape
    return pl.pallas_call(
        _mm_kernel,
        grid=(depth, n // tn),
        in_specs=[pl.BlockSpec((m, k), lambda l, j: (0, 0)),
                  pl.BlockSpec((None, k, tn), lambda l, j: (l, 0, j))],
        out_specs=pl.BlockSpec((None, m, tn), lambda l, j: (l, 0, j)),
        out_shape=jax.ShapeDtypeStruct((depth, m, n), F32),
        compiler_params=_cparams(("parallel", "arbitrary")),
        name="mem_kv_matmul",
    )(x, w)


def _outproj_kernel(mix_ref, w_ref, x_ref, g_ref, b_ref, xo_ref, xb_ref, acc_ref, *, tn, nj):
    j = pl.program_id(1)
    acc_ref[j] = jnp.dot(mix_ref[...], w_ref[...], preferred_element_type=F32)

    @pl.when(j == nj - 1)
    def _():
        tm = acc_ref.shape[1]
        tot = jnp.zeros((tm, 1), F32)
        for jj in range(nj):
            y = DN_ALPHA * x_ref[:, jj * tn:(jj + 1) * tn] + acc_ref[jj]
            acc_ref[jj] = y
            tot = tot + jnp.sum(y, axis=1, keepdims=True)
        mean = tot * (1.0 / D_MODEL)
        sq = jnp.zeros((tm, 1), F32)
        for jj in range(nj):
            yc = acc_ref[jj] - mean
            sq = sq + jnp.sum(yc * yc, axis=1, keepdims=True)
        rstd = lax.rsqrt(sq * (1.0 / D_MODEL) + LN_EPS)
        for jj in range(nj):
            sl = slice(jj * tn, (jj + 1) * tn)
            out = (acc_ref[jj] - mean) * rstd * g_ref[:, sl] + b_ref[:, sl]
            xo_ref[:, sl] = out
            xb_ref[:, sl] = out.astype(BF16)


def _outproj_ln(mix, w_out, layer, x, ln_g, ln_b, tm, tn):
    m = x.shape[0]
    nj = D_MODEL // tn
    row_spec = pl.BlockSpec((None, 1, D_MODEL), lambda i, j: (layer, 0, 0))
    blk = pl.BlockSpec((tm, D_MODEL), lambda i, j: (i, 0))
    return pl.pallas_call(
        functools.partial(_outproj_kernel, tn=tn, nj=nj),
        grid=(m // tm, nj),
        in_specs=[blk, pl.BlockSpec((None, D_MODEL, tn), lambda i, j: (layer, 0, j)),
                  blk, row_spec, row_spec],
        out_specs=[blk, blk],
        out_shape=[jax.ShapeDtypeStruct((m, D_MODEL), F32), jax.ShapeDtypeStruct((m, D_MODEL), BF16)],
        scratch_shapes=[pltpu.VMEM((nj, tm, tn), F32)],
        compiler_params=_cparams(("parallel", "arbitrary")),
        name="outproj_ln",
    )(mix, w_out, x, ln_g, ln_b)


def _attn_kernel(q_ref, z_ref, k_ref, v_ref, _mix_ref, o_ref, *, nb, te):
    items = [(slice(e * te, (e + 1) * te), slice(e * N_MEM, (e + 1) * N_MEM), slice(h * XA_DH, (h + 1) * XA_DH))
             for e in range(nb) for h in range(XA_HEADS)]
    scores = [lax.dot_general(q_ref[qr, sl].astype(BF16), k_ref[mr, sl].astype(BF16), NT_DIMS,
                              preferred_element_type=F32) * (XA_DH ** -0.5)
              for qr, mr, sl in items]
    outs = []
    for (qr, mr, sl), s in zip(items, scores):
        ex = jnp.exp(s - jnp.max(s, axis=1, keepdims=True))
        pr = ex / jnp.sum(ex, axis=1, keepdims=True)
        outs.append(jnp.dot(pr.astype(BF16), v_ref[mr, sl].astype(BF16), preferred_element_type=F32))
    for (qr, mr, sl), o in zip(items, outs):
        o_ref[qr, sl] = (o * _silu(z_ref[qr, sl])).astype(BF16)


def _mem_attn(p, mem_k, mem_v, mix, layer, rows):
    pspec = lambda cb: rows.spec(GROUP_W, cb)
    kvspec = pl.BlockSpec((None, rows.nb * N_MEM, GROUP_W), lambda g, t: (layer, g, 0))
    return pl.pallas_call(
        functools.partial(_attn_kernel, nb=rows.nb, te=rows.te),
        grid=rows.grid,
        in_specs=[pspec(C_XQ // GROUP_W), pspec(C_XZ // GROUP_W), kvspec, kvspec, ANY_SPEC],
        out_specs=_mix_out_spec(rows, 3),
        out_shape=_sds(mix),
        input_output_aliases={4: 0},
        compiler_params=_cparams(("parallel", "arbitrary")),
        name="mem_attn",
    )(p, p, mem_k, mem_v, mix)


XA_SUB = XA_DH // LANE
XA_ROWS = N_MEM * XA_SUB * XA_HEADS


def _attn_cache_kernel(q_ref, z_ref, k_ref, v_ref, _mix_ref, o_ref, *, nb, te):
    t = te
    tiles = XA_HEADS * XA_SUB
    col = lax.broadcasted_iota(jnp.int32, (t, XA_ROWS), 1)
    col_head = col & (XA_HEADS - 1)
    col_sub = lax.shift_right_logical(col, 2) & (XA_SUB - 1)
    sub_all = lax.shift_right_logical(lax.broadcasted_iota(jnp.int32, (XA_HEADS * t, XA_ROWS), 1), 2) & (XA_SUB - 1)
    scores = []
    for e in range(nb):
        qr = slice(e * t, (e + 1) * t)
        qx = jnp.concatenate([q_ref[qr, c * LANE:(c + 1) * LANE] for c in range(tiles)], axis=0).astype(BF16)
        kx = k_ref[e * XA_ROWS:(e + 1) * XA_ROWS, :].astype(BF16)
        scores.append(lax.dot_general(qx, kx, NT_DIMS, preferred_element_type=F32))
    outs = []
    for e in range(nb):
        s_all = scores[e]
        probs = []
        for h in range(XA_HEADS):
            mine = col_head == h
            part = jnp.zeros((t, XA_ROWS), F32)
            for s in range(XA_SUB):
                rows = slice((h * XA_SUB + s) * t, (h * XA_SUB + s + 1) * t)
                part = part + jnp.where(mine & (col_sub == s), s_all[rows], 0.0)
            other = jnp.where(col_sub == 0, pltpu.roll(part, XA_ROWS - XA_HEADS, axis=1),
                              pltpu.roll(part, XA_HEADS, axis=1))
            sc = jnp.where(mine, (part + other) * (XA_DH ** -0.5), -jnp.inf)
            ex = jnp.exp(sc - jnp.max(sc, axis=1, keepdims=True))
            probs.append(ex / (jnp.sum(ex, axis=1, keepdims=True) * (1.0 / XA_SUB)))
        pr = jnp.concatenate(probs, axis=0)
        vb = v_ref[e * XA_ROWS:(e + 1) * XA_ROWS, :].astype(BF16)
        outs.append([jnp.dot(jnp.where(sub_all == s, pr, 0.0).astype(BF16), vb, preferred_element_type=F32)
                     for s in range(XA_SUB)])
    for e in range(nb):
        qr = slice(e * t, (e + 1) * t)
        for h in range(XA_HEADS):
            for s in range(XA_SUB):
                sl = slice((h * XA_SUB + s) * LANE, (h * XA_SUB + s + 1) * LANE)
                o_ref[qr, sl] = (outs[e][s][h * t:(h + 1) * t] * _silu(z_ref[qr, sl])).astype(BF16)


def _mem_attn_cache(p, cache_k, cache_v, mix, layer, rows):
    assert XA_SUB == 2 and XA_HEADS == 4 and rows.nt == 1
    pspec = lambda cb: rows.spec(GROUP_W, cb)
    kvspec = pl.BlockSpec((None, rows.nb * XA_ROWS, LANE), lambda g, t: (layer, g, 0))
    return pl.pallas_call(
        functools.partial(_attn_cache_kernel, nb=rows.nb, te=rows.te),
        grid=rows.grid,
        in_specs=[pspec(C_XQ // GROUP_W), pspec(C_XZ // GROUP_W), kvspec, kvspec, ANY_SPEC],
        out_specs=_mix_out_spec(rows, 3),
        out_shape=_sds(mix),
        input_output_aliases={4: 0},
        compiler_params=_cparams(("parallel", "arbitrary")),
        name="mem_attn_cache",
    )(p, p, cache_k, cache_v, mix)


def _hgrn_kernel(q_ref, f_ref, i_ref, z_ref, par_ref, s0_ref, _mix_ref, _acc_ref, o_ref, s_ref,
                 st_scr, b_scr, qs_scr, kk_scr, qt_scr, vb_scr, h_scr, *, chunk, nb, te, nt):
    t = pl.program_id(1)
    d = HG_D
    sub = 8
    transposed = nt > 1

    @pl.when(t == 0)
    def _():
        for e in range(nb):
            for h in range(HG_HEADS):
                st_scr[e * HG_HEADS + h] = s0_ref[e, h].T if transposed else s0_ref[e, h]

    log_lb = par_ref[0:1, :]
    log1m_lb = par_ref[1:2, :]
    one_m_lb = par_ref[2:3, :]
    norm_w = par_ref[3:4, :]

    fpre = f_ref[...]
    bt = log1m_lb + _log_sigmoid(fpre)
    logf = jnp.maximum(log_lb, bt) + jnp.log1p(jnp.exp(-jnp.abs(log_lb - bt)))
    b = _chunk_cumsum(logf, chunk)
    qs = _silu(q_ref[...])
    b_scr[...] = b
    qs_scr[...] = qs
    kk_scr[...] = one_m_lb * _sigmoid(-fpre)
    qt_scr[...] = (qs * jnp.exp(b)).astype(BF16)
    vb_scr[...] = i_ref[...].astype(BF16)

    rid = lax.broadcasted_iota(jnp.int32, (sub, d), 0)

    def intra(bc, qc, kc, vc):
        blocks = []
        for rb in range(chunk // sub):
            rs = slice(rb * sub, (rb + 1) * sub)
            bb, qb = bc[rs], qc[rs]
            o = jnp.zeros((sub, d), F32)
            for s in range((rb + 1) * sub):
                diff = bb - bc[s:s + 1, :]
                if s >= rb * sub:
                    diff = jnp.where(rid >= s - rb * sub, diff, -jnp.inf)
                a = jnp.sum(qb * kc[s:s + 1, :] * jnp.exp(diff), axis=1, keepdims=True)
                o = o + a * vc[s:s + 1, :]
            blocks.append(o)
        return blocks[0] if len(blocks) == 1 else jnp.concatenate(blocks, axis=0)

    def chunk_body(c, carry):
        items = range(nb * HG_HEADS)
        lanes = [slice((i % HG_HEADS) * d, (i % HG_HEADS + 1) * d) for i in items]
        starts = []
        for e in range(nb):
            r0 = e * te + c * chunk
            starts.append(r0 if isinstance(c, int) else pl.multiple_of(r0, chunk))
        rws = [pl.ds(starts[i // HG_HEADS], chunk) for i in items]
        inter, upd, decay = [], [], []
        for i in items:
            sl, rows = lanes[i], rws[i]
            bc = b_scr[rows, sl]
            b_last = bc[chunk - 1:chunk, :]
            kt = (kk_scr[rows, sl] * jnp.exp(b_last - bc)).astype(BF16)
            if transposed:
                inter.append(lax.dot_general(qt_scr[rows, sl], st_scr[i].astype(BF16), NT_DIMS,
                                             preferred_element_type=F32))
                upd.append(lax.dot_general(vb_scr[rows, sl], kt, TN_DIMS, preferred_element_type=F32))
                decay.append(jnp.exp(b_last))
            else:
                inter.append(jnp.dot(qt_scr[rows, sl], st_scr[i].astype(BF16), preferred_element_type=F32))
                upd.append(lax.dot_general(kt, vb_scr[rows, sl], TN_DIMS, preferred_element_type=F32))
                decay.append(jnp.broadcast_to(jnp.exp(b_last), (sub, d)).T[:, 0:1])
        for i in items:
            sl, rows = lanes[i], rws[i]
            o = intra(b_scr[rows, sl], qs_scr[rows, sl], kk_scr[rows, sl], i_ref[rows, sl])
            h_scr[rows, sl] = o + inter[i]
            st_scr[i] = st_scr[i] * decay[i] + upd[i]
        return carry

    nchunk = te // chunk
    if nchunk == 1:
        chunk_body(0, 0)
    else:
        lax.fori_loop(0, nchunk, chunk_body, 0)

    for h in range(HG_HEADS):
        sl = slice(h * d, (h + 1) * d)
        hh = h_scr[:, sl]
        ms = jnp.mean(hh * hh, axis=1, keepdims=True)
        y = hh * lax.rsqrt(ms + 1e-5) * norm_w[:, sl] * _silu(z_ref[:, sl])
        o_ref[:, sl] = y.astype(BF16)

    @pl.when(t == nt - 1)
    def _():
        for e in range(nb):
            for h in range(HG_HEADS):
                s_ref[e, h] = st_scr[e * HG_HEADS + h].T if transposed else st_scr[e * HG_HEADS + h]


ANY_SPEC = pl.BlockSpec(memory_space=pl.ANY)


def _state_in_spec(state, layer, tail, nb=1):
    zeros = (0,) * len(tail)
    if state.ndim == len(tail) + 2:
        return pl.BlockSpec((None, nb) + tail, lambda g, t: (layer, g) + zeros)
    return pl.BlockSpec((nb,) + tail, lambda g, t: (g,) + zeros)


def _state_out_spec(layer, tail, nb=1):
    zeros = (0,) * len(tail)
    return pl.BlockSpec((None, nb) + tail, lambda g, t: (layer, g) + zeros)


class _Rows:
    def __init__(self, row0, batch, seq, te, nb=1):
        self.nb, self.te, self.rows = nb, te, nb * te
        self.groups, self.nt = batch // nb, seq // te
        self.rb0 = row0 // self.rows
        self.grid = (self.groups, self.nt)

    def spec(self, width, col_block):
        groups, rb0 = self.groups, self.rb0
        return pl.BlockSpec((self.rows, width), lambda g, t: (rb0 + t * groups + g, col_block))


def _mix_out_spec(rows, group):
    return rows.spec(GROUP_W, group)


def _sds(x):
    return jax.ShapeDtypeStruct(x.shape, x.dtype)


def _hgrn(p, par, state, mix, acc, layer, rows, chunk):
    nb = rows.nb
    pspec = lambda cb: rows.spec(GROUP_W, cb)
    tail = (HG_HEADS, HG_D, HG_D)
    big = lambda dt: pltpu.VMEM((rows.rows, GROUP_W), dt)
    return pl.pallas_call(
        functools.partial(_hgrn_kernel, chunk=chunk, nb=nb, te=rows.te, nt=rows.nt),
        grid=rows.grid,
        in_specs=[pspec(C_HQ // GROUP_W), pspec(C_HF // GROUP_W), pspec(C_HI // GROUP_W), pspec(C_HZ // GROUP_W),
                  pl.BlockSpec((None, 8, GROUP_W), lambda b, t: (layer, 0, 0)),
                  _state_in_spec(state, layer, tail, nb), ANY_SPEC, ANY_SPEC],
        out_specs=[_mix_out_spec(rows, 0), _state_out_spec(layer, tail, nb)],
        out_shape=[_sds(mix), _sds(acc)],
        input_output_aliases={6: 0, 7: 1},
        scratch_shapes=[pltpu.VMEM((nb * HG_HEADS, HG_D, HG_D), F32),
                        big(F32), big(F32), big(F32), big(BF16), big(BF16), big(F32)],
        compiler_params=_cparams(("parallel", "arbitrary")),
        name="hgrn2",
    )(p, p, p, p, par, state, mix, acc)


def _mlstm_kernel(qk_ref, v_ref, z_ref, g_ref, par_ref, bias_ref, c0_ref, n0_ref, m0_ref,
                  _mix_ref, _acc_c_ref, _acc_n_ref, _acc_m_ref,
                  o_ref, c_ref, n_ref, m_ref, m_scr, h_scr, *, chunk, nb, te, nt):
    t = pl.program_id(1)
    L = chunk
    sub = 8

    @pl.when(t == 0)
    def _():
        c_ref[...] = c0_ref[...]
        n_ref[...] = n0_ref[...]
        m_scr[...] = jnp.zeros(m_scr.shape, F32)
        for e in range(nb):
            for h in range(ML_HEADS):
                m_scr[e * sub + h:e * sub + h + 1, :] = jnp.broadcast_to(m0_ref[e, :, h:h + 1], (1, LANE))

    r = lax.broadcasted_iota(jnp.int32, (L, L), 0)
    c = lax.broadcasted_iota(jnp.int32, (L, L), 1)
    tril = r >= c
    triu = r <= c

    items = [(e, h) for e in range(nb) for h in range(ML_HEADS)]
    nchunk = te // L
    pre = {}
    for ci in range(nchunk):
        for e in range(nb):
            rows = slice(e * te + ci * L, e * te + (ci + 1) * L)
            g = g_ref[rows, :]
            gt = g.T
            for h in range(ML_HEADS):
                ib = bias_ref[:, h:h + 1]
                fb = bias_ref[:, ML_HEADS + h:ML_HEADS + h + 1]
                i_col = g[:, h:h + 1] + ib
                i_row = gt[h:h + 1, 0:L] + ib
                lf_col = _log_sigmoid(g[:, ML_HEADS + h:ML_HEADS + h + 1] + fb)
                lf_row = _log_sigmoid(gt[ML_HEADS + h:ML_HEADS + h + 1, 0:L] + fb)
                b_col = jnp.sum(jnp.where(tril, lf_row, 0.0), axis=1, keepdims=True)
                b_row = jnp.sum(jnp.where(triu, lf_col, 0.0), axis=0, keepdims=True)
                dmat = jnp.where(tril, b_col - b_row + i_row, -jnp.inf)
                q = qk_ref[rows, h * ML_DK:(h + 1) * ML_DK]
                k = qk_ref[rows, ML_HEADS * ML_DK + h * ML_DK:ML_HEADS * ML_DK + (h + 1) * ML_DK] * (ML_DK ** -0.5)
                qb = q.astype(BF16)
                pre[ci, e, h] = dict(
                    rows=rows, i_col=i_col, b_col=b_col, dmat=dmat, dmax=jnp.max(dmat, axis=1, keepdims=True),
                    q=q, k=k, qb=qb, vb=v_ref[rows, h * ML_DV:(h + 1) * ML_DV].astype(BF16),
                    s_qk=lax.dot_general(qb, k.astype(BF16), NT_DIMS, preferred_element_type=F32))

    for ci in range(nchunk):
        mid = {}
        for e, h in items:
            a = pre[ci, e, h]
            m = m_scr[e * sub + h:e * sub + h + 1, 0:1]
            m_t = jnp.maximum(a["b_col"] + m, a["dmax"])
            pm = jnp.exp(a["dmat"] - m_t) * a["s_qk"]
            cst = c_ref[e, h]
            mid[e, h] = dict(m=m, m_t=m_t, pm=pm, cst=cst,
                             pv=jnp.dot(pm.astype(BF16), a["vb"], preferred_element_type=F32),
                             qc=jnp.dot(a["qb"], cst.astype(BF16), preferred_element_type=F32))
        for e, h in items:
            a, u = pre[ci, e, h], mid[e, h]
            m, m_t, b_col = u["m"], u["m_t"], a["b_col"]
            nst = n_ref[e, h:h + 1, :]
            inter = jnp.exp(b_col + m - m_t)
            num = u["pv"] + inter * u["qc"]
            den = jnp.sum(u["pm"], axis=1, keepdims=True) + inter * jnp.sum(a["q"] * nst, axis=1, keepdims=True)
            hh = num / jnp.maximum(jnp.abs(den), jnp.exp(-m_t))
            m_new = m_t[L - 1:L, :]
            b_last = b_col[L - 1:L, :]
            wgt = jnp.exp(b_last - b_col + a["i_col"] - m_new)
            decay = jnp.exp(b_last + m - m_new)
            kw = a["k"] * wgt
            c_ref[e, h] = decay * u["cst"] + lax.dot_general(kw.astype(BF16), a["vb"], TN_DIMS,
                                                             preferred_element_type=F32)
            n_ref[e, h:h + 1, :] = decay * nst + jnp.sum(kw, axis=0, keepdims=True)
            m_scr[e * sub + h:e * sub + h + 1, :] = jnp.broadcast_to(m_new, (1, LANE))
            h_scr[a["rows"], h * ML_DV:(h + 1) * ML_DV] = hh

    for h in range(ML_HEADS):
        sl = slice(h * ML_DV, (h + 1) * ML_DV)
        x = h_scr[:, sl]
        xc = x - jnp.mean(x, axis=1, keepdims=True)
        y = xc * lax.rsqrt(jnp.mean(xc * xc, axis=1, keepdims=True) + 1e-6)
        o_ref[:, sl] = (y * par_ref[:, sl] * _silu(z_ref[:, sl])).astype(BF16)

    @pl.when(t == nt - 1)
    def _():
        for e in range(nb):
            m_ref[e] = m_scr[e * sub:(e + 1) * sub, :]


def _mlstm(p, ps, norm_w, bias, c0, n0, m0, mix, acc_c, acc_n, acc_m, layer, rows, chunk):
    nb = rows.nb
    pspec = lambda cb: rows.spec(GROUP_W, cb)
    c_tail, n_tail, m_tail = (ML_HEADS, ML_DK, ML_DV), (ML_HEADS, ML_DK), (1, ML_HEADS)
    return pl.pallas_call(
        functools.partial(_mlstm_kernel, chunk=chunk, nb=nb, te=rows.te, nt=rows.nt),
        grid=rows.grid,
        in_specs=[pspec(C_MQK // GROUP_W), pspec(C_MV // GROUP_W), pspec(C_MZ // GROUP_W),
                  rows.spec(LANE, CB_MG),
                  pl.BlockSpec((None, 1, GROUP_W), lambda b, t: (layer, 0, 0)),
                  pl.BlockSpec((None, 1, 2 * ML_HEADS), lambda b, t: (layer, 0, 0)),
                  _state_in_spec(c0, layer, c_tail, nb), _state_in_spec(n0, layer, n_tail, nb),
                  _state_in_spec(m0, layer, m_tail, nb), ANY_SPEC, ANY_SPEC, ANY_SPEC, ANY_SPEC],
        out_specs=[_mix_out_spec(rows, 1), _state_out_spec(layer, c_tail, nb), _state_out_spec(layer, n_tail, nb),
                   _state_out_spec(layer, (8, LANE), nb)],
        out_shape=[_sds(mix), _sds(acc_c), _sds(acc_n), _sds(acc_m)],
        input_output_aliases={9: 0, 10: 1, 11: 2, 12: 3},
        scratch_shapes=[pltpu.VMEM((nb * 8, LANE), F32), pltpu.VMEM((rows.rows, GROUP_W), F32)],
        compiler_params=_cparams(("parallel", "arbitrary")),
        name="mlstm",
    )(p, p, p, ps, norm_w, bias, c0, n0, m0, mix, acc_c, acc_n, acc_m)


RW_CHUNK = 16


def _rwkv_kernel(r_ref, k_ref, v_ref, z_ref, x_ref, buf_ref, bufx_ref, mu_ref, mux_ref, par_ref,
                 w2_ref, a2_ref, s0_ref, _mix_ref, _acc_ref, o_ref, s_ref,
                 sp_scr, prev_scr, prevx_scr, kh_scr, rh_scr, ki_scr, ai_scr, kd_scr, ad_scr, vb_scr,
                 gl_scr, y_scr, bonus_scr, *, nb, te, nt):
    t = pl.program_id(1)
    L = RW_CHUNK
    W = GROUP_W
    tc = nb * te
    tep = max(te, L)
    tp = nb * tep

    lane2 = lax.broadcasted_iota(jnp.int32, (LANE, LANE), 1)
    row2 = lax.broadcasted_iota(jnp.int32, (LANE, LANE), 0)
    diag_blocks = (lane2 < RW_HEAD) == (row2 < RW_HEAD)
    seg_ones = jnp.where(diag_blocks, 1.0, 0.0).astype(BF16)

    @pl.when(t == 0)
    def _():
        zero = jnp.zeros((RW_HEAD, RW_HEAD), F32)
        for e in range(nb):
            prev_scr[e:e + 1, :] = buf_ref[e]
            prevx_scr[e:e + 1, :] = bufx_ref[e]
            for pr in range(RW_PAIRS):
                top = jnp.concatenate([s0_ref[e, 2 * pr], zero], axis=1)
                bot = jnp.concatenate([zero, s0_ref[e, 2 * pr + 1]], axis=1)
                sp_scr[e * RW_PAIRS + pr] = jnp.concatenate([top, bot], axis=0)

    def seg_sum(x):
        parts = [_dot_sel_right(x[:, i * LANE:(i + 1) * LANE], seg_ones, 2) for i in range(x.shape[1] // LANE)]
        return jnp.concatenate(parts, axis=1)

    def mix(cur, prev_rows, mu):
        rid = lax.broadcasted_iota(jnp.int32, cur.shape, 0)
        prev = pltpu.roll(cur, 1, axis=0) if tc > 1 else cur
        for e in range(nb):
            prev = jnp.where(rid == e * te, prev_rows[e:e + 1, :], prev)
        return cur + (prev - cur) * mu

    pr_ = r_ref[...]
    pk_ = k_ref[...]
    pv_ = v_ref[...]
    px_ = x_ref[...]
    xr = mix(pr_, prev_scr[:, 0:W], mu_ref[:, 0:W])
    xk = mix(pk_, prev_scr[:, W:2 * W], mu_ref[:, W:2 * W])
    xv = mix(pv_, prev_scr[:, 2 * W:3 * W], mu_ref[:, 2 * W:3 * W])
    xx = mix(px_, prevx_scr[...], mux_ref[...])
    for e in range(nb):
        last = slice((e + 1) * te - 1, (e + 1) * te)
        prev_scr[e:e + 1, 0:W] = pr_[last, :]
        prev_scr[e:e + 1, W:2 * W] = pk_[last, :]
        prev_scr[e:e + 1, 2 * W:3 * W] = pv_[last, :]
        prevx_scr[e:e + 1, :] = px_[last, :]

    w0 = par_ref[0:1, :]
    a0 = par_ref[1:2, :]
    k_k = par_ref[2:3, :]
    k_a = par_ref[3:4, :]
    r_k = par_ref[4:5, :]
    ln_w = par_ref[5:6, :]
    ln_b = par_ref[6:7, :]

    wlin = w0 + jnp.dot(jnp.tanh(xx).astype(BF16), w2_ref[...], preferred_element_type=F32)
    wdec = -_softplus(-wlin) - 0.5
    logd = -jnp.exp(wdec)
    a = _sigmoid(a0 + jnp.dot(xx.astype(BF16), a2_ref[...], preferred_element_type=F32))
    kk = xk * k_k
    kk = kk / jnp.maximum(jnp.sqrt(seg_sum(kk * kk)), 1e-12)
    kp = xk * (1.0 + (a - 1.0) * k_a)
    alpha = a * kk
    bonus = seg_sum(xr * kp * r_k) * xv

    if tep > te:
        def pad(u):
            zeros = jnp.zeros((tep - te, u.shape[1]), F32)
            return jnp.concatenate([piece for e in range(nb) for piece in (u[e * te:(e + 1) * te], zeros)], axis=0)
        logd, kk, kp, alpha, xr, xv = pad(logd), pad(kk), pad(kp), pad(alpha), pad(xr), pad(xv)
    bonus_scr[...] = bonus

    tri, ones = _chunk_masks(tp, L)
    g = _dot_sel_left(tri, logd, 3)
    gl = _dot_sel_left(ones, logd, 3)
    einv = jnp.exp(-g)
    egl = jnp.exp(gl - g)
    kh_scr[...] = (kk * jnp.exp(g - logd)).astype(BF16)
    rh_scr[...] = (xr * jnp.exp(g)).astype(BF16)
    ki_scr[...] = (kp * einv).astype(BF16)
    ai_scr[...] = (alpha * einv).astype(BF16)
    kd_scr[...] = (kp * egl).astype(BF16)
    ad_scr[...] = (alpha * egl).astype(BF16)
    vb_scr[...] = xv.astype(BF16)
    gl_scr[...] = jnp.exp(gl)

    lane_l = lax.broadcasted_iota(jnp.int32, (L, LANE), 1)
    head_a = lane_l < RW_HEAD
    rl = lax.broadcasted_iota(jnp.int32, (L, L), 0)
    cl = lax.broadcasted_iota(jnp.int32, (L, L), 1)
    incl = rl >= cl
    rl2 = lax.broadcasted_iota(jnp.int32, (2 * L, L), 0) & (L - 1)
    strict2 = rl2 > lax.broadcasted_iota(jnp.int32, (2 * L, L), 1)
    cl_ab = lax.broadcasted_iota(jnp.int32, (L, 2 * L), 1) & (L - 1)
    strict_ab = lax.broadcasted_iota(jnp.int32, (L, 2 * L), 0) > cl_ab
    sp_r = lax.broadcasted_iota(jnp.int32, (2 * L, (L - 1) * LANE), 0)
    sp_c = lax.broadcasted_iota(jnp.int32, (2 * L, (L - 1) * LANE), 1)
    sp_head = jnp.where((sp_c & (LANE - 1)) < RW_HEAD, 0, L)
    spread = jnp.where(sp_r == lax.shift_right_logical(sp_c, 7) + sp_head, 1.0, 0.0).astype(BF16)
    zb = jnp.zeros((L, LANE), BF16)

    def chunk_body(c, carry):
        pairs = range(nb * RW_PAIRS)
        lanes = [slice((i % RW_PAIRS) * LANE, (i % RW_PAIRS + 1) * LANE) for i in pairs]
        starts = []
        for e in range(nb):
            r0 = e * tep + c * L
            starts.append(r0 if isinstance(c, int) else pl.multiple_of(r0, L))
        rws = [pl.ds(starts[i // RW_PAIRS], L) for i in pairs]
        vb = [vb_scr[rws[i], lanes[i]] for i in pairs]
        gm, ks, nab = [], [], []
        for pr in pairs:
            sl, rows = lanes[pr], rws[pr]
            kh = kh_scr[rows, sl]
            rh = rh_scr[rows, sl]
            x4 = jnp.concatenate([jnp.where(head_a, kh, zb), jnp.where(head_a, zb, kh),
                                  jnp.where(head_a, rh, zb), jnp.where(head_a, zb, rh)], axis=0)
            ai = ai_scr[rows, sl]
            y2 = jnp.concatenate([ai, ki_scr[rows, sl]], axis=0)
            gm.append(lax.dot_general(x4, y2, NT_DIMS, preferred_element_type=F32))
            ai2 = jnp.concatenate([jnp.where(head_a, ai, zb), jnp.where(head_a, zb, ai)], axis=0)
            nab.append(lax.dot_general(kh, ai2, NT_DIMS, preferred_element_type=F32))
            ks.append(lax.dot_general(jnp.concatenate([kh, rh], axis=0), sp_scr[pr].astype(BF16), NT_DIMS,
                                      preferred_element_type=F32))
        mv, coef = [], []
        for pr in pairs:
            g = gm[pr]
            m_ab = jnp.where(strict2, g[0:2 * L, L:2 * L], 0.0)
            mv.append(jnp.dot(m_ab.astype(BF16), vb[pr], preferred_element_type=F32))
            coef.append(_dot_sel_right(jnp.where(strict_ab, nab[pr], 0.0), spread, 1))
        vw = []
        for pr in pairs:
            w = ks[pr][0:L] + jnp.where(head_a, mv[pr][0:L], mv[pr][L:2 * L])
            for s in range(L - 1):
                w = w - coef[pr][:, s * LANE:(s + 1) * LANE] * w[s:s + 1, :]
            vw.append(jnp.concatenate([vb[pr], w.astype(BF16)], axis=0))
        yy, upd = [], []
        for pr in pairs:
            g = gm[pr]
            cm = jnp.concatenate([
                jnp.concatenate([jnp.where(incl, g[2 * L:3 * L, L:2 * L], 0.0),
                                 -jnp.where(incl, g[2 * L:3 * L, 0:L], 0.0)], axis=1),
                jnp.concatenate([jnp.where(incl, g[3 * L:4 * L, L:2 * L], 0.0),
                                 -jnp.where(incl, g[3 * L:4 * L, 0:L], 0.0)], axis=1)], axis=0)
            yy.append(jnp.dot(cm.astype(BF16), vw[pr], preferred_element_type=F32))
        grp = LANE // (2 * L)
        zblk = jnp.zeros((2 * L, LANE), BF16)
        for g0 in range(0, nb * RW_PAIRS, grp):
            members = range(g0, g0 + grp)
            vw_t = jnp.concatenate([vw[pr] for pr in members], axis=0).T
            kd_rows = []
            for q, pr in enumerate(members):
                sl, rows = lanes[pr], rws[pr]
                kd_ad = jnp.concatenate([kd_scr[rows, sl], -ad_scr[rows, sl]], axis=0)
                kd_rows.append(jnp.concatenate([kd_ad if col == q else zblk for col in range(grp)], axis=1))
            u_all = jnp.dot(vw_t, jnp.concatenate(kd_rows, axis=0), preferred_element_type=F32)
            for q in range(grp):
                upd.append(u_all[:, q * LANE:(q + 1) * LANE])
        for pr in pairs:
            sl, rows = lanes[pr], rws[pr]
            y_scr[rows, sl] = ks[pr][L:2 * L] + jnp.where(head_a, yy[pr][0:L], yy[pr][L:2 * L])
            gl_row = gl_scr[pl.ds(starts[pr // RW_PAIRS], 1), sl]
            sp_scr[pr] = sp_scr[pr] * gl_row + jnp.where(diag_blocks, upd[pr], 0.0)
        return carry

    nchunk = tep // L
    if nchunk == 1:
        chunk_body(0, 0)
    else:
        lax.fori_loop(0, nchunk, chunk_body, 0)

    if tep > te:
        y = jnp.concatenate([y_scr[e * tep:e * tep + te, :] for e in range(nb)], axis=0)
    else:
        y = y_scr[...]
    yc = y - seg_sum(y) * (1.0 / RW_HEAD)
    yn = yc * lax.rsqrt(seg_sum(yc * yc) * (1.0 / RW_HEAD) + RW_GN_EPS)
    out = (yn * ln_w + ln_b + bonus_scr[...]) * _silu(z_ref[...])
    o_ref[...] = out.astype(BF16)

    @pl.when(t == nt - 1)
    def _():
        for e in range(nb):
            for pr in range(RW_PAIRS):
                sp = sp_scr[e * RW_PAIRS + pr]
                s_ref[e, 2 * pr] = sp[0:RW_HEAD, 0:RW_HEAD]
                s_ref[e, 2 * pr + 1] = sp[RW_HEAD:LANE, RW_HEAD:LANE]


def _rwkv(p, ps, buf, bufx, mu, mux, par, w2p, a2p, state, mix, acc, layer, rows):
    nb, te = rows.nb, rows.te
    tp = nb * max(te, RW_CHUNK)
    pspec = lambda cb: rows.spec(GROUP_W, cb)
    tail = (RW_HEADS, RW_HEAD, RW_HEAD)
    lay = lambda shape: pl.BlockSpec((None,) + shape, lambda b, t: (layer,) + (0,) * len(shape))
    big = lambda dt: pltpu.VMEM((tp, GROUP_W), dt)
    return pl.pallas_call(
        functools.partial(_rwkv_kernel, nb=nb, te=te, nt=rows.nt),
        grid=rows.grid,
        in_specs=[pspec(C_RR // GROUP_W), pspec(C_RK // GROUP_W), pspec(C_RV // GROUP_W), pspec(C_RZ // GROUP_W),
                  rows.spec(LANE, CB_RX),
                  pl.BlockSpec((nb, 1, 3 * GROUP_W), lambda g, t: (g, 0, 0)),
                  pl.BlockSpec((nb, 1, LANE), lambda g, t: (g, 0, 0)),
                  lay((1, 3 * GROUP_W)), lay((1, LANE)), lay((8, GROUP_W)),
                  lay((LANE, GROUP_W)), lay((LANE, GROUP_W)), _state_in_spec(state, layer, tail, nb),
                  ANY_SPEC, ANY_SPEC],
        out_specs=[_mix_out_spec(rows, 2), _state_out_spec(layer, tail, nb)],
        out_shape=[_sds(mix), _sds(acc)],
        input_output_aliases={13: 0, 14: 1},
        scratch_shapes=[pltpu.VMEM((nb * RW_PAIRS, LANE, LANE), F32),
                        pltpu.VMEM((nb, 3 * GROUP_W), F32), pltpu.VMEM((nb, LANE), F32),
                        big(BF16), big(BF16), big(BF16), big(BF16), big(BF16), big(BF16), big(BF16),
                        big(F32), big(F32), pltpu.VMEM((rows.rows, GROUP_W), F32)],
        compiler_params=_cparams(("parallel", "arbitrary")),
        name="rwkv7",
    )(p, p, p, p, ps, buf, bufx, mu, mux, par, w2p, a2p, state, mix, acc)


def _trunk_layer(l, x_f32, x_bf, w_in_t, w_out_b, hg_par, ml_nw, ml_bias, rw_mu, rw_mux, rw_par, rw_w2, rw_a2,
                 ln_g, ln_b, groups, cfg):
    p = _proj_matmul(x_bf, w_in_t, l, cfg["mm_tm"], cfg["mm_tn"])
    ps = _proj_small(x_bf, w_in_t, l, cfg["mm_tm"])
    mix = jnp.zeros(x_bf.shape, BF16)
    for grp in groups:
        row0, batch, seq, te = grp["row0"], grp["batch"], grp["seq"], grp["te"]
        rows = lambda nb: _Rows(row0, batch, seq, te, nb)
        mix, grp["acc_hg"] = _hgrn(p, hg_par, grp["hg"], mix, grp["acc_hg"], l, rows(grp["hg_nb"]), grp["hg_chunk"])
        mix, grp["acc_c"], grp["acc_n"], grp["acc_m"] = _mlstm(
            p, ps, ml_nw, ml_bias, grp["ml_c"], grp["ml_n"], grp["ml_m"], mix, grp["acc_c"], grp["acc_n"],
            grp["acc_m"], l, rows(grp["ml_nb"]), grp["ml_chunk"])
        mix, grp["acc_rw"] = _rwkv(p, ps, grp["rw_buf"][l], grp["rw_bufx"][l], rw_mu, rw_mux, rw_par, rw_w2, rw_a2,
                                   grp["rw"], mix, grp["acc_rw"], l, rows(grp["rw_nb"]))
        if grp["cache_layout"]:
            mix = _mem_attn_cache(p, grp["mem_k"], grp["mem_v"], mix, l, rows(grp["xa_nb"]))
        else:
            mix = _mem_attn(p, grp["mem_k"], grp["mem_v"], mix, l, rows(grp["xa_nb"]))
        first = row0 + (seq // te - 1) * batch * te + te - 1
        last_rkv = lax.slice(p, (first, C_RR), (row0 + batch * seq, C_RR + 3 * GROUP_W), (te, 1))
        last_x = lax.slice(ps, (first, CB_RX * LANE), (row0 + batch * seq, (CB_RX + 1) * LANE), (te, 1))
        grp["buf_new"].append(jnp.concatenate([last_rkv, last_x], axis=-1))
    x_f32, x_bf = _outproj_ln(mix, w_out_b, l, x_f32, ln_g, ln_b, cfg["op_tm"], cfg["op_tn"])
    return x_f32, x_bf


def kernel(x_prompt, x_sample, mem_prompt, state_hgrn, state_mlstm_C, state_mlstm_n, state_mlstm_m, state_rwkv, state_rwkv_shift, cache_mem_k, cache_mem_v, w_in, hgrn_lb, hgrn_norm_w, mlstm_ig_b, mlstm_fg_b, mlstm_norm_w, rwkv_mu, rwkv_w0, rwkv_w2, rwkv_a0, rwkv_a2, rwkv_k_k, rwkv_k_a, rwkv_r_k, rwkv_ln_w, rwkv_ln_b, mem_wk, mem_wv, w_out, ln_g, ln_b):
    bp, tp_, _ = x_prompt.shape
    bs, ts, _ = x_sample.shape
    depth = w_in.shape[0]
    mp, ms = bp * tp_, bs * ts

    w_in_t = jnp.swapaxes(w_in, 1, 2)
    w_out_b = w_out.astype(BF16)
    lb_all = jnp.cumsum(jax.nn.softmax(hgrn_lb.astype(F32), axis=0), axis=0)
    lb_all = lb_all - lb_all[0]
    zrow = jnp.zeros_like(lb_all)
    hg_par = jnp.stack([jnp.log(lb_all), jnp.log1p(-lb_all), 1.0 - lb_all, hgrn_norm_w.astype(F32),
                        zrow, zrow, zrow, zrow], axis=1)
    ml_nw = mlstm_norm_w.astype(F32)[:, None, :]
    ml_bias = jnp.concatenate([mlstm_ig_b, mlstm_fg_b], axis=-1).astype(F32)[:, None, :]
    rw_mu = rwkv_mu[:, None, :3 * GROUP_W].astype(F32)
    rw_mux = rwkv_mu[:, None, 3 * GROUP_W:].astype(F32)
    zr = jnp.zeros((depth, GROUP_W), F32)
    rw_par = jnp.stack([rwkv_w0, rwkv_a0, rwkv_k_k, rwkv_k_a, rwkv_r_k.reshape(depth, GROUP_W), rwkv_ln_w,
                        rwkv_ln_b, zr], axis=1).astype(F32)
    zl = jnp.zeros((depth, RW_LORA, GROUP_W), F32)
    rw_w2 = jnp.concatenate([rwkv_w2.astype(F32), zl], axis=1).astype(BF16)
    rw_a2 = jnp.concatenate([zl, rwkv_a2.astype(F32)], axis=1).astype(BF16)
    ln_g3 = ln_g.astype(F32)[:, None, :]
    ln_b3 = ln_b.astype(F32)[:, None, :]

    mem_x = mem_prompt.reshape(bp * N_MEM, D_MODEL).astype(BF16)
    mk_p = _matmul_layers(mem_x, mem_wk, 256)
    mv_p = _matmul_layers(mem_x, mem_wv, 256)
    mk_out = mk_p.reshape(depth, bp, N_MEM, XA_HEADS, XA_DH)
    mv_out = mv_p.reshape(depth, bp, N_MEM, XA_HEADS, XA_DH)

    def split_buf(buf):
        return buf[:, :, None, :3 * GROUP_W].astype(F32), buf[:, :, None, 3 * GROUP_W:].astype(F32)

    def cache_view(c):
        c = c.reshape(depth, bs, N_MEM, XA_HEADS, XA_SUB, LANE)
        return jnp.transpose(c, (0, 1, 2, 4, 3, 5)).reshape(depth, bs * XA_ROWS, LANE)

    def results(b):
        return dict(acc_hg=jnp.zeros((depth, b, HG_HEADS, HG_D, HG_D), F32),
                    acc_c=jnp.zeros((depth, b, ML_HEADS, ML_DK, ML_DV), F32),
                    acc_n=jnp.zeros((depth, b, ML_HEADS, ML_DK), F32),
                    acc_m=jnp.zeros((depth, b, 8, LANE), F32),
                    acc_rw=jnp.zeros((depth, b, RW_HEADS, RW_HEAD, RW_HEAD), F32), buf_new=[])

    zbuf, zbufx = split_buf(jnp.zeros((depth, bp, RW_SHIFT_W), F32))
    sbuf, sbufx = split_buf(state_rwkv_shift)
    te_p = 128
    prompt = dict(row0=0, batch=bp, seq=tp_, te=te_p, hg_nb=1, ml_nb=2, xa_nb=4,
                  hg=jnp.zeros((bp, HG_HEADS, HG_D, HG_D), F32), hg_chunk=16,
                  ml_c=jnp.zeros((bp, ML_HEADS, ML_DK, ML_DV), F32), ml_n=jnp.zeros((bp, ML_HEADS, ML_DK), F32),
                  ml_m=jnp.zeros((bp, 1, ML_HEADS), F32), ml_chunk=64,
                  rw=jnp.zeros((bp, RW_HEADS, RW_HEAD, RW_HEAD), F32), rw_buf=zbuf, rw_bufx=zbufx, rw_nb=2,
                  mem_k=mk_p, mem_v=mv_p, cache_layout=False, **results(bp))
    sample = dict(row0=mp, batch=bs, seq=ts, te=ts, hg_nb=4, ml_nb=4, xa_nb=4,
                  hg=state_hgrn, hg_chunk=ts,
                  ml_c=state_mlstm_C, ml_n=state_mlstm_n, ml_m=state_mlstm_m[:, :, None, :], ml_chunk=ts,
                  rw=state_rwkv, rw_buf=sbuf, rw_bufx=sbufx, rw_nb=4,
                  mem_k=cache_view(cache_mem_k), mem_v=cache_view(cache_mem_v), cache_layout=True, **results(bs))
    cfg = dict(mm_tm=1536, mm_tn=512, op_tm=384, op_tn=512)

    nblk = tp_ // te_p
    xp = jnp.transpose(x_prompt.reshape(bp, nblk, te_p, D_MODEL), (1, 0, 2, 3)).reshape(mp, D_MODEL)
    x_f32 = jnp.concatenate([xp, x_sample.reshape(ms, D_MODEL)], axis=0).astype(F32)
    x_bf = x_f32.astype(BF16)
    for l in range(depth):
        x_f32, x_bf = _trunk_layer(l, x_f32, x_bf, w_in_t, w_out_b, hg_par, ml_nw, ml_bias, rw_mu, rw_mux,
                                   rw_par, rw_w2, rw_a2, ln_g3, ln_b3, [prompt, sample], cfg)

    def states(g):
        return (g["acc_hg"], g["acc_c"], g["acc_n"], g["acc_m"][:, :, :ML_HEADS, 0], g["acc_rw"],
                jnp.stack(g["buf_new"], axis=0))

    y_prompt = jnp.transpose(x_f32[:mp].reshape(nblk, bp, te_p, D_MODEL), (1, 0, 2, 3)).reshape(bp, tp_, D_MODEL)
    y_sample = x_f32[mp:].reshape(bs, ts, D_MODEL)
    return (y_prompt, y_sample) + states(prompt) + (mk_out, mv_out) + states(sample)
```

```python
import functools
import math

import jax
import jax.numpy as jnp
from jax import lax
from jax.experimental import pallas as pl
from jax.experimental.pallas import tpu as pltpu

F32 = jnp.float32
BF16 = jnp.bfloat16
HI = lax.Precision.HIGHEST

D_MODEL = 4096
DEPTH = 4
GROUP_W = D_MODEL // 4
N_MEM = 256
HG_HEADS, HG_D = 8, 128
ML_HEADS, ML_DK, ML_DV = 4, 128, 256
RW_HEADS, RW_HEAD, RW_LORA = 16, 64, 64
RW_PAIRS = RW_HEADS // 2
XA_HEADS, XA_DH = 4, 256
RW_SHIFT_W = 3 * GROUP_W + 2 * RW_LORA
N_IN = 13448
DN_ALPHA = (2.0 * DEPTH) ** 0.25
LN_EPS = 1e-5
RW_GN_EPS = 64e-5

C_HQ, C_HF, C_HI, C_HZ = 0, 1024, 2048, 3072
C_MQK, C_MV, C_MZ = 4096, 5120, 6144
C_RR, C_RK, C_RV, C_RZ = 7168, 8192, 9216, 10240
C_XQ, C_XZ = 11264, 12288
NP_MAIN = 13312
ORIG_MG, ORIG_MZ, ORIG_RX, ORIG_RZ = 6144, 6152, 10248, 10376
CB_RX, CB_MG = 0, 1
LANE = 128

VMEM_LIMIT = 60 * 1024 * 1024

NT_DIMS = (((1,), (1,)), ((), ()))
TN_DIMS = (((0,), (0,)), ((), ()))


def _cparams(sem):
    return pltpu.CompilerParams(dimension_semantics=sem, vmem_limit_bytes=VMEM_LIMIT)


def _sigmoid(x):
    return jax.nn.sigmoid(x)


def _silu(x):
    return x * _sigmoid(x)


def _log_sigmoid(x):
    return jnp.minimum(x, 0.0) - jnp.log1p(jnp.exp(-jnp.abs(x)))


def _softplus(x):
    return jnp.maximum(x, 0.0) + jnp.log1p(jnp.exp(-jnp.abs(x)))


def _chunk_masks(n, chunk):
    r = lax.broadcasted_iota(jnp.int32, (n, n), 0)
    c = lax.broadcasted_iota(jnp.int32, (n, n), 1)
    sh = int(math.log2(chunk))
    same = lax.shift_right_logical(r, sh) == lax.shift_right_logical(c, sh)
    tri = jnp.where(same & (c <= r), 1.0, 0.0).astype(BF16)
    ones = jnp.where(same, 1.0, 0.0).astype(BF16)
    return tri, ones


def _split_bf16(x, terms):
    parts = []
    for i in range(terms):
        part = x.astype(BF16)
        parts.append(part)
        if i + 1 < terms:
            x = x - part.astype(F32)
    return parts


def _dot_sel_left(sel, x, terms):
    return sum(jnp.dot(sel, part, preferred_element_type=F32) for part in _split_bf16(x, terms))


def _dot_sel_right(x, sel, terms):
    return sum(jnp.dot(part, sel, preferred_element_type=F32) for part in _split_bf16(x, terms))


def _chunk_cumsum(x, chunk):
    rid = lax.broadcasted_iota(jnp.int32, x.shape, 0) & (chunk - 1)
    step = 1
    while step < chunk:
        x = x + jnp.where(rid >= step, pltpu.roll(x, step, axis=0), 0.0)
        step *= 2
    return x


def _mm_kernel(x_ref, w_ref, o_ref):
    o_ref[...] = jnp.dot(x_ref[...], w_ref[...].astype(BF16), preferred_element_type=F32)


def _mm_nt_kernel(x_ref, w_ref, o_ref):
    o_ref[...] = lax.dot_general(x_ref[...], w_ref[0].astype(BF16), NT_DIMS, preferred_element_type=F32)


def _proj_matmul(x, w_t, layer, tm, tn):
    m, k = x.shape

    def w_index(i, j):
        col = j * tn
        off = jnp.where(col >= C_RZ, ORIG_RZ - C_RZ, jnp.where(col >= C_MZ, ORIG_MZ - C_MZ, 0))
        return (layer, pl.multiple_of(col + off, 8), 0)

    return pl.pallas_call(
        _mm_nt_kernel,
        grid=(m // tm, NP_MAIN // tn),
        in_specs=[pl.BlockSpec((tm, k), lambda i, j: (i, 0)),
                  pl.BlockSpec((pl.Element(1), pl.Element(tn), pl.Element(k)), w_index)],
        out_specs=pl.BlockSpec((tm, tn), lambda i, j: (i, j)),
        out_shape=jax.ShapeDtypeStruct((m, NP_MAIN), F32),
        compiler_params=_cparams(("parallel", "arbitrary")),
        name="proj_matmul",
    )(x, w_t)


def _proj_small(x, w_t, layer, tm):
    m, k = x.shape

    def kernel(x_ref, wa_ref, wb_ref, o_ref):
        w = jnp.concatenate([wa_ref[0], wb_ref[0]], axis=0).astype(BF16)
        o_ref[...] = lax.dot_general(x_ref[...], w, NT_DIMS, preferred_element_type=F32)

    wspec = lambda row: pl.BlockSpec((pl.Element(1), pl.Element(LANE), pl.Element(k)), lambda i: (layer, row, 0))
    return pl.pallas_call(
        kernel,
        grid=(m // tm,),
        in_specs=[pl.BlockSpec((tm, k), lambda i: (i, 0)), wspec(ORIG_RX), wspec(ORIG_MG)],
        out_specs=pl.BlockSpec((tm, 2 * LANE), lambda i: (i, 0)),
        out_shape=jax.ShapeDtypeStruct((m, 2 * LANE), F32),
        compiler_params=_cparams(("parallel",)),
        name="proj_small",
    )(x, w_t, w_t)


def _matmul_layers(x, w, tn):
    m, k = x.shape
    depth, _, n = w.shape
    return pl.pallas_call(
        _mm_kernel,
        grid=(depth, n // tn),
        in_specs=[pl.BlockSpec((m, k), lambda l, j: (0, 0)),
                  pl.BlockSpec((None, k, tn), lambda l, j: (l, 0, j))],
        out_specs=pl.BlockSpec((None, m, tn), lambda l, j: (l, 0, j)),
        out_shape=jax.ShapeDtypeStruct((depth, m, n), F32),
        compiler_params=_cparams(("parallel", "arbitrary")),
        name="mem_kv_matmul",
    )(x, w)


def _outproj_mm_kernel(mix_ref, w_ref, x_ref, o_ref):
    o_ref[...] = DN_ALPHA * x_ref[...] + jnp.dot(mix_ref[...], w_ref[...], preferred_element_type=F32)


def _layer_norm_kernel(y_ref, g_ref, b_ref, xo_ref, xb_ref):
    y = y_ref[...]
    yc = y - jnp.mean(y, axis=1, keepdims=True)
    out = yc * lax.rsqrt(jnp.mean(yc * yc, axis=1, keepdims=True) + LN_EPS) * g_ref[...] + b_ref[...]
    xo_ref[...] = out
    xb_ref[...] = out.astype(BF16)


def _outproj_then_ln(mix, w_out, layer, x, ln_g, ln_b, tm, tn, tr):
    m = x.shape[0]
    y = pl.pallas_call(
        _outproj_mm_kernel,
        grid=(m // tm, D_MODEL // tn),
        in_specs=[pl.BlockSpec((tm, D_MODEL), lambda i, j: (i, 0)),
                  pl.BlockSpec((None, D_MODEL, tn), lambda i, j: (layer, 0, j)),
                  pl.BlockSpec((tm, tn), lambda i, j: (i, j))],
        out_specs=pl.BlockSpec((tm, tn), lambda i, j: (i, j)),
        out_shape=jax.ShapeDtypeStruct((m, D_MODEL), F32),
        compiler_params=_cparams(("parallel", "arbitrary")),
        name="outproj_mm",
    )(mix, w_out, x)
    row_spec = pl.BlockSpec((None, 1, D_MODEL), lambda i: (layer, 0, 0))
    blk = pl.BlockSpec((tr, D_MODEL), lambda i: (i, 0))
    return pl.pallas_call(
        _layer_norm_kernel,
        grid=(m // tr,),
        in_specs=[blk, row_spec, row_spec],
        out_specs=[blk, blk],
        out_shape=[jax.ShapeDtypeStruct((m, D_MODEL), F32), jax.ShapeDtypeStruct((m, D_MODEL), BF16)],
        compiler_params=_cparams(("parallel",)),
        name="layer_norm",
    )(y, ln_g, ln_b)


def _attn_kernel(q_ref, z_ref, k_ref, v_ref, _mix_ref, o_ref, *, nb, te):
    items = [(slice(e * te, (e + 1) * te), slice(e * N_MEM, (e + 1) * N_MEM), slice(h * XA_DH, (h + 1) * XA_DH))
             for e in range(nb) for h in range(XA_HEADS)]
    scores = [lax.dot_general(q_ref[qr, sl].astype(BF16), k_ref[mr, sl].astype(BF16), NT_DIMS,
                              preferred_element_type=F32) * (XA_DH ** -0.5)
              for qr, mr, sl in items]
    outs = []
    for (qr, mr, sl), s in zip(items, scores):
        ex = jnp.exp(s - jnp.max(s, axis=1, keepdims=True))
        pr = ex / jnp.sum(ex, axis=1, keepdims=True)
        outs.append(jnp.dot(pr.astype(BF16), v_ref[mr, sl].astype(BF16), preferred_element_type=F32))
    for (qr, mr, sl), o in zip(items, outs):
        o_ref[qr, sl] = (o * _silu(z_ref[qr, sl])).astype(BF16)


def _mem_attn(p, mem_k, mem_v, mix, layer, rows):
    pspec = lambda cb: rows.spec(GROUP_W, cb)
    kvspec = pl.BlockSpec((None, rows.nb * N_MEM, GROUP_W), lambda g, t: (layer, g, 0))
    return pl.pallas_call(
        functools.partial(_attn_kernel, nb=rows.nb, te=rows.te),
        grid=rows.grid,
        in_specs=[pspec(C_XQ // GROUP_W), pspec(C_XZ // GROUP_W), kvspec, kvspec, ANY_SPEC],
        out_specs=_mix_out_spec(rows, 3),
        out_shape=_sds(mix),
        input_output_aliases={4: 0},
        compiler_params=_cparams(("parallel", "arbitrary")),
        name="mem_attn",
    )(p, p, mem_k, mem_v, mix)


XA_SUB = XA_DH // LANE
XA_ROWS = N_MEM * XA_SUB * XA_HEADS


def _attn_cache_kernel(q_ref, z_ref, k_ref, v_ref, _mix_ref, o_ref, *, nb, te):
    t = te
    tiles = XA_HEADS * XA_SUB
    col = lax.broadcasted_iota(jnp.int32, (t, XA_ROWS), 1)
    col_head = col & (XA_HEADS - 1)
    col_sub = lax.shift_right_logical(col, 2) & (XA_SUB - 1)
    sub_all = lax.shift_right_logical(lax.broadcasted_iota(jnp.int32, (XA_HEADS * t, XA_ROWS), 1), 2) & (XA_SUB - 1)
    scores = []
    for e in range(nb):
        qr = slice(e * t, (e + 1) * t)
        qx = jnp.concatenate([q_ref[qr, c * LANE:(c + 1) * LANE] for c in range(tiles)], axis=0).astype(BF16)
        kx = k_ref[e * XA_ROWS:(e + 1) * XA_ROWS, :].astype(BF16)
        scores.append(lax.dot_general(qx, kx, NT_DIMS, preferred_element_type=F32))
    outs = []
    for e in range(nb):
        s_all = scores[e]
        probs = []
        for h in range(XA_HEADS):
            mine = col_head == h
            part = jnp.zeros((t, XA_ROWS), F32)
            for s in range(XA_SUB):
                rows = slice((h * XA_SUB + s) * t, (h * XA_SUB + s + 1) * t)
                part = part + jnp.where(mine & (col_sub == s), s_all[rows], 0.0)
            other = jnp.where(col_sub == 0, pltpu.roll(part, XA_ROWS - XA_HEADS, axis=1),
                              pltpu.roll(part, XA_HEADS, axis=1))
            sc = jnp.where(mine, (part + other) * (XA_DH ** -0.5), -jnp.inf)
            ex = jnp.exp(sc - jnp.max(sc, axis=1, keepdims=True))
            probs.append(ex / (jnp.sum(ex, axis=1, keepdims=True) * (1.0 / XA_SUB)))
        pr = jnp.concatenate(probs, axis=0)
        vb = v_ref[e * XA_ROWS:(e + 1) * XA_ROWS, :].astype(BF16)
        outs.append([jnp.dot(jnp.where(sub_all == s, pr, 0.0).astype(BF16), vb, preferred_element_type=F32)
                     for s in range(XA_SUB)])
    for e in range(nb):
        qr = slice(e * t, (e + 1) * t)
        for h in range(XA_HEADS):
            for s in range(XA_SUB):
                sl = slice((h * XA_SUB + s) * LANE, (h * XA_SUB + s + 1) * LANE)
                o_ref[qr, sl] = (outs[e][s][h * t:(h + 1) * t] * _silu(z_ref[qr, sl])).astype(BF16)


def _mem_attn_cache(p, cache_k, cache_v, mix, layer, rows):
    assert XA_SUB == 2 and XA_HEADS == 4 and rows.nt == 1
    pspec = lambda cb: rows.spec(GROUP_W, cb)
    kvspec = pl.BlockSpec((None, rows.nb * XA_ROWS, LANE), lambda g, t: (layer, g, 0))
    return pl.pallas_call(
        functools.partial(_attn_cache_kernel, nb=rows.nb, te=rows.te),
        grid=rows.grid,
        in_specs=[pspec(C_XQ // GROUP_W), pspec(C_XZ // GROUP_W), kvspec, kvspec, ANY_SPEC],
        out_specs=_mix_out_spec(rows, 3),
        out_shape=_sds(mix),
        input_output_aliases={4: 0},
        compiler_params=_cparams(("parallel", "arbitrary")),
        name="mem_attn_cache",
    )(p, p, cache_k, cache_v, mix)


def _hgrn_kernel(q_ref, f_ref, i_ref, z_ref, par_ref, s0_ref, _mix_ref, _acc_ref, o_ref, s_ref,
                 st_scr, b_scr, qs_scr, kk_scr, qt_scr, vb_scr, h_scr, *, chunk, nb, te, nt):
    t = pl.program_id(1)
    d = HG_D
    sub = 8
    transposed = nt > 1

    @pl.when(t == 0)
    def _():
        for e in range(nb):
            for h in range(HG_HEADS):
                st_scr[e * HG_HEADS + h] = s0_ref[e, h].T if transposed else s0_ref[e, h]

    log_lb = par_ref[0:1, :]
    log1m_lb = par_ref[1:2, :]
    one_m_lb = par_ref[2:3, :]
    norm_w = par_ref[3:4, :]

    fpre = f_ref[...]
    bt = log1m_lb + _log_sigmoid(fpre)
    logf = jnp.maximum(log_lb, bt) + jnp.log1p(jnp.exp(-jnp.abs(log_lb - bt)))
    b = _chunk_cumsum(logf, chunk)
    qs = _silu(q_ref[...])
    b_scr[...] = b
    qs_scr[...] = qs
    kk_scr[...] = one_m_lb * _sigmoid(-fpre)
    qt_scr[...] = (qs * jnp.exp(b)).astype(BF16)
    vb_scr[...] = i_ref[...].astype(BF16)

    rid = lax.broadcasted_iota(jnp.int32, (sub, d), 0)

    def intra(bc, qc, kc, vc):
        blocks = []
        for rb in range(chunk // sub):
            rs = slice(rb * sub, (rb + 1) * sub)
            bb, qb = bc[rs], qc[rs]
            o = jnp.zeros((sub, d), F32)
            for s in range((rb + 1) * sub):
                diff = bb - bc[s:s + 1, :]
                if s >= rb * sub:
                    diff = jnp.where(rid >= s - rb * sub, diff, -jnp.inf)
                a = jnp.sum(qb * kc[s:s + 1, :] * jnp.exp(diff), axis=1, keepdims=True)
                o = o + a * vc[s:s + 1, :]
            blocks.append(o)
        return blocks[0] if len(blocks) == 1 else jnp.concatenate(blocks, axis=0)

    def chunk_body(c, carry):
        items = range(nb * HG_HEADS)
        lanes = [slice((i % HG_HEADS) * d, (i % HG_HEADS + 1) * d) for i in items]
        starts = []
        for e in range(nb):
            r0 = e * te + c * chunk
            starts.append(r0 if isinstance(c, int) else pl.multiple_of(r0, chunk))
        rws = [pl.ds(starts[i // HG_HEADS], chunk) for i in items]
        inter, upd, decay = [], [], []
        for i in items:
            sl, rows = lanes[i], rws[i]
            bc = b_scr[rows, sl]
            b_last = bc[chunk - 1:chunk, :]
            kt = (kk_scr[rows, sl] * jnp.exp(b_last - bc)).astype(BF16)
            if transposed:
                inter.append(lax.dot_general(qt_scr[rows, sl], st_scr[i].astype(BF16), NT_DIMS,
                                             preferred_element_type=F32))
                upd.append(lax.dot_general(vb_scr[rows, sl], kt, TN_DIMS, preferred_element_type=F32))
                decay.append(jnp.exp(b_last))
            else:
                inter.append(jnp.dot(qt_scr[rows, sl], st_scr[i].astype(BF16), preferred_element_type=F32))
                upd.append(lax.dot_general(kt, vb_scr[rows, sl], TN_DIMS, preferred_element_type=F32))
                decay.append(jnp.broadcast_to(jnp.exp(b_last), (sub, d)).T[:, 0:1])
        for i in items:
            sl, rows = lanes[i], rws[i]
            o = intra(b_scr[rows, sl], qs_scr[rows, sl], kk_scr[rows, sl], i_ref[rows, sl])
            h_scr[rows, sl] = o + inter[i]
            st_scr[i] = st_scr[i] * decay[i] + upd[i]
        return carry

    nchunk = te // chunk
    if nchunk == 1:
        chunk_body(0, 0)
    else:
        lax.fori_loop(0, nchunk, chunk_body, 0)

    for h in range(HG_HEADS):
        sl = slice(h * d, (h + 1) * d)
        hh = h_scr[:, sl]
        ms = jnp.mean(hh * hh, axis=1, keepdims=True)
        y = hh * lax.rsqrt(ms + 1e-5) * norm_w[:, sl] * _silu(z_ref[:, sl])
        o_ref[:, sl] = y.astype(BF16)

    @pl.when(t == nt - 1)
    def _():
        for e in range(nb):
            for h in range(HG_HEADS):
                s_ref[e, h] = st_scr[e * HG_HEADS + h].T if transposed else st_scr[e * HG_HEADS + h]


ANY_SPEC = pl.BlockSpec(memory_space=pl.ANY)


def _state_in_spec(state, layer, tail, nb=1):
    zeros = (0,) * len(tail)
    if state.ndim == len(tail) + 2:
        return pl.BlockSpec((None, nb) + tail, lambda g, t: (layer, g) + zeros)
    return pl.BlockSpec((nb,) + tail, lambda g, t: (g,) + zeros)


def _state_out_spec(layer, tail, nb=1):
    zeros = (0,) * len(tail)
    return pl.BlockSpec((None, nb) + tail, lambda g, t: (layer, g) + zeros)


class _Rows:
    def __init__(self, row0, batch, seq, te, nb=1):
        self.nb, self.te, self.rows = nb, te, nb * te
        self.groups, self.nt = batch // nb, seq // te
        self.rb0 = row0 // self.rows
        self.grid = (self.groups, self.nt)

    def spec(self, width, col_block):
        groups, rb0 = self.groups, self.rb0
        return pl.BlockSpec((self.rows, width), lambda g, t: (rb0 + t * groups + g, col_block))


def _mix_out_spec(rows, group):
    return rows.spec(GROUP_W, group)


def _sds(x):
    return jax.ShapeDtypeStruct(x.shape, x.dtype)


def _hgrn(p, par, state, mix, acc, layer, rows, chunk):
    nb = rows.nb
    pspec = lambda cb: rows.spec(GROUP_W, cb)
    tail = (HG_HEADS, HG_D, HG_D)
    big = lambda dt: pltpu.VMEM((rows.rows, GROUP_W), dt)
    return pl.pallas_call(
        functools.partial(_hgrn_kernel, chunk=chunk, nb=nb, te=rows.te, nt=rows.nt),
        grid=rows.grid,
        in_specs=[pspec(C_HQ // GROUP_W), pspec(C_HF // GROUP_W), pspec(C_HI // GROUP_W), pspec(C_HZ // GROUP_W),
                  pl.BlockSpec((None, 8, GROUP_W), lambda b, t: (layer, 0, 0)),
                  _state_in_spec(state, layer, tail, nb), ANY_SPEC, ANY_SPEC],
        out_specs=[_mix_out_spec(rows, 0), _state_out_spec(layer, tail, nb)],
        out_shape=[_sds(mix), _sds(acc)],
        input_output_aliases={6: 0, 7: 1},
        scratch_shapes=[pltpu.VMEM((nb * HG_HEADS, HG_D, HG_D), F32),
                        big(F32), big(F32), big(F32), big(BF16), big(BF16), big(F32)],
        compiler_params=_cparams(("parallel", "arbitrary")),
        name="hgrn2",
    )(p, p, p, p, par, state, mix, acc)


def _mlstm_kernel(qk_ref, v_ref, z_ref, g_ref, par_ref, bias_ref, c0_ref, n0_ref, m0_ref,
                  _mix_ref, _acc_c_ref, _acc_n_ref, _acc_m_ref,
                  o_ref, c_ref, n_ref, m_ref, m_scr, h_scr, *, chunk, nb, te, nt):
    t = pl.program_id(1)
    L = chunk
    sub = 8

    @pl.when(t == 0)
    def _():
        c_ref[...] = c0_ref[...]
        n_ref[...] = n0_ref[...]
        m_scr[...] = jnp.zeros(m_scr.shape, F32)
        for e in range(nb):
            for h in range(ML_HEADS):
                m_scr[e * sub + h:e * sub + h + 1, :] = jnp.broadcast_to(m0_ref[e, :, h:h + 1], (1, LANE))

    r = lax.broadcasted_iota(jnp.int32, (L, L), 0)
    c = lax.broadcasted_iota(jnp.int32, (L, L), 1)
    tril = r >= c
    triu = r <= c

    items = [(e, h) for e in range(nb) for h in range(ML_HEADS)]
    nchunk = te // L
    pre = {}
    for ci in range(nchunk):
        for e in range(nb):
            rows = slice(e * te + ci * L, e * te + (ci + 1) * L)
            g = g_ref[rows, :]
            gt = g.T
            for h in range(ML_HEADS):
                ib = bias_ref[:, h:h + 1]
                fb = bias_ref[:, ML_HEADS + h:ML_HEADS + h + 1]
                i_col = g[:, h:h + 1] + ib
                i_row = gt[h:h + 1, 0:L] + ib
                lf_col = _log_sigmoid(g[:, ML_HEADS + h:ML_HEADS + h + 1] + fb)
                lf_row = _log_sigmoid(gt[ML_HEADS + h:ML_HEADS + h + 1, 0:L] + fb)
                b_col = jnp.sum(jnp.where(tril, lf_row, 0.0), axis=1, keepdims=True)
                b_row = jnp.sum(jnp.where(triu, lf_col, 0.0), axis=0, keepdims=True)
                dmat = jnp.where(tril, b_col - b_row + i_row, -jnp.inf)
                q = qk_ref[rows, h * ML_DK:(h + 1) * ML_DK]
                k = qk_ref[rows, ML_HEADS * ML_DK + h * ML_DK:ML_HEADS * ML_DK + (h + 1) * ML_DK] * (ML_DK ** -0.5)
                qb = q.astype(BF16)
                pre[ci, e, h] = dict(
                    rows=rows, i_col=i_col, b_col=b_col, dmat=dmat, dmax=jnp.max(dmat, axis=1, keepdims=True),
                    q=q, k=k, qb=qb, vb=v_ref[rows, h * ML_DV:(h + 1) * ML_DV].astype(BF16),
                    s_qk=lax.dot_general(qb, k.astype(BF16), NT_DIMS, preferred_element_type=F32))

    for ci in range(nchunk):
        mid = {}
        for e, h in items:
            a = pre[ci, e, h]
            m = m_scr[e * sub + h:e * sub + h + 1, 0:1]
            m_t = jnp.maximum(a["b_col"] + m, a["dmax"])
            pm = jnp.exp(a["dmat"] - m_t) * a["s_qk"]
            cst = c_ref[e, h]
            mid[e, h] = dict(m=m, m_t=m_t, pm=pm, cst=cst,
                             pv=jnp.dot(pm.astype(BF16), a["vb"], preferred_element_type=F32),
                             qc=jnp.dot(a["qb"], cst.astype(BF16), preferred_element_type=F32))
        for e, h in items:
            a, u = pre[ci, e, h], mid[e, h]
            m, m_t, b_col = u["m"], u["m_t"], a["b_col"]
            nst = n_ref[e, h:h + 1, :]
            inter = jnp.exp(b_col + m - m_t)
            num = u["pv"] + inter * u["qc"]
            den = jnp.sum(u["pm"], axis=1, keepdims=True) + inter * jnp.sum(a["q"] * nst, axis=1, keepdims=True)
            hh = num / jnp.maximum(jnp.abs(den), jnp.exp(-m_t))
            m_new = m_t[L - 1:L, :]
            b_last = b_col[L - 1:L, :]
            wgt = jnp.exp(b_last - b_col + a["i_col"] - m_new)
            decay = jnp.exp(b_last + m - m_new)
            kw = a["k"] * wgt
            c_ref[e, h] = decay * u["cst"] + lax.dot_general(kw.astype(BF16), a["vb"], TN_DIMS,
                                                             preferred_element_type=F32)
            n_ref[e, h:h + 1, :] = decay * nst + jnp.sum(kw, axis=0, keepdims=True)
            m_scr[e * sub + h:e * sub + h + 1, :] = jnp.broadcast_to(m_new, (1, LANE))
            h_scr[a["rows"], h * ML_DV:(h + 1) * ML_DV] = hh

    for h in range(ML_HEADS):
        sl = slice(h * ML_DV, (h + 1) * ML_DV)
        x = h_scr[:, sl]
        xc = x - jnp.mean(x, axis=1, keepdims=True)
        y = xc * lax.rsqrt(jnp.mean(xc * xc, axis=1, keepdims=True) + 1e-6)
        o_ref[:, sl] = (y * par_ref[:, sl] * _silu(z_ref[:, sl])).astype(BF16)

    @pl.when(t == nt - 1)
    def _():
        for e in range(nb):
            m_ref[e] = m_scr[e * sub:(e + 1) * sub, :]


def _mlstm(p, ps, norm_w, bias, c0, n0, m0, mix, acc_c, acc_n, acc_m, layer, rows, chunk):
    nb = rows.nb
    pspec = lambda cb: rows.spec(GROUP_W, cb)
    c_tail, n_tail, m_tail = (ML_HEADS, ML_DK, ML_DV), (ML_HEADS, ML_DK), (1, ML_HEADS)
    return pl.pallas_call(
        functools.partial(_mlstm_kernel, chunk=chunk, nb=nb, te=rows.te, nt=rows.nt),
        grid=rows.grid,
        in_specs=[pspec(C_MQK // GROUP_W), pspec(C_MV // GROUP_W), pspec(C_MZ // GROUP_W),
                  rows.spec(LANE, CB_MG),
                  pl.BlockSpec((None, 1, GROUP_W), lambda b, t: (layer, 0, 0)),
                  pl.BlockSpec((None, 1, 2 * ML_HEADS), lambda b, t: (layer, 0, 0)),
                  _state_in_spec(c0, layer, c_tail, nb), _state_in_spec(n0, layer, n_tail, nb),
                  _state_in_spec(m0, layer, m_tail, nb), ANY_SPEC, ANY_SPEC, ANY_SPEC, ANY_SPEC],
        out_specs=[_mix_out_spec(rows, 1), _state_out_spec(layer, c_tail, nb), _state_out_spec(layer, n_tail, nb),
                   _state_out_spec(layer, (8, LANE), nb)],
        out_shape=[_sds(mix), _sds(acc_c), _sds(acc_n), _sds(acc_m)],
        input_output_aliases={9: 0, 10: 1, 11: 2, 12: 3},
        scratch_shapes=[pltpu.VMEM((nb * 8, LANE), F32), pltpu.VMEM((rows.rows, GROUP_W), F32)],
        compiler_params=_cparams(("parallel", "arbitrary")),
        name="mlstm",
    )(p, p, p, ps, norm_w, bias, c0, n0, m0, mix, acc_c, acc_n, acc_m)


RW_CHUNK = 16


def _rwkv_kernel(r_ref, k_ref, v_ref, z_ref, x_ref, buf_ref, bufx_ref, mu_ref, mux_ref, par_ref,
                 w2_ref, a2_ref, s0_ref, _mix_ref, _acc_ref, o_ref, s_ref,
                 sp_scr, prev_scr, prevx_scr, kh_scr, rh_scr, ki_scr, ai_scr, kd_scr, ad_scr, vb_scr,
                 gl_scr, y_scr, bonus_scr, coef_scr, mv_scr, cm_scr, *, nb, te, nt):
    t = pl.program_id(1)
    L = RW_CHUNK
    W = GROUP_W
    tc = nb * te
    tep = max(te, L)
    tp = nb * tep

    lane2 = lax.broadcasted_iota(jnp.int32, (LANE, LANE), 1)
    row2 = lax.broadcasted_iota(jnp.int32, (LANE, LANE), 0)
    diag_blocks = (lane2 < RW_HEAD) == (row2 < RW_HEAD)
    seg_ones = jnp.where(diag_blocks, 1.0, 0.0).astype(BF16)

    @pl.when(t == 0)
    def _():
        zero = jnp.zeros((RW_HEAD, RW_HEAD), F32)
        for e in range(nb):
            prev_scr[e:e + 1, :] = buf_ref[e]
            prevx_scr[e:e + 1, :] = bufx_ref[e]
            for pr in range(RW_PAIRS):
                top = jnp.concatenate([s0_ref[e, 2 * pr], zero], axis=1)
                bot = jnp.concatenate([zero, s0_ref[e, 2 * pr + 1]], axis=1)
                sp_scr[e * RW_PAIRS + pr] = jnp.concatenate([top, bot], axis=0)

    def seg_sum(x):
        parts = [_dot_sel_right(x[:, i * LANE:(i + 1) * LANE], seg_ones, 2) for i in range(x.shape[1] // LANE)]
        return jnp.concatenate(parts, axis=1)

    def mix(cur, prev_rows, mu):
        rid = lax.broadcasted_iota(jnp.int32, cur.shape, 0)
        prev = pltpu.roll(cur, 1, axis=0) if tc > 1 else cur
        for e in range(nb):
            prev = jnp.where(rid == e * te, prev_rows[e:e + 1, :], prev)
        return cur + (prev - cur) * mu

    pr_ = r_ref[...]
    pk_ = k_ref[...]
    pv_ = v_ref[...]
    px_ = x_ref[...]
    xr = mix(pr_, prev_scr[:, 0:W], mu_ref[:, 0:W])
    xk = mix(pk_, prev_scr[:, W:2 * W], mu_ref[:, W:2 * W])
    xv = mix(pv_, prev_scr[:, 2 * W:3 * W], mu_ref[:, 2 * W:3 * W])
    xx = mix(px_, prevx_scr[...], mux_ref[...])
    for e in range(nb):
        last = slice((e + 1) * te - 1, (e + 1) * te)
        prev_scr[e:e + 1, 0:W] = pr_[last, :]
        prev_scr[e:e + 1, W:2 * W] = pk_[last, :]
        prev_scr[e:e + 1, 2 * W:3 * W] = pv_[last, :]
        prevx_scr[e:e + 1, :] = px_[last, :]

    w0 = par_ref[0:1, :]
    a0 = par_ref[1:2, :]
    k_k = par_ref[2:3, :]
    k_a = par_ref[3:4, :]
    r_k = par_ref[4:5, :]
    ln_w = par_ref[5:6, :]
    ln_b = par_ref[6:7, :]

    wlin = w0 + jnp.dot(jnp.tanh(xx).astype(BF16), w2_ref[...], preferred_element_type=F32)
    wdec = -_softplus(-wlin) - 0.5
    logd = -jnp.exp(wdec)
    a = _sigmoid(a0 + jnp.dot(xx.astype(BF16), a2_ref[...], preferred_element_type=F32))
    kk = xk * k_k
    kk = kk / jnp.maximum(jnp.sqrt(seg_sum(kk * kk)), 1e-12)
    kp = xk * (1.0 + (a - 1.0) * k_a)
    alpha = a * kk
    bonus = seg_sum(xr * kp * r_k) * xv

    if tep > te:
        def pad(u):
            zeros = jnp.zeros((tep - te, u.shape[1]), F32)
            return jnp.concatenate([piece for e in range(nb) for piece in (u[e * te:(e + 1) * te], zeros)], axis=0)
        logd, kk, kp, alpha, xr, xv = pad(logd), pad(kk), pad(kp), pad(alpha), pad(xr), pad(xv)
    bonus_scr[...] = bonus

    tri, ones = _chunk_masks(tp, L)
    g = _dot_sel_left(tri, logd, 3)
    gl = _dot_sel_left(ones, logd, 3)
    einv = jnp.exp(-g)
    egl = jnp.exp(gl - g)
    kh_scr[...] = (kk * jnp.exp(g - logd)).astype(BF16)
    rh_scr[...] = (xr * jnp.exp(g)).astype(BF16)
    ki_scr[...] = (kp * einv).astype(BF16)
    ai_scr[...] = (alpha * einv).astype(BF16)
    kd_scr[...] = (kp * egl).astype(BF16)
    ad_scr[...] = (alpha * egl).astype(BF16)
    vb_scr[...] = xv.astype(BF16)
    gl_scr[...] = jnp.exp(gl)

    lane_l = lax.broadcasted_iota(jnp.int32, (L, LANE), 1)
    head_a = lane_l < RW_HEAD
    rl = lax.broadcasted_iota(jnp.int32, (L, L), 0)
    cl = lax.broadcasted_iota(jnp.int32, (L, L), 1)
    incl = rl >= cl
    rl2 = lax.broadcasted_iota(jnp.int32, (2 * L, L), 0) & (L - 1)
    strict2 = rl2 > lax.broadcasted_iota(jnp.int32, (2 * L, L), 1)
    cl_ab = lax.broadcasted_iota(jnp.int32, (L, 2 * L), 1) & (L - 1)
    strict_ab = lax.broadcasted_iota(jnp.int32, (L, 2 * L), 0) > cl_ab
    sp_r = lax.broadcasted_iota(jnp.int32, (2 * L, (L - 1) * LANE), 0)
    sp_c = lax.broadcasted_iota(jnp.int32, (2 * L, (L - 1) * LANE), 1)
    sp_head = jnp.where((sp_c & (LANE - 1)) < RW_HEAD, 0, L)
    spread = jnp.where(sp_r == lax.shift_right_logical(sp_c, 7) + sp_head, 1.0, 0.0).astype(BF16)
    zb = jnp.zeros((L, LANE), BF16)

    pairs = range(nb * RW_PAIRS)
    lanes = [slice((i % RW_PAIRS) * LANE, (i % RW_PAIRS + 1) * LANE) for i in pairs]
    grp = LANE // (2 * L)
    zblk = jnp.zeros((2 * L, LANE), BF16)

    def chunk_rows(c):
        starts = []
        for e in range(nb):
            r0 = e * tep + c * L
            starts.append(r0 if isinstance(c, int) else pl.multiple_of(r0, L))
        return starts, [pl.ds(starts[i // RW_PAIRS], L) for i in pairs]


    def free_first(c):
        _, rws = chunk_rows(c)
        gm, nab = [], []
        for pr in pairs:
            sl, rows = lanes[pr], rws[pr]
            kh = kh_scr[rows, sl]
            rh = rh_scr[rows, sl]
            x4 = jnp.concatenate([jnp.where(head_a, kh, zb), jnp.where(head_a, zb, kh),
                                  jnp.where(head_a, rh, zb), jnp.where(head_a, zb, rh)], axis=0)
            ai = ai_scr[rows, sl]
            y2 = jnp.concatenate([ai, ki_scr[rows, sl]], axis=0)
            gm.append(lax.dot_general(x4, y2, NT_DIMS, preferred_element_type=F32))
            ai2 = jnp.concatenate([jnp.where(head_a, ai, zb), jnp.where(head_a, zb, ai)], axis=0)
            nab.append(lax.dot_general(kh, ai2, NT_DIMS, preferred_element_type=F32))
        return rws, gm, nab

    def free_second(first, slot):
        rws, gm, nab = first
        for pr in pairs:
            g = gm[pr]
            m_ab = jnp.where(strict2, g[0:2 * L, L:2 * L], 0.0)
            mv = jnp.dot(m_ab.astype(BF16), vb_scr[rws[pr], lanes[pr]], preferred_element_type=F32)
            mv_scr[slot, pr] = jnp.where(head_a, mv[0:L], mv[L:2 * L])
            coef_scr[slot, pr] = _dot_sel_right(jnp.where(strict_ab, nab[pr], 0.0), spread, 1)
            cm = jnp.concatenate([
                jnp.concatenate([jnp.where(incl, g[2 * L:3 * L, L:2 * L], 0.0),
                                 -jnp.where(incl, g[2 * L:3 * L, 0:L], 0.0)], axis=1),
                jnp.concatenate([jnp.where(incl, g[3 * L:4 * L, L:2 * L], 0.0),
                                 -jnp.where(incl, g[3 * L:4 * L, 0:L], 0.0)], axis=1)], axis=0)
            cm_scr[slot, pr] = cm.astype(BF16)

    def stage_state_head(c):
        _, rws = chunk_rows(c)
        return [lax.dot_general(jnp.concatenate([kh_scr[rws[pr], lanes[pr]], rh_scr[rws[pr], lanes[pr]]], axis=0),
                                sp_scr[pr].astype(BF16), NT_DIMS, preferred_element_type=F32)
                for pr in pairs]

    def state_solve(c, slot, ks):
        starts, rws = chunk_rows(c)
        vw = []
        for pr in pairs:
            w = ks[pr][0:L] + mv_scr[slot, pr]
            for s in range(L - 1):
                w = w - coef_scr[slot, pr, :, s * LANE:(s + 1) * LANE] * w[s:s + 1, :]
            vw.append(jnp.concatenate([vb_scr[rws[pr], lanes[pr]], w.astype(BF16)], axis=0))
        yy = [jnp.dot(cm_scr[slot, pr], vw[pr], preferred_element_type=F32) for pr in pairs]
        upd = []
        for g0 in range(0, nb * RW_PAIRS, grp):
            members = range(g0, g0 + grp)
            vw_t = jnp.concatenate([vw[pr] for pr in members], axis=0).T
            kd_rows = []
            for q, pr in enumerate(members):
                sl, rows = lanes[pr], rws[pr]
                kd_ad = jnp.concatenate([kd_scr[rows, sl], -ad_scr[rows, sl]], axis=0)
                kd_rows.append(jnp.concatenate([kd_ad if col == q else zblk for col in range(grp)], axis=1))
            u_all = jnp.dot(vw_t, jnp.concatenate(kd_rows, axis=0), preferred_element_type=F32)
            for q in range(grp):
                upd.append(u_all[:, q * LANE:(q + 1) * LANE])
        return starts, rws, ks, yy, upd

    def state_store(solved):
        starts, rws, ks, yy, upd = solved
        for pr in pairs:
            sl, rows = lanes[pr], rws[pr]
            y_scr[rows, sl] = ks[pr][L:2 * L] + jnp.where(head_a, yy[pr][0:L], yy[pr][L:2 * L])
            gl_row = gl_scr[pl.ds(starts[pr // RW_PAIRS], 1), sl]
            sp_scr[pr] = sp_scr[pr] * gl_row + jnp.where(diag_blocks, upd[pr], 0.0)

    def chunk_step(c, slot, c_next):
        ks = stage_state_head(c)
        first = free_first(c_next)
        solved = state_solve(c, slot, ks)
        free_second(first, 1 - slot)
        state_store(solved)

    nchunk = tep // L
    free_second(free_first(0), 0)
    if nchunk == 1:
        state_store(state_solve(0, 0, stage_state_head(0)))
    else:
        assert nchunk % 2 == 0

        def two_chunks(cc, carry):
            c0 = cc * 2
            chunk_step(c0, 0, c0 + 1)
            chunk_step(c0 + 1, 1, jnp.minimum(c0 + 2, nchunk - 1))
            return carry

        lax.fori_loop(0, nchunk // 2, two_chunks, 0)

    if tep > te:
        y = jnp.concatenate([y_scr[e * tep:e * tep + te, :] for e in range(nb)], axis=0)
    else:
        y = y_scr[...]
    yc = y - seg_sum(y) * (1.0 / RW_HEAD)
    yn = yc * lax.rsqrt(seg_sum(yc * yc) * (1.0 / RW_HEAD) + RW_GN_EPS)
    out = (yn * ln_w + ln_b + bonus_scr[...]) * _silu(z_ref[...])
    o_ref[...] = out.astype(BF16)

    @pl.when(t == nt - 1)
    def _():
        for e in range(nb):
            for pr in range(RW_PAIRS):
                sp = sp_scr[e * RW_PAIRS + pr]
                s_ref[e, 2 * pr] = sp[0:RW_HEAD, 0:RW_HEAD]
                s_ref[e, 2 * pr + 1] = sp[RW_HEAD:LANE, RW_HEAD:LANE]


def _rwkv(p, ps, buf, bufx, mu, mux, par, w2p, a2p, state, mix, acc, layer, rows):
    nb, te = rows.nb, rows.te
    tp = nb * max(te, RW_CHUNK)
    pspec = lambda cb: rows.spec(GROUP_W, cb)
    tail = (RW_HEADS, RW_HEAD, RW_HEAD)
    lay = lambda shape: pl.BlockSpec((None,) + shape, lambda b, t: (layer,) + (0,) * len(shape))
    big = lambda dt: pltpu.VMEM((tp, GROUP_W), dt)
    return pl.pallas_call(
        functools.partial(_rwkv_kernel, nb=nb, te=te, nt=rows.nt),
        grid=rows.grid,
        in_specs=[pspec(C_RR // GROUP_W), pspec(C_RK // GROUP_W), pspec(C_RV // GROUP_W), pspec(C_RZ // GROUP_W),
                  rows.spec(LANE, CB_RX),
                  pl.BlockSpec((nb, 1, 3 * GROUP_W), lambda g, t: (g, 0, 0)),
                  pl.BlockSpec((nb, 1, LANE), lambda g, t: (g, 0, 0)),
                  lay((1, 3 * GROUP_W)), lay((1, LANE)), lay((8, GROUP_W)),
                  lay((LANE, GROUP_W)), lay((LANE, GROUP_W)), _state_in_spec(state, layer, tail, nb),
                  ANY_SPEC, ANY_SPEC],
        out_specs=[_mix_out_spec(rows, 2), _state_out_spec(layer, tail, nb)],
        out_shape=[_sds(mix), _sds(acc)],
        input_output_aliases={13: 0, 14: 1},
        scratch_shapes=[pltpu.VMEM((nb * RW_PAIRS, LANE, LANE), F32),
                        pltpu.VMEM((nb, 3 * GROUP_W), F32), pltpu.VMEM((nb, LANE), F32),
                        big(BF16), big(BF16), big(BF16), big(BF16), big(BF16), big(BF16), big(BF16),
                        big(F32), big(F32), pltpu.VMEM((rows.rows, GROUP_W), F32),
                        pltpu.VMEM((2, nb * RW_PAIRS, RW_CHUNK, (RW_CHUNK - 1) * LANE), F32),
                        pltpu.VMEM((2, nb * RW_PAIRS, RW_CHUNK, LANE), F32),
                        pltpu.VMEM((2, nb * RW_PAIRS, 2 * RW_CHUNK, 2 * RW_CHUNK), BF16)],
        compiler_params=_cparams(("parallel", "arbitrary")),
        name="rwkv7",
    )(p, p, p, p, ps, buf, bufx, mu, mux, par, w2p, a2p, state, mix, acc)


def _trunk_layer(l, x_f32, x_bf, w_in_t, w_out_b, hg_par, ml_nw, ml_bias, rw_mu, rw_mux, rw_par, rw_w2, rw_a2,
                 ln_g, ln_b, groups, cfg):
    p = _proj_matmul(x_bf, w_in_t, l, cfg["mm_tm"], cfg["mm_tn"])
    ps = _proj_small(x_bf, w_in_t, l, cfg["mm_tm"])
    mix = jnp.zeros(x_bf.shape, BF16)
    for grp in groups:
        row0, batch, seq, te = grp["row0"], grp["batch"], grp["seq"], grp["te"]
        rows = lambda nb: _Rows(row0, batch, seq, te, nb)
        mix, grp["acc_hg"] = _hgrn(p, hg_par, grp["hg"], mix, grp["acc_hg"], l, rows(grp["hg_nb"]), grp["hg_chunk"])
        mix, grp["acc_c"], grp["acc_n"], grp["acc_m"] = _mlstm(
            p, ps, ml_nw, ml_bias, grp["ml_c"], grp["ml_n"], grp["ml_m"], mix, grp["acc_c"], grp["acc_n"],
            grp["acc_m"], l, rows(grp["ml_nb"]), grp["ml_chunk"])
        mix, grp["acc_rw"] = _rwkv(p, ps, grp["rw_buf"][l], grp["rw_bufx"][l], rw_mu, rw_mux, rw_par, rw_w2, rw_a2,
                                   grp["rw"], mix, grp["acc_rw"], l, rows(grp["rw_nb"]))
        if grp["cache_layout"]:
            mix = _mem_attn_cache(p, grp["mem_k"], grp["mem_v"], mix, l, rows(grp["xa_nb"]))
        else:
            mix = _mem_attn(p, grp["mem_k"], grp["mem_v"], mix, l, rows(grp["xa_nb"]))
        first = row0 + (seq // te - 1) * batch * te + te - 1
        last_rkv = lax.slice(p, (first, C_RR), (row0 + batch * seq, C_RR + 3 * GROUP_W), (te, 1))
        last_x = lax.slice(ps, (first, CB_RX * LANE), (row0 + batch * seq, (CB_RX + 1) * LANE), (te, 1))
        grp["buf_new"].append(jnp.concatenate([last_rkv, last_x], axis=-1))
    x_f32, x_bf = _outproj_then_ln(mix, w_out_b, l, x_f32, ln_g, ln_b, cfg["op_tm"], cfg["op_tn"], cfg["ln_rows"])
    return x_f32, x_bf


def kernel(x_prompt, x_sample, mem_prompt, state_hgrn, state_mlstm_C, state_mlstm_n, state_mlstm_m, state_rwkv, state_rwkv_shift, cache_mem_k, cache_mem_v, w_in, hgrn_lb, hgrn_norm_w, mlstm_ig_b, mlstm_fg_b, mlstm_norm_w, rwkv_mu, rwkv_w0, rwkv_w2, rwkv_a0, rwkv_a2, rwkv_k_k, rwkv_k_a, rwkv_r_k, rwkv_ln_w, rwkv_ln_b, mem_wk, mem_wv, w_out, ln_g, ln_b):
    bp, tp_, _ = x_prompt.shape
    bs, ts, _ = x_sample.shape
    depth = w_in.shape[0]
    mp, ms = bp * tp_, bs * ts

    w_in_t = jnp.swapaxes(w_in, 1, 2)
    w_out_b = w_out.astype(BF16)
    lb_all = jnp.cumsum(jax.nn.softmax(hgrn_lb.astype(F32), axis=0), axis=0)
    lb_all = lb_all - lb_all[0]
    zrow = jnp.zeros_like(lb_all)
    hg_par = jnp.stack([jnp.log(lb_all), jnp.log1p(-lb_all), 1.0 - lb_all, hgrn_norm_w.astype(F32),
                        zrow, zrow, zrow, zrow], axis=1)
    ml_nw = mlstm_norm_w.astype(F32)[:, None, :]
    ml_bias = jnp.concatenate([mlstm_ig_b, mlstm_fg_b], axis=-1).astype(F32)[:, None, :]
    rw_mu = rwkv_mu[:, None, :3 * GROUP_W].astype(F32)
    rw_mux = rwkv_mu[:, None, 3 * GROUP_W:].astype(F32)
    zr = jnp.zeros((depth, GROUP_W), F32)
    rw_par = jnp.stack([rwkv_w0, rwkv_a0, rwkv_k_k, rwkv_k_a, rwkv_r_k.reshape(depth, GROUP_W), rwkv_ln_w,
                        rwkv_ln_b, zr], axis=1).astype(F32)
    zl = jnp.zeros((depth, RW_LORA, GROUP_W), F32)
    rw_w2 = jnp.concatenate([rwkv_w2.astype(F32), zl], axis=1).astype(BF16)
    rw_a2 = jnp.concatenate([zl, rwkv_a2.astype(F32)], axis=1).astype(BF16)
    ln_g3 = ln_g.astype(F32)[:, None, :]
    ln_b3 = ln_b.astype(F32)[:, None, :]

    mem_x = mem_prompt.reshape(bp * N_MEM, D_MODEL).astype(BF16)
    mk_p = _matmul_layers(mem_x, mem_wk, 256)
    mv_p = _matmul_layers(mem_x, mem_wv, 256)
    mk_out = mk_p.reshape(depth, bp, N_MEM, XA_HEADS, XA_DH)
    mv_out = mv_p.reshape(depth, bp, N_MEM, XA_HEADS, XA_DH)

    def split_buf(buf):
        return buf[:, :, None, :3 * GROUP_W].astype(F32), buf[:, :, None, 3 * GROUP_W:].astype(F32)

    def cache_view(c):
        c = c.reshape(depth, bs, N_MEM, XA_HEADS, XA_SUB, LANE)
        return jnp.transpose(c, (0, 1, 2, 4, 3, 5)).reshape(depth, bs * XA_ROWS, LANE)

    def results(b):
        return dict(acc_hg=jnp.zeros((depth, b, HG_HEADS, HG_D, HG_D), F32),
                    acc_c=jnp.zeros((depth, b, ML_HEADS, ML_DK, ML_DV), F32),
                    acc_n=jnp.zeros((depth, b, ML_HEADS, ML_DK), F32),
                    acc_m=jnp.zeros((depth, b, 8, LANE), F32),
                    acc_rw=jnp.zeros((depth, b, RW_HEADS, RW_HEAD, RW_HEAD), F32), buf_new=[])

    zbuf, zbufx = split_buf(jnp.zeros((depth, bp, RW_SHIFT_W), F32))
    sbuf, sbufx = split_buf(state_rwkv_shift)
    te_p = 128
    prompt = dict(row0=0, batch=bp, seq=tp_, te=te_p, hg_nb=1, ml_nb=4, xa_nb=4,
                  hg=jnp.zeros((bp, HG_HEADS, HG_D, HG_D), F32), hg_chunk=16,
                  ml_c=jnp.zeros((bp, ML_HEADS, ML_DK, ML_DV), F32), ml_n=jnp.zeros((bp, ML_HEADS, ML_DK), F32),
                  ml_m=jnp.zeros((bp, 1, ML_HEADS), F32), ml_chunk=64,
                  rw=jnp.zeros((bp, RW_HEADS, RW_HEAD, RW_HEAD), F32), rw_buf=zbuf, rw_bufx=zbufx, rw_nb=2,
                  mem_k=mk_p, mem_v=mv_p, cache_layout=False, **results(bp))
    sample = dict(row0=mp, batch=bs, seq=ts, te=ts, hg_nb=4, ml_nb=4, xa_nb=4,
                  hg=state_hgrn, hg_chunk=ts,
                  ml_c=state_mlstm_C, ml_n=state_mlstm_n, ml_m=state_mlstm_m[:, :, None, :], ml_chunk=ts,
                  rw=state_rwkv, rw_buf=sbuf, rw_bufx=sbufx, rw_nb=4,
                  mem_k=cache_view(cache_mem_k), mem_v=cache_view(cache_mem_v), cache_layout=True, **results(bs))
    cfg = dict(mm_tm=1536, mm_tn=512, op_tm=1536, op_tn=512, ln_rows=512)

    nblk = tp_ // te_p
    xp = jnp.transpose(x_prompt.reshape(bp, nblk, te_p, D_MODEL), (1, 0, 2, 3)).reshape(mp, D_MODEL)
    x_f32 = jnp.concatenate([xp, x_sample.reshape(ms, D_MODEL)], axis=0).astype(F32)
    x_bf = x_f32.astype(BF16)
    for l in range(depth):
        x_f32, x_bf = _trunk_layer(l, x_f32, x_bf, w_in_t, w_out_b, hg_par, ml_nw, ml_bias, rw_mu, rw_mux,
                                   rw_par, rw_w2, rw_a2, ln_g3, ln_b3, [prompt, sample], cfg)

    def states(g):
        return (g["acc_hg"], g["acc_c"], g["acc_n"], g["acc_m"][:, :, :ML_HEADS, 0], g["acc_rw"],
                jnp.stack(g["buf_new"], axis=0))

    y_prompt = jnp.transpose(x_f32[:mp].reshape(nblk, bp, te_p, D_MODEL), (1, 0, 2, 3)).reshape(bp, tp_, D_MODEL)
    y_sample = x_f32[mp:].reshape(bs, ts, D_MODEL)
    return (y_prompt, y_sample) + states(prompt) + (mk_out, mv_out) + states(sample)
```

```python
import functools
import math

import jax
import jax.numpy as jnp
from jax import lax
from jax.experimental import pallas as pl
from jax.experimental.pallas import tpu as pltpu

F32 = jnp.float32
BF16 = jnp.bfloat16
HI = lax.Precision.HIGHEST

D_MODEL = 4096
DEPTH = 4
GROUP_W = D_MODEL // 4
N_MEM = 256
HG_HEADS, HG_D = 8, 128
ML_HEADS, ML_DK, ML_DV = 4, 128, 256
RW_HEADS, RW_HEAD, RW_LORA = 16, 64, 64
RW_PAIRS = RW_HEADS // 2
XA_HEADS, XA_DH = 4, 256
RW_SHIFT_W = 3 * GROUP_W + 2 * RW_LORA
N_IN = 13448
DN_ALPHA = (2.0 * DEPTH) ** 0.25
LN_EPS = 1e-5
RW_GN_EPS = 64e-5

C_HQ, C_HF, C_HI, C_HZ = 0, 1024, 2048, 3072
C_MQK, C_MV, C_MZ = 4096, 5120, 6144
C_RR, C_RK, C_RV, C_RZ = 7168, 8192, 9216, 10240
C_XQ, C_XZ = 11264, 12288
NP_MAIN = 13312
ORIG_MG, ORIG_MZ, ORIG_RX, ORIG_RZ = 6144, 6152, 10248, 10376
CB_RX, CB_MG = 0, 1
LANE = 128

VMEM_LIMIT = 60 * 1024 * 1024

NT_DIMS = (((1,), (1,)), ((), ()))
TN_DIMS = (((0,), (0,)), ((), ()))


def _cparams(sem):
    return pltpu.CompilerParams(dimension_semantics=sem, vmem_limit_bytes=VMEM_LIMIT)


def _sigmoid(x):
    return jax.nn.sigmoid(x)


def _silu(x):
    return x * _sigmoid(x)


def _log_sigmoid(x):
    return jnp.minimum(x, 0.0) - jnp.log1p(jnp.exp(-jnp.abs(x)))


def _softplus(x):
    return jnp.maximum(x, 0.0) + jnp.log1p(jnp.exp(-jnp.abs(x)))


def _chunk_masks(n, chunk):
    r = lax.broadcasted_iota(jnp.int32, (n, n), 0)
    c = lax.broadcasted_iota(jnp.int32, (n, n), 1)
    sh = int(math.log2(chunk))
    same = lax.shift_right_logical(r, sh) == lax.shift_right_logical(c, sh)
    tri = jnp.where(same & (c <= r), 1.0, 0.0).astype(BF16)
    ones = jnp.where(same, 1.0, 0.0).astype(BF16)
    return tri, ones


def _split_bf16(x, terms):
    parts = []
    for i in range(terms):
        part = x.astype(BF16)
        parts.append(part)
        if i + 1 < terms:
            x = x - part.astype(F32)
    return parts


def _dot_sel_left(sel, x, terms):
    return sum(jnp.dot(sel, part, preferred_element_type=F32) for part in _split_bf16(x, terms))


def _dot_sel_right(x, sel, terms):
    return sum(jnp.dot(part, sel, preferred_element_type=F32) for part in _split_bf16(x, terms))


def _chunk_cumsum(x, chunk):
    rid = lax.broadcasted_iota(jnp.int32, x.shape, 0) & (chunk - 1)
    step = 1
    while step < chunk:
        x = x + jnp.where(rid >= step, pltpu.roll(x, step, axis=0), 0.0)
        step *= 2
    return x


def _mm_kernel(x_ref, w_ref, o_ref):
    o_ref[...] = jnp.dot(x_ref[...], w_ref[...].astype(BF16), preferred_element_type=F32)


def _mm_nt_kernel(x_ref, w_ref, o_ref):
    o_ref[...] = lax.dot_general(x_ref[...], w_ref[0].astype(BF16), NT_DIMS, preferred_element_type=F32)


def _proj_matmul(x, w_t, layer, tm, tn):
    m, k = x.shape

    def w_index(i, j):
        col = j * tn
        off = jnp.where(col >= C_RZ, ORIG_RZ - C_RZ, jnp.where(col >= C_MZ, ORIG_MZ - C_MZ, 0))
        return (layer, pl.multiple_of(col + off, 8), 0)

    return pl.pallas_call(
        _mm_nt_kernel,
        grid=(m // tm, NP_MAIN // tn),
        in_specs=[pl.BlockSpec((tm, k), lambda i, j: (i, 0)),
                  pl.BlockSpec((pl.Element(1), pl.Element(tn), pl.Element(k)), w_index)],
        out_specs=pl.BlockSpec((tm, tn), lambda i, j: (i, j)),
        out_shape=jax.ShapeDtypeStruct((m, NP_MAIN), F32),
        compiler_params=_cparams(("parallel", "arbitrary")),
        name="proj_matmul",
    )(x, w_t)


def _proj_small(x, w_t, layer, tm):
    m, k = x.shape

    def kernel(x_ref, wa_ref, wb_ref, o_ref):
        w = jnp.concatenate([wa_ref[0], wb_ref[0]], axis=0).astype(BF16)
        o_ref[...] = lax.dot_general(x_ref[...], w, NT_DIMS, preferred_element_type=F32)

    wspec = lambda row: pl.BlockSpec((pl.Element(1), pl.Element(LANE), pl.Element(k)), lambda i: (layer, row, 0))
    return pl.pallas_call(
        kernel,
        grid=(m // tm,),
        in_specs=[pl.BlockSpec((tm, k), lambda i: (i, 0)), wspec(ORIG_RX), wspec(ORIG_MG)],
        out_specs=pl.BlockSpec((tm, 2 * LANE), lambda i: (i, 0)),
        out_shape=jax.ShapeDtypeStruct((m, 2 * LANE), F32),
        compiler_params=_cparams(("parallel",)),
        name="proj_small",
    )(x, w_t, w_t)


def _matmul_layers(x, w, tn):
    m, k = x.shape
    depth, _, n = w.shape
    return pl.pallas_call(
        _mm_kernel,
        grid=(depth, n // tn),
        in_specs=[pl.BlockSpec((m, k), lambda l, j: (0, 0)),
                  pl.BlockSpec((None, k, tn), lambda l, j: (l, 0, j))],
        out_specs=pl.BlockSpec((None, m, tn), lambda l, j: (l, 0, j)),
        out_shape=jax.ShapeDtypeStruct((depth, m, n), F32),
        compiler_params=_cparams(("parallel", "arbitrary")),
        name="mem_kv_matmul",
    )(x, w)


def _outproj_mm_kernel(mix_ref, w_ref, x_ref, o_ref):
    o_ref[...] = DN_ALPHA * x_ref[...] + jnp.dot(mix_ref[...], w_ref[...], preferred_element_type=F32)


def _layer_norm_kernel(y_ref, g_ref, b_ref, xo_ref, xb_ref):
    y = y_ref[...]
    yc = y - jnp.mean(y, axis=1, keepdims=True)
    out = yc * lax.rsqrt(jnp.mean(yc * yc, axis=1, keepdims=True) + LN_EPS) * g_ref[...] + b_ref[...]
    xo_ref[...] = out
    xb_ref[...] = out.astype(BF16)


def _outproj_then_ln(mix, w_out, layer, x, ln_g, ln_b, tm, tn, tr):
    m = x.shape[0]
    y = pl.pallas_call(
        _outproj_mm_kernel,
        grid=(m // tm, D_MODEL // tn),
        in_specs=[pl.BlockSpec((tm, D_MODEL), lambda i, j: (i, 0)),
                  pl.BlockSpec((None, D_MODEL, tn), lambda i, j: (layer, 0, j)),
                  pl.BlockSpec((tm, tn), lambda i, j: (i, j))],
        out_specs=pl.BlockSpec((tm, tn), lambda i, j: (i, j)),
        out_shape=jax.ShapeDtypeStruct((m, D_MODEL), F32),
        compiler_params=_cparams(("parallel", "arbitrary")),
        name="outproj_mm",
    )(mix, w_out, x)
    row_spec = pl.BlockSpec((None, 1, D_MODEL), lambda i: (layer, 0, 0))
    blk = pl.BlockSpec((tr, D_MODEL), lambda i: (i, 0))
    return pl.pallas_call(
        _layer_norm_kernel,
        grid=(m // tr,),
        in_specs=[blk, row_spec, row_spec],
        out_specs=[blk, blk],
        out_shape=[jax.ShapeDtypeStruct((m, D_MODEL), F32), jax.ShapeDtypeStruct((m, D_MODEL), BF16)],
        compiler_params=_cparams(("parallel",)),
        name="layer_norm",
    )(y, ln_g, ln_b)


def _attn_kernel(q_ref, z_ref, k_ref, v_ref, _mix_ref, o_ref, *, nb, te):
    items = [(slice(e * te, (e + 1) * te), slice(e * N_MEM, (e + 1) * N_MEM), slice(h * XA_DH, (h + 1) * XA_DH))
             for e in range(nb) for h in range(XA_HEADS)]
    scores = [lax.dot_general(q_ref[qr, sl].astype(BF16), k_ref[mr, sl].astype(BF16), NT_DIMS,
                              preferred_element_type=F32) * (XA_DH ** -0.5)
              for qr, mr, sl in items]
    outs = []
    for (qr, mr, sl), s in zip(items, scores):
        ex = jnp.exp(s - jnp.max(s, axis=1, keepdims=True))
        pr = ex / jnp.sum(ex, axis=1, keepdims=True)
        outs.append(jnp.dot(pr.astype(BF16), v_ref[mr, sl].astype(BF16), preferred_element_type=F32))
    for (qr, mr, sl), o in zip(items, outs):
        o_ref[qr, sl] = (o * _silu(z_ref[qr, sl])).astype(BF16)


def _mem_attn(p, mem_k, mem_v, mix, layer, rows):
    pspec = lambda cb: rows.spec(GROUP_W, cb)
    kvspec = pl.BlockSpec((None, rows.nb * N_MEM, GROUP_W), lambda g, t: (layer, g, 0))
    return pl.pallas_call(
        functools.partial(_attn_kernel, nb=rows.nb, te=rows.te),
        grid=rows.grid,
        in_specs=[pspec(C_XQ // GROUP_W), pspec(C_XZ // GROUP_W), kvspec, kvspec, ANY_SPEC],
        out_specs=_mix_out_spec(rows, 3),
        out_shape=_sds(mix),
        input_output_aliases={4: 0},
        compiler_params=_cparams(("parallel", "arbitrary")),
        name="mem_attn",
    )(p, p, mem_k, mem_v, mix)


XA_SUB = XA_DH // LANE
XA_ROWS = N_MEM * XA_SUB * XA_HEADS


def _attn_cache_kernel(q_ref, z_ref, k_ref, v_ref, _mix_ref, o_ref, *, nb, te):
    t = te
    tiles = XA_HEADS * XA_SUB
    col = lax.broadcasted_iota(jnp.int32, (t, XA_ROWS), 1)
    col_head = col & (XA_HEADS - 1)
    col_sub = lax.shift_right_logical(col, 2) & (XA_SUB - 1)
    sub_all = lax.shift_right_logical(lax.broadcasted_iota(jnp.int32, (XA_HEADS * t, XA_ROWS), 1), 2) & (XA_SUB - 1)
    scores = []
    for e in range(nb):
        qr = slice(e * t, (e + 1) * t)
        qx = jnp.concatenate([q_ref[qr, c * LANE:(c + 1) * LANE] for c in range(tiles)], axis=0).astype(BF16)
        kx = k_ref[e * XA_ROWS:(e + 1) * XA_ROWS, :].astype(BF16)
        scores.append(lax.dot_general(qx, kx, NT_DIMS, preferred_element_type=F32))
    outs = []
    for e in range(nb):
        s_all = scores[e]
        probs = []
        for h in range(XA_HEADS):
            mine = col_head == h
            part = jnp.zeros((t, XA_ROWS), F32)
            for s in range(XA_SUB):
                rows = slice((h * XA_SUB + s) * t, (h * XA_SUB + s + 1) * t)
                part = part + jnp.where(mine & (col_sub == s), s_all[rows], 0.0)
            other = jnp.where(col_sub == 0, pltpu.roll(part, XA_ROWS - XA_HEADS, axis=1),
                              pltpu.roll(part, XA_HEADS, axis=1))
            sc = jnp.where(mine, (part + other) * (XA_DH ** -0.5), -jnp.inf)
            ex = jnp.exp(sc - jnp.max(sc, axis=1, keepdims=True))
            probs.append(ex / (jnp.sum(ex, axis=1, keepdims=True) * (1.0 / XA_SUB)))
        pr = jnp.concatenate(probs, axis=0)
        vb = v_ref[e * XA_ROWS:(e + 1) * XA_ROWS, :].astype(BF16)
        outs.append([jnp.dot(jnp.where(sub_all == s, pr, 0.0).astype(BF16), vb, preferred_element_type=F32)
                     for s in range(XA_SUB)])
    for e in range(nb):
        qr = slice(e * t, (e + 1) * t)
        for h in range(XA_HEADS):
            for s in range(XA_SUB):
                sl = slice((h * XA_SUB + s) * LANE, (h * XA_SUB + s + 1) * LANE)
                o_ref[qr, sl] = (outs[e][s][h * t:(h + 1) * t] * _silu(z_ref[qr, sl])).astype(BF16)


def _mem_attn_cache(p, cache_k, cache_v, mix, layer, rows):
    assert XA_SUB == 2 and XA_HEADS == 4 and rows.nt == 1
    pspec = lambda cb: rows.spec(GROUP_W, cb)
    kvspec = pl.BlockSpec((None, rows.nb * XA_ROWS, LANE), lambda g, t: (layer, g, 0))
    return pl.pallas_call(
        functools.partial(_attn_cache_kernel, nb=rows.nb, te=rows.te),
        grid=rows.grid,
        in_specs=[pspec(C_XQ // GROUP_W), pspec(C_XZ // GROUP_W), kvspec, kvspec, ANY_SPEC],
        out_specs=_mix_out_spec(rows, 3),
        out_shape=_sds(mix),
        input_output_aliases={4: 0},
        compiler_params=_cparams(("parallel", "arbitrary")),
        name="mem_attn_cache",
    )(p, p, cache_k, cache_v, mix)


def _hgrn_kernel(q_ref, f_ref, i_ref, z_ref, par_ref, s0_ref, _mix_ref, _acc_ref, o_ref, s_ref,
                 st_scr, b_scr, qs_scr, kk_scr, qt_scr, vb_scr, h_scr, *, chunk, nb, te, nt):
    t = pl.program_id(1)
    d = HG_D
    sub = 8
    transposed = nt > 1

    @pl.when(t == 0)
    def _():
        for e in range(nb):
            for h in range(HG_HEADS):
                st_scr[e * HG_HEADS + h] = s0_ref[e, h].T if transposed else s0_ref[e, h]

    log_lb = par_ref[0:1, :]
    log1m_lb = par_ref[1:2, :]
    one_m_lb = par_ref[2:3, :]
    norm_w = par_ref[3:4, :]

    fpre = f_ref[...]
    bt = log1m_lb + _log_sigmoid(fpre)
    logf = jnp.maximum(log_lb, bt) + jnp.log1p(jnp.exp(-jnp.abs(log_lb - bt)))
    b = _chunk_cumsum(logf, chunk)
    qs = _silu(q_ref[...])
    b_scr[...] = b
    qs_scr[...] = qs
    kk_scr[...] = one_m_lb * _sigmoid(-fpre)
    qt_scr[...] = (qs * jnp.exp(b)).astype(BF16)
    vb_scr[...] = i_ref[...].astype(BF16)

    rid = lax.broadcasted_iota(jnp.int32, (sub, d), 0)

    def intra(bc, qc, kc, vc):
        blocks = []
        for rb in range(chunk // sub):
            rs = slice(rb * sub, (rb + 1) * sub)
            bb, qb = bc[rs], qc[rs]
            o = jnp.zeros((sub, d), F32)
            for s in range((rb + 1) * sub):
                diff = bb - bc[s:s + 1, :]
                if s >= rb * sub:
                    diff = jnp.where(rid >= s - rb * sub, diff, -jnp.inf)
                a = jnp.sum(qb * kc[s:s + 1, :] * jnp.exp(diff), axis=1, keepdims=True)
                o = o + a * vc[s:s + 1, :]
            blocks.append(o)
        return blocks[0] if len(blocks) == 1 else jnp.concatenate(blocks, axis=0)

    def chunk_body(c, carry):
        items = range(nb * HG_HEADS)
        lanes = [slice((i % HG_HEADS) * d, (i % HG_HEADS + 1) * d) for i in items]
        starts = []
        for e in range(nb):
            r0 = e * te + c * chunk
            starts.append(r0 if isinstance(c, int) else pl.multiple_of(r0, chunk))
        rws = [pl.ds(starts[i // HG_HEADS], chunk) for i in items]
        inter, upd, decay = [], [], []
        for i in items:
            sl, rows = lanes[i], rws[i]
            bc = b_scr[rows, sl]
            b_last = bc[chunk - 1:chunk, :]
            kt = (kk_scr[rows, sl] * jnp.exp(b_last - bc)).astype(BF16)
            if transposed:
                inter.append(lax.dot_general(qt_scr[rows, sl], st_scr[i].astype(BF16), NT_DIMS,
                                             preferred_element_type=F32))
                upd.append(lax.dot_general(vb_scr[rows, sl], kt, TN_DIMS, preferred_element_type=F32))
                decay.append(jnp.exp(b_last))
            else:
                inter.append(jnp.dot(qt_scr[rows, sl], st_scr[i].astype(BF16), preferred_element_type=F32))
                upd.append(lax.dot_general(kt, vb_scr[rows, sl], TN_DIMS, preferred_element_type=F32))
                decay.append(jnp.broadcast_to(jnp.exp(b_last), (sub, d)).T[:, 0:1])
        for i in items:
            sl, rows = lanes[i], rws[i]
            o = intra(b_scr[rows, sl], qs_scr[rows, sl], kk_scr[rows, sl], i_ref[rows, sl])
            h_scr[rows, sl] = o + inter[i]
            st_scr[i] = st_scr[i] * decay[i] + upd[i]
        return carry

    nchunk = te // chunk
    if nchunk == 1:
        chunk_body(0, 0)
    else:
        lax.fori_loop(0, nchunk, chunk_body, 0)

    for h in range(HG_HEADS):
        sl = slice(h * d, (h + 1) * d)
        hh = h_scr[:, sl]
        ms = jnp.mean(hh * hh, axis=1, keepdims=True)
        y = hh * lax.rsqrt(ms + 1e-5) * norm_w[:, sl] * _silu(z_ref[:, sl])
        o_ref[:, sl] = y.astype(BF16)

    @pl.when(t == nt - 1)
    def _():
        for e in range(nb):
            for h in range(HG_HEADS):
                s_ref[e, h] = st_scr[e * HG_HEADS + h].T if transposed else st_scr[e * HG_HEADS + h]


ANY_SPEC = pl.BlockSpec(memory_space=pl.ANY)


def _state_in_spec(state, layer, tail, nb=1):
    zeros = (0,) * len(tail)
    if state.ndim == len(tail) + 2:
        return pl.BlockSpec((None, nb) + tail, lambda g, t: (layer, g) + zeros)
    return pl.BlockSpec((nb,) + tail, lambda g, t: (g,) + zeros)


def _state_out_spec(layer, tail, nb=1):
    zeros = (0,) * len(tail)
    return pl.BlockSpec((None, nb) + tail, lambda g, t: (layer, g) + zeros)


class _Rows:
    def __init__(self, row0, batch, seq, te, nb=1):
        self.nb, self.te, self.rows = nb, te, nb * te
        self.groups, self.nt = batch // nb, seq // te
        self.rb0 = row0 // self.rows
        self.grid = (self.groups, self.nt)

    def spec(self, width, col_block):
        groups, rb0 = self.groups, self.rb0
        return pl.BlockSpec((self.rows, width), lambda g, t: (rb0 + t * groups + g, col_block))


def _mix_out_spec(rows, group):
    return rows.spec(GROUP_W, group)


def _sds(x):
    return jax.ShapeDtypeStruct(x.shape, x.dtype)


def _hgrn(p, par, state, mix, acc, layer, rows, chunk):
    nb = rows.nb
    pspec = lambda cb: rows.spec(GROUP_W, cb)
    tail = (HG_HEADS, HG_D, HG_D)
    big = lambda dt: pltpu.VMEM((rows.rows, GROUP_W), dt)
    return pl.pallas_call(
        functools.partial(_hgrn_kernel, chunk=chunk, nb=nb, te=rows.te, nt=rows.nt),
        grid=rows.grid,
        in_specs=[pspec(C_HQ // GROUP_W), pspec(C_HF // GROUP_W), pspec(C_HI // GROUP_W), pspec(C_HZ // GROUP_W),
                  pl.BlockSpec((None, 8, GROUP_W), lambda b, t: (layer, 0, 0)),
                  _state_in_spec(state, layer, tail, nb), ANY_SPEC, ANY_SPEC],
        out_specs=[_mix_out_spec(rows, 0), _state_out_spec(layer, tail, nb)],
        out_shape=[_sds(mix), _sds(acc)],
        input_output_aliases={6: 0, 7: 1},
        scratch_shapes=[pltpu.VMEM((nb * HG_HEADS, HG_D, HG_D), F32),
                        big(F32), big(F32), big(F32), big(BF16), big(BF16), big(F32)],
        compiler_params=_cparams(("parallel", "arbitrary")),
        name="hgrn2",
    )(p, p, p, p, par, state, mix, acc)


def _mlstm_kernel(qk_ref, v_ref, z_ref, g_ref, par_ref, bias_ref, c0_ref, n0_ref, m0_ref,
                  _mix_ref, _acc_c_ref, _acc_n_ref, _acc_m_ref,
                  o_ref, c_ref, n_ref, m_ref, m_scr, h_scr, *, chunk, nb, te, nt):
    t = pl.program_id(1)
    L = chunk
    sub = 8

    @pl.when(t == 0)
    def _():
        c_ref[...] = c0_ref[...]
        n_ref[...] = n0_ref[...]
        m_scr[...] = jnp.zeros(m_scr.shape, F32)
        for e in range(nb):
            for h in range(ML_HEADS):
                m_scr[e * sub + h:e * sub + h + 1, :] = jnp.broadcast_to(m0_ref[e, :, h:h + 1], (1, LANE))

    r = lax.broadcasted_iota(jnp.int32, (L, L), 0)
    c = lax.broadcasted_iota(jnp.int32, (L, L), 1)
    tril = r >= c
    triu = r <= c

    items = [(e, h) for e in range(nb) for h in range(ML_HEADS)]
    nchunk = te // L
    pre = {}
    for ci in range(nchunk):
        for e in range(nb):
            rows = slice(e * te + ci * L, e * te + (ci + 1) * L)
            g = g_ref[rows, :]
            gt = g.T
            for h in range(ML_HEADS):
                ib = bias_ref[:, h:h + 1]
                fb = bias_ref[:, ML_HEADS + h:ML_HEADS + h + 1]
                i_col = g[:, h:h + 1] + ib
                i_row = gt[h:h + 1, 0:L] + ib
                lf_col = _log_sigmoid(g[:, ML_HEADS + h:ML_HEADS + h + 1] + fb)
                lf_row = _log_sigmoid(gt[ML_HEADS + h:ML_HEADS + h + 1, 0:L] + fb)
                b_col = jnp.sum(jnp.where(tril, lf_row, 0.0), axis=1, keepdims=True)
                b_row = jnp.sum(jnp.where(triu, lf_col, 0.0), axis=0, keepdims=True)
                dmat = jnp.where(tril, b_col - b_row + i_row, -jnp.inf)
                q = qk_ref[rows, h * ML_DK:(h + 1) * ML_DK]
                k = qk_ref[rows, ML_HEADS * ML_DK + h * ML_DK:ML_HEADS * ML_DK + (h + 1) * ML_DK] * (ML_DK ** -0.5)
                qb = q.astype(BF16)
                pre[ci, e, h] = dict(
                    rows=rows, i_col=i_col, b_col=b_col, dmat=dmat, dmax=jnp.max(dmat, axis=1, keepdims=True),
                    q=q, k=k, qb=qb, vb=v_ref[rows, h * ML_DV:(h + 1) * ML_DV].astype(BF16),
                    s_qk=lax.dot_general(qb, k.astype(BF16), NT_DIMS, preferred_element_type=F32))

    for ci in range(nchunk):
        mid = {}
        for e, h in items:
            a = pre[ci, e, h]
            m = m_scr[e * sub + h:e * sub + h + 1, 0:1]
            m_t = jnp.maximum(a["b_col"] + m, a["dmax"])
            pm = jnp.exp(a["dmat"] - m_t) * a["s_qk"]
            cst = c_ref[e, h]
            mid[e, h] = dict(m=m, m_t=m_t, pm=pm, cst=cst,
                             pv=jnp.dot(pm.astype(BF16), a["vb"], preferred_element_type=F32),
                             qc=jnp.dot(a["qb"], cst.astype(BF16), preferred_element_type=F32))
        for e, h in items:
            a, u = pre[ci, e, h], mid[e, h]
            m, m_t, b_col = u["m"], u["m_t"], a["b_col"]
            nst = n_ref[e, h:h + 1, :]
            inter = jnp.exp(b_col + m - m_t)
            num = u["pv"] + inter * u["qc"]
            den = jnp.sum(u["pm"], axis=1, keepdims=True) + inter * jnp.sum(a["q"] * nst, axis=1, keepdims=True)
            hh = num / jnp.maximum(jnp.abs(den), jnp.exp(-m_t))
            m_new = m_t[L - 1:L, :]
            b_last = b_col[L - 1:L, :]
            wgt = jnp.exp(b_last - b_col + a["i_col"] - m_new)
            decay = jnp.exp(b_last + m - m_new)
            kw = a["k"] * wgt
            c_ref[e, h] = decay * u["cst"] + lax.dot_general(kw.astype(BF16), a["vb"], TN_DIMS,
                                                             preferred_element_type=F32)
            n_ref[e, h:h + 1, :] = decay * nst + jnp.sum(kw, axis=0, keepdims=True)
            m_scr[e * sub + h:e * sub + h + 1, :] = jnp.broadcast_to(m_new, (1, LANE))
            h_scr[a["rows"], h * ML_DV:(h + 1) * ML_DV] = hh

    for h in range(ML_HEADS):
        sl = slice(h * ML_DV, (h + 1) * ML_DV)
        x = h_scr[:, sl]
        xc = x - jnp.mean(x, axis=1, keepdims=True)
        y = xc * lax.rsqrt(jnp.mean(xc * xc, axis=1, keepdims=True) + 1e-6)
        o_ref[:, sl] = (y * par_ref[:, sl] * _silu(z_ref[:, sl])).astype(BF16)

    @pl.when(t == nt - 1)
    def _():
        for e in range(nb):
            m_ref[e] = m_scr[e * sub:(e + 1) * sub, :]


def _mlstm(p, ps, norm_w, bias, c0, n0, m0, mix, acc_c, acc_n, acc_m, layer, rows, chunk):
    nb = rows.nb
    pspec = lambda cb: rows.spec(GROUP_W, cb)
    c_tail, n_tail, m_tail = (ML_HEADS, ML_DK, ML_DV), (ML_HEADS, ML_DK), (1, ML_HEADS)
    return pl.pallas_call(
        functools.partial(_mlstm_kernel, chunk=chunk, nb=nb, te=rows.te, nt=rows.nt),
        grid=rows.grid,
        in_specs=[pspec(C_MQK // GROUP_W), pspec(C_MV // GROUP_W), pspec(C_MZ // GROUP_W),
                  rows.spec(LANE, CB_MG),
                  pl.BlockSpec((None, 1, GROUP_W), lambda b, t: (layer, 0, 0)),
                  pl.BlockSpec((None, 1, 2 * ML_HEADS), lambda b, t: (layer, 0, 0)),
                  _state_in_spec(c0, layer, c_tail, nb), _state_in_spec(n0, layer, n_tail, nb),
                  _state_in_spec(m0, layer, m_tail, nb), ANY_SPEC, ANY_SPEC, ANY_SPEC, ANY_SPEC],
        out_specs=[_mix_out_spec(rows, 1), _state_out_spec(layer, c_tail, nb), _state_out_spec(layer, n_tail, nb),
                   _state_out_spec(layer, (8, LANE), nb)],
        out_shape=[_sds(mix), _sds(acc_c), _sds(acc_n), _sds(acc_m)],
        input_output_aliases={9: 0, 10: 1, 11: 2, 12: 3},
        scratch_shapes=[pltpu.VMEM((nb * 8, LANE), F32), pltpu.VMEM((rows.rows, GROUP_W), F32)],
        compiler_params=_cparams(("parallel", "arbitrary")),
        name="mlstm",
    )(p, p, p, ps, norm_w, bias, c0, n0, m0, mix, acc_c, acc_n, acc_m)


RW_CHUNK = 16


def _rwkv_kernel(r_ref, k_ref, v_ref, z_ref, x_ref, buf_ref, bufx_ref, mu_ref, mux_ref, par_ref,
                 w2_ref, a2_ref, s0_ref, _mix_ref, _acc_ref, o_ref, s_ref,
                 sp_scr, prev_scr, prevx_scr, kh_scr, rh_scr, ki_scr, ai_scr, kd_scr, ad_scr, vb_scr,
                 gl_scr, y_scr, bonus_scr, coef_scr, mv_scr, cm_scr, *, nb, te, nt):
    t = pl.program_id(1)
    L = RW_CHUNK
    W = GROUP_W
    tc = nb * te
    tep = max(te, L)
    tp = nb * tep

    lane2 = lax.broadcasted_iota(jnp.int32, (LANE, LANE), 1)
    row2 = lax.broadcasted_iota(jnp.int32, (LANE, LANE), 0)
    diag_blocks = (lane2 < RW_HEAD) == (row2 < RW_HEAD)
    seg_ones = jnp.where(diag_blocks, 1.0, 0.0).astype(BF16)

    @pl.when(t == 0)
    def _():
        zero = jnp.zeros((RW_HEAD, RW_HEAD), F32)
        for e in range(nb):
            prev_scr[e:e + 1, :] = buf_ref[e]
            prevx_scr[e:e + 1, :] = bufx_ref[e]
            for pr in range(RW_PAIRS):
                top = jnp.concatenate([s0_ref[e, 2 * pr], zero], axis=1)
                bot = jnp.concatenate([zero, s0_ref[e, 2 * pr + 1]], axis=1)
                sp_scr[e * RW_PAIRS + pr] = jnp.concatenate([top, bot], axis=0)

    def seg_sum(x):
        parts = [_dot_sel_right(x[:, i * LANE:(i + 1) * LANE], seg_ones, 2) for i in range(x.shape[1] // LANE)]
        return jnp.concatenate(parts, axis=1)

    def mix(cur, prev_rows, mu):
        rid = lax.broadcasted_iota(jnp.int32, cur.shape, 0)
        prev = pltpu.roll(cur, 1, axis=0) if tc > 1 else cur
        for e in range(nb):
            prev = jnp.where(rid == e * te, prev_rows[e:e + 1, :], prev)
        return cur + (prev - cur) * mu

    pr_ = r_ref[...]
    pk_ = k_ref[...]
    pv_ = v_ref[...]
    px_ = x_ref[...]
    xr = mix(pr_, prev_scr[:, 0:W], mu_ref[:, 0:W])
    xk = mix(pk_, prev_scr[:, W:2 * W], mu_ref[:, W:2 * W])
    xv = mix(pv_, prev_scr[:, 2 * W:3 * W], mu_ref[:, 2 * W:3 * W])
    xx = mix(px_, prevx_scr[...], mux_ref[...])
    for e in range(nb):
        last = slice((e + 1) * te - 1, (e + 1) * te)
        prev_scr[e:e + 1, 0:W] = pr_[last, :]
        prev_scr[e:e + 1, W:2 * W] = pk_[last, :]
        prev_scr[e:e + 1, 2 * W:3 * W] = pv_[last, :]
        prevx_scr[e:e + 1, :] = px_[last, :]

    w0 = par_ref[0:1, :]
    a0 = par_ref[1:2, :]
    k_k = par_ref[2:3, :]
    k_a = par_ref[3:4, :]
    r_k = par_ref[4:5, :]
    ln_w = par_ref[5:6, :]
    ln_b = par_ref[6:7, :]

    wlin = w0 + jnp.dot(jnp.tanh(xx).astype(BF16), w2_ref[...], preferred_element_type=F32)
    wdec = -_softplus(-wlin) - 0.5
    logd = -jnp.exp(wdec)
    a = _sigmoid(a0 + jnp.dot(xx.astype(BF16), a2_ref[...], preferred_element_type=F32))
    kk = xk * k_k
    kk = kk / jnp.maximum(jnp.sqrt(seg_sum(kk * kk)), 1e-12)
    kp = xk * (1.0 + (a - 1.0) * k_a)
    alpha = a * kk
    bonus = seg_sum(xr * kp * r_k) * xv

    if tep > te:
        def pad(u):
            zeros = jnp.zeros((tep - te, u.shape[1]), F32)
            return jnp.concatenate([piece for e in range(nb) for piece in (u[e * te:(e + 1) * te], zeros)], axis=0)
        logd, kk, kp, alpha, xr, xv = pad(logd), pad(kk), pad(kp), pad(alpha), pad(xr), pad(xv)
    bonus_scr[...] = bonus

    tri, ones = _chunk_masks(tp, L)
    logd_parts = _split_bf16(logd, 3)
    g = sum(jnp.dot(tri, part, preferred_element_type=F32) for part in logd_parts)
    gl = sum(jnp.dot(ones, part, preferred_element_type=F32) for part in logd_parts)
    einv = jnp.exp(-g)
    egl = jnp.exp(gl - g)
    kh_scr[...] = (kk * jnp.exp(g - logd)).astype(BF16)
    rh_scr[...] = (xr * jnp.exp(g)).astype(BF16)
    ki_scr[...] = (kp * einv).astype(BF16)
    ai_scr[...] = (alpha * einv).astype(BF16)
    kd_scr[...] = (kp * egl).astype(BF16)
    ad_scr[...] = (alpha * egl).astype(BF16)
    vb_scr[...] = xv.astype(BF16)
    gl_scr[...] = jnp.exp(gl)

    lane_l = lax.broadcasted_iota(jnp.int32, (L, LANE), 1)
    head_a = lane_l < RW_HEAD
    rl = lax.broadcasted_iota(jnp.int32, (L, L), 0)
    cl = lax.broadcasted_iota(jnp.int32, (L, L), 1)
    incl = rl >= cl
    rl2 = lax.broadcasted_iota(jnp.int32, (2 * L, L), 0) & (L - 1)
    strict2 = rl2 > lax.broadcasted_iota(jnp.int32, (2 * L, L), 1)
    cl_ab = lax.broadcasted_iota(jnp.int32, (L, 2 * L), 1) & (L - 1)
    strict_ab = lax.broadcasted_iota(jnp.int32, (L, 2 * L), 0) > cl_ab
    sp_r = lax.broadcasted_iota(jnp.int32, (2 * L, (L - 1) * LANE), 0)
    sp_c = lax.broadcasted_iota(jnp.int32, (2 * L, (L - 1) * LANE), 1)
    sp_head = jnp.where((sp_c & (LANE - 1)) < RW_HEAD, 0, L)
    spread = jnp.where(sp_r == lax.shift_right_logical(sp_c, 7) + sp_head, 1.0, 0.0).astype(BF16)
    zb = jnp.zeros((L, LANE), BF16)

    pairs = range(nb * RW_PAIRS)
    lanes = [slice((i % RW_PAIRS) * LANE, (i % RW_PAIRS + 1) * LANE) for i in pairs]
    grp = LANE // (2 * L)
    zblk = jnp.zeros((2 * L, LANE), BF16)

    def chunk_rows(c):
        starts = []
        for e in range(nb):
            r0 = e * tep + c * L
            starts.append(r0 if isinstance(c, int) else pl.multiple_of(r0, L))
        return starts, [pl.ds(starts[i // RW_PAIRS], L) for i in pairs]


    def free_first(c):
        _, rws = chunk_rows(c)
        gm, nab = [], []
        for pr in pairs:
            sl, rows = lanes[pr], rws[pr]
            kh = kh_scr[rows, sl]
            rh = rh_scr[rows, sl]
            x4 = jnp.concatenate([jnp.where(head_a, kh, zb), jnp.where(head_a, zb, kh),
                                  jnp.where(head_a, rh, zb), jnp.where(head_a, zb, rh)], axis=0)
            ai = ai_scr[rows, sl]
            y2 = jnp.concatenate([ai, ki_scr[rows, sl]], axis=0)
            gm.append(lax.dot_general(x4, y2, NT_DIMS, preferred_element_type=F32))
            ai2 = jnp.concatenate([jnp.where(head_a, ai, zb), jnp.where(head_a, zb, ai)], axis=0)
            nab.append(lax.dot_general(kh, ai2, NT_DIMS, preferred_element_type=F32))
        return rws, gm, nab

    def free_second(first, slot):
        rws, gm, nab = first
        for pr in pairs:
            g = gm[pr]
            m_ab = jnp.where(strict2, g[0:2 * L, L:2 * L], 0.0)
            mv = jnp.dot(m_ab.astype(BF16), vb_scr[rws[pr], lanes[pr]], preferred_element_type=F32)
            mv_scr[slot, pr] = jnp.where(head_a, mv[0:L], mv[L:2 * L])
            coef_scr[slot, pr] = _dot_sel_right(jnp.where(strict_ab, nab[pr], 0.0), spread, 1)
            cm = jnp.concatenate([
                jnp.concatenate([jnp.where(incl, g[2 * L:3 * L, L:2 * L], 0.0),
                                 -jnp.where(incl, g[2 * L:3 * L, 0:L], 0.0)], axis=1),
                jnp.concatenate([jnp.where(incl, g[3 * L:4 * L, L:2 * L], 0.0),
                                 -jnp.where(incl, g[3 * L:4 * L, 0:L], 0.0)], axis=1)], axis=0)
            cm_scr[slot, pr] = cm.astype(BF16)

    def stage_state_head(c):
        _, rws = chunk_rows(c)
        return [lax.dot_general(jnp.concatenate([kh_scr[rws[pr], lanes[pr]], rh_scr[rws[pr], lanes[pr]]], axis=0),
                                sp_scr[pr].astype(BF16), NT_DIMS, preferred_element_type=F32)
                for pr in pairs]

    def state_solve(c, slot, ks):
        starts, rws = chunk_rows(c)
        vw = []
        for pr in pairs:
            w = ks[pr][0:L] + mv_scr[slot, pr]
            for s in range(L - 1):
                w = w - coef_scr[slot, pr, :, s * LANE:(s + 1) * LANE] * w[s:s + 1, :]
            vw.append(jnp.concatenate([vb_scr[rws[pr], lanes[pr]], w.astype(BF16)], axis=0))
        yy = [jnp.dot(cm_scr[slot, pr], vw[pr], preferred_element_type=F32) for pr in pairs]
        upd = []
        for g0 in range(0, nb * RW_PAIRS, grp):
            members = range(g0, g0 + grp)
            vw_t = jnp.concatenate([vw[pr] for pr in members], axis=0).T
            kd_rows = []
            for q, pr in enumerate(members):
                sl, rows = lanes[pr], rws[pr]
                kd_ad = jnp.concatenate([kd_scr[rows, sl], -ad_scr[rows, sl]], axis=0)
                kd_rows.append(jnp.concatenate([kd_ad if col == q else zblk for col in range(grp)], axis=1))
            u_all = jnp.dot(vw_t, jnp.concatenate(kd_rows, axis=0), preferred_element_type=F32)
            for q in range(grp):
                upd.append(u_all[:, q * LANE:(q + 1) * LANE])
        return starts, rws, ks, yy, upd

    def state_store(solved):
        starts, rws, ks, yy, upd = solved
        for pr in pairs:
            sl, rows = lanes[pr], rws[pr]
            y_scr[rows, sl] = ks[pr][L:2 * L] + jnp.where(head_a, yy[pr][0:L], yy[pr][L:2 * L])
            gl_row = gl_scr[pl.ds(starts[pr // RW_PAIRS], 1), sl]
            sp_scr[pr] = sp_scr[pr] * gl_row + jnp.where(diag_blocks, upd[pr], 0.0)

    def chunk_step(c, slot, c_next):
        ks = stage_state_head(c)
        first = free_first(c_next)
        solved = state_solve(c, slot, ks)
        free_second(first, 1 - slot)
        state_store(solved)

    nchunk = tep // L
    free_second(free_first(0), 0)
    if nchunk == 1:
        state_store(state_solve(0, 0, stage_state_head(0)))
    else:
        assert nchunk % 2 == 0

        def two_chunks(cc, carry):
            c0 = cc * 2
            chunk_step(c0, 0, c0 + 1)
            chunk_step(c0 + 1, 1, jnp.minimum(c0 + 2, nchunk - 1))
            return carry

        lax.fori_loop(0, nchunk // 2, two_chunks, 0)

    if tep > te:
        y = jnp.concatenate([y_scr[e * tep:e * tep + te, :] for e in range(nb)], axis=0)
    else:
        y = y_scr[...]
    yc = y - seg_sum(y) * (1.0 / RW_HEAD)
    yn = yc * lax.rsqrt(seg_sum(yc * yc) * (1.0 / RW_HEAD) + RW_GN_EPS)
    out = (yn * ln_w + ln_b + bonus_scr[...]) * _silu(z_ref[...])
    o_ref[...] = out.astype(BF16)

    @pl.when(t == nt - 1)
    def _():
        for e in range(nb):
            for pr in range(RW_PAIRS):
                sp = sp_scr[e * RW_PAIRS + pr]
                s_ref[e, 2 * pr] = sp[0:RW_HEAD, 0:RW_HEAD]
                s_ref[e, 2 * pr + 1] = sp[RW_HEAD:LANE, RW_HEAD:LANE]


def _rwkv(p, ps, buf, bufx, mu, mux, par, w2p, a2p, state, mix, acc, layer, rows):
    nb, te = rows.nb, rows.te
    tp = nb * max(te, RW_CHUNK)
    pspec = lambda cb: rows.spec(GROUP_W, cb)
    tail = (RW_HEADS, RW_HEAD, RW_HEAD)
    lay = lambda shape: pl.BlockSpec((None,) + shape, lambda b, t: (layer,) + (0,) * len(shape))
    big = lambda dt: pltpu.VMEM((tp, GROUP_W), dt)
    return pl.pallas_call(
        functools.partial(_rwkv_kernel, nb=nb, te=te, nt=rows.nt),
        grid=rows.grid,
        in_specs=[pspec(C_RR // GROUP_W), pspec(C_RK // GROUP_W), pspec(C_RV // GROUP_W), pspec(C_RZ // GROUP_W),
                  rows.spec(LANE, CB_RX),
                  pl.BlockSpec((nb, 1, 3 * GROUP_W), lambda g, t: (g, 0, 0)),
                  pl.BlockSpec((nb, 1, LANE), lambda g, t: (g, 0, 0)),
                  lay((1, 3 * GROUP_W)), lay((1, LANE)), lay((8, GROUP_W)),
                  lay((LANE, GROUP_W)), lay((LANE, GROUP_W)), _state_in_spec(state, layer, tail, nb),
                  ANY_SPEC, ANY_SPEC],
        out_specs=[_mix_out_spec(rows, 2), _state_out_spec(layer, tail, nb)],
        out_shape=[_sds(mix), _sds(acc)],
        input_output_aliases={13: 0, 14: 1},
        scratch_shapes=[pltpu.VMEM((nb * RW_PAIRS, LANE, LANE), F32),
                        pltpu.VMEM((nb, 3 * GROUP_W), F32), pltpu.VMEM((nb, LANE), F32),
                        big(BF16), big(BF16), big(BF16), big(BF16), big(BF16), big(BF16), big(BF16),
                        big(F32), big(F32), pltpu.VMEM((rows.rows, GROUP_W), F32),
                        pltpu.VMEM((2, nb * RW_PAIRS, RW_CHUNK, (RW_CHUNK - 1) * LANE), F32),
                        pltpu.VMEM((2, nb * RW_PAIRS, RW_CHUNK, LANE), F32),
                        pltpu.VMEM((2, nb * RW_PAIRS, 2 * RW_CHUNK, 2 * RW_CHUNK), BF16)],
        compiler_params=_cparams(("parallel", "arbitrary")),
        name="rwkv7",
    )(p, p, p, p, ps, buf, bufx, mu, mux, par, w2p, a2p, state, mix, acc)


def _trunk_layer(l, x_f32, x_bf, w_in_t, w_out_b, hg_par, ml_nw, ml_bias, rw_mu, rw_mux, rw_par, rw_w2, rw_a2,
                 ln_g, ln_b, groups, cfg):
    p = _proj_matmul(x_bf, w_in_t, l, cfg["mm_tm"], cfg["mm_tn"])
    ps = _proj_small(x_bf, w_in_t, l, cfg["mm_tm"])
    mix = jnp.zeros(x_bf.shape, BF16)
    for grp in groups:
        row0, batch, seq, te = grp["row0"], grp["batch"], grp["seq"], grp["te"]
        rows = lambda nb: _Rows(row0, batch, seq, te, nb)
        mix, grp["acc_hg"] = _hgrn(p, hg_par, grp["hg"], mix, grp["acc_hg"], l, rows(grp["hg_nb"]), grp["hg_chunk"])
        mix, grp["acc_c"], grp["acc_n"], grp["acc_m"] = _mlstm(
            p, ps, ml_nw, ml_bias, grp["ml_c"], grp["ml_n"], grp["ml_m"], mix, grp["acc_c"], grp["acc_n"],
            grp["acc_m"], l, rows(grp["ml_nb"]), grp["ml_chunk"])
        mix, grp["acc_rw"] = _rwkv(p, ps, grp["rw_buf"][l], grp["rw_bufx"][l], rw_mu, rw_mux, rw_par, rw_w2, rw_a2,
                                   grp["rw"], mix, grp["acc_rw"], l, rows(grp["rw_nb"]))
        if grp["cache_layout"]:
            mix = _mem_attn_cache(p, grp["mem_k"], grp["mem_v"], mix, l, rows(grp["xa_nb"]))
        else:
            mix = _mem_attn(p, grp["mem_k"], grp["mem_v"], mix, l, rows(grp["xa_nb"]))
        first = row0 + (seq // te - 1) * batch * te + te - 1
        last_rkv = lax.slice(p, (first, C_RR), (row0 + batch * seq, C_RR + 3 * GROUP_W), (te, 1))
        last_x = lax.slice(ps, (first, CB_RX * LANE), (row0 + batch * seq, (CB_RX + 1) * LANE), (te, 1))
        grp["buf_new"].append(jnp.concatenate([last_rkv, last_x], axis=-1))
    x_f32, x_bf = _outproj_then_ln(mix, w_out_b, l, x_f32, ln_g, ln_b, cfg["op_tm"], cfg["op_tn"], cfg["ln_rows"])
    return x_f32, x_bf


def kernel(x_prompt, x_sample, mem_prompt, state_hgrn, state_mlstm_C, state_mlstm_n, state_mlstm_m, state_rwkv, state_rwkv_shift, cache_mem_k, cache_mem_v, w_in, hgrn_lb, hgrn_norm_w, mlstm_ig_b, mlstm_fg_b, mlstm_norm_w, rwkv_mu, rwkv_w0, rwkv_w2, rwkv_a0, rwkv_a2, rwkv_k_k, rwkv_k_a, rwkv_r_k, rwkv_ln_w, rwkv_ln_b, mem_wk, mem_wv, w_out, ln_g, ln_b):
    bp, tp_, _ = x_prompt.shape
    bs, ts, _ = x_sample.shape
    depth = w_in.shape[0]
    mp, ms = bp * tp_, bs * ts

    w_in_t = jnp.swapaxes(w_in, 1, 2)
    w_out_b = w_out.astype(BF16)
    lb_all = jnp.cumsum(jax.nn.softmax(hgrn_lb.astype(F32), axis=0), axis=0)
    lb_all = lb_all - lb_all[0]
    zrow = jnp.zeros_like(lb_all)
    hg_par = jnp.stack([jnp.log(lb_all), jnp.log1p(-lb_all), 1.0 - lb_all, hgrn_norm_w.astype(F32),
                        zrow, zrow, zrow, zrow], axis=1)
    ml_nw = mlstm_norm_w.astype(F32)[:, None, :]
    ml_bias = jnp.concatenate([mlstm_ig_b, mlstm_fg_b], axis=-1).astype(F32)[:, None, :]
    rw_mu = rwkv_mu[:, None, :3 * GROUP_W].astype(F32)
    rw_mux = rwkv_mu[:, None, 3 * GROUP_W:].astype(F32)
    zr = jnp.zeros((depth, GROUP_W), F32)
    rw_par = jnp.stack([rwkv_w0, rwkv_a0, rwkv_k_k, rwkv_k_a, rwkv_r_k.reshape(depth, GROUP_W), rwkv_ln_w,
                        rwkv_ln_b, zr], axis=1).astype(F32)
    zl = jnp.zeros((depth, RW_LORA, GROUP_W), F32)
    rw_w2 = jnp.concatenate([rwkv_w2.astype(F32), zl], axis=1).astype(BF16)
    rw_a2 = jnp.concatenate([zl, rwkv_a2.astype(F32)], axis=1).astype(BF16)
    ln_g3 = ln_g.astype(F32)[:, None, :]
    ln_b3 = ln_b.astype(F32)[:, None, :]

    mem_x = mem_prompt.reshape(bp * N_MEM, D_MODEL).astype(BF16)
    mk_p = _matmul_layers(mem_x, mem_wk, 256)
    mv_p = _matmul_layers(mem_x, mem_wv, 256)
    mk_out = mk_p.reshape(depth, bp, N_MEM, XA_HEADS, XA_DH)
    mv_out = mv_p.reshape(depth, bp, N_MEM, XA_HEADS, XA_DH)

    def split_buf(buf):
        return buf[:, :, None, :3 * GROUP_W].astype(F32), buf[:, :, None, 3 * GROUP_W:].astype(F32)

    def cache_view(c):
        c = c.reshape(depth, bs, N_MEM, XA_HEADS, XA_SUB, LANE)
        return jnp.transpose(c, (0, 1, 2, 4, 3, 5)).reshape(depth, bs * XA_ROWS, LANE)

    def results(b):
        return dict(acc_hg=jnp.zeros((depth, b, HG_HEADS, HG_D, HG_D), F32),
                    acc_c=jnp.zeros((depth, b, ML_HEADS, ML_DK, ML_DV), F32),
                    acc_n=jnp.zeros((depth, b, ML_HEADS, ML_DK), F32),
                    acc_m=jnp.zeros((depth, b, 8, LANE), F32),
                    acc_rw=jnp.zeros((depth, b, RW_HEADS, RW_HEAD, RW_HEAD), F32), buf_new=[])

    zbuf, zbufx = split_buf(jnp.zeros((depth, bp, RW_SHIFT_W), F32))
    sbuf, sbufx = split_buf(state_rwkv_shift)
    te_p = 128
    prompt = dict(row0=0, batch=bp, seq=tp_, te=te_p, hg_nb=1, ml_nb=4, xa_nb=4,
                  hg=jnp.zeros((bp, HG_HEADS, HG_D, HG_D), F32), hg_chunk=16,
                  ml_c=jnp.zeros((bp, ML_HEADS, ML_DK, ML_DV), F32), ml_n=jnp.zeros((bp, ML_HEADS, ML_DK), F32),
                  ml_m=jnp.zeros((bp, 1, ML_HEADS), F32), ml_chunk=64,
                  rw=jnp.zeros((bp, RW_HEADS, RW_HEAD, RW_HEAD), F32), rw_buf=zbuf, rw_bufx=zbufx, rw_nb=2,
                  mem_k=mk_p, mem_v=mv_p, cache_layout=False, **results(bp))
    sample = dict(row0=mp, batch=bs, seq=ts, te=ts, hg_nb=8, ml_nb=8, xa_nb=4,
                  hg=state_hgrn, hg_chunk=ts,
                  ml_c=state_mlstm_C, ml_n=state_mlstm_n, ml_m=state_mlstm_m[:, :, None, :], ml_chunk=ts,
                  rw=state_rwkv, rw_buf=sbuf, rw_bufx=sbufx, rw_nb=8,
                  mem_k=cache_view(cache_mem_k), mem_v=cache_view(cache_mem_v), cache_layout=True, **results(bs))
    cfg = dict(mm_tm=1536, mm_tn=512, op_tm=1536, op_tn=512, ln_rows=512)

    nblk = tp_ // te_p
    xp = jnp.transpose(x_prompt.reshape(bp, nblk, te_p, D_MODEL), (1, 0, 2, 3)).reshape(mp, D_MODEL)
    x_f32 = jnp.concatenate([xp, x_sample.reshape(ms, D_MODEL)], axis=0).astype(F32)
    x_bf = x_f32.astype(BF16)
    for l in range(depth):
        x_f32, x_bf = _trunk_layer(l, x_f32, x_bf, w_in_t, w_out_b, hg_par, ml_nw, ml_bias, rw_mu, rw_mux,
                                   rw_par, rw_w2, rw_a2, ln_g3, ln_b3, [prompt, sample], cfg)

    def states(g):
        return (g["acc_hg"], g["acc_c"], g["acc_n"], g["acc_m"][:, :, :ML_HEADS, 0], g["acc_rw"],
                jnp.stack(g["buf_new"], axis=0))

    y_prompt = jnp.transpose(x_f32[:mp].reshape(nblk, bp, te_p, D_MODEL), (1, 0, 2, 3)).reshape(bp, tp_, D_MODEL)
    y_sample = x_f32[mp:].reshape(bs, ts, D_MODEL)
    return (y_prompt, y_sample) + states(prompt) + (mk_out, mv_out) + states(sample)
```
